```python
import jax, jax.numpy as jnp
from jax import lax
import numpy as np

D_MODEL = 1024
BATCH = 16
SEQ = 4096
DEPTH = 1
DEC_BATCH = 8
DEC_SEQ = 16
PAST_LEN = 1024

CHUNK = 64
BAND_CHUNKS = 8
WINDOW_A = BAND_CHUNKS * CHUNK
N_HEADS_A = 8
HEAD_DIM_A = 64
W_A = N_HEADS_A * HEAD_DIM_A
REL_CLIP = 128
N_HEADS_B = 4
DK_B = 128
DV_B = 256
QK_B = N_HEADS_B * DK_B
V_B = N_HEADS_B * DV_B
ROPE_BASE = 10000.0
N_GROUPS = 4
EXPERTS_PER_GROUP = 8
N_EXPERTS = N_GROUPS * EXPERTS_PER_GROUP
TOP_K = 2
D_EXPERT = 512
D_PLE = 256
EPS = 1e-6
_IN_SIZES = (W_A, W_A, W_A, QK_B, QK_B, V_B, V_B, D_MODEL, D_MODEL)
IN_WIDTH = sum(_IN_SIZES)
_IN_CUTS = [int(c) for c in np.cumsum(_IN_SIZES)[:-1]]

kernel_name = "hybrid_chunkband_retention_hmoe_stream_step"

F32 = jnp.float32


def rmsnorm(x, g):
    xf = x.astype(F32)
    y = xf * lax.rsqrt(jnp.mean(xf * xf, axis=-1, keepdims=True) + EPS)
    return (y * g.astype(F32)).astype(x.dtype)


def rope(x, pos):
    half = x.shape[-1] // 2
    freqs = ROPE_BASE ** (-jnp.arange(half, dtype=F32) / half)
    ang = pos.astype(F32)[:, None] * freqs[None, :]
    cos = jnp.cos(ang)[:, None, :]
    sin = jnp.sin(ang)[:, None, :]
    xf = x.astype(F32)
    x1, x2 = xf[..., :half], xf[..., half:]
    return jnp.concatenate([x1 * cos - x2 * sin, x2 * cos + x1 * sin], axis=-1).astype(x.dtype)


def rel_bias_block(table, n_q, n_k, offset):
    dist = jnp.arange(n_q)[:, None] + offset - jnp.arange(n_k)[None, :]
    idx = jnp.clip(dist, -REL_CLIP, REL_CLIP) + REL_CLIP
    return table[:, idx].astype(F32)


def attend(q, k, v, bias, valid=None):
    s = jnp.einsum('bqhd,bkhd->bhqk', q, k).astype(F32) * (HEAD_DIM_A ** -0.5) + bias[None]
    if valid is not None:
        s = jnp.where(valid[None, None, None, :], s, jnp.float32(-1e30))
    p = jax.nn.softmax(s, axis=-1)
    return jnp.einsum('bhqk,bkhd->bqhd', p.astype(v.dtype), v)


def band_attention_prompt(q, k, v, table):
    B, S, H, hd = q.shape
    nc = S // CHUNK
    band = WINDOW_A + CHUNK
    pad = ((0, 0), (WINDOW_A, 0), (0, 0), (0, 0))
    kp = jnp.pad(k, pad)
    vp = jnp.pad(v, pad)
    qc = q.reshape(B, nc, CHUNK, H, hd).transpose(1, 0, 2, 3, 4)
    bias = rel_bias_block(table, CHUNK, band, WINDOW_A)
    key_idx = jnp.arange(band)

    def one_chunk(args):
        c, qb = args
        kb = lax.dynamic_slice_in_dim(kp, c * CHUNK, band, axis=1)
        vb = lax.dynamic_slice_in_dim(vp, c * CHUNK, band, axis=1)
        valid = key_idx >= (BAND_CHUNKS - c) * CHUNK
        return attend(qb, kb, vb, bias, valid)

    out = lax.map(one_chunk, (jnp.arange(nc), qc))
    return out.transpose(1, 0, 2, 3, 4).reshape(B, S, H, hd)


def band_attention_sample(q, k, v, cache_k, cache_v, table):
    L = cache_k.shape[1]
    n = q.shape[1]
    k_all = jnp.concatenate([cache_k.astype(k.dtype), k], axis=1)
    v_all = jnp.concatenate([cache_v.astype(v.dtype), v], axis=1)
    bias = rel_bias_block(table, n, L + n, L)
    return attend(q, k_all, v_all, bias)


def retention_block(q, k, v, state, log_g):
    T = q.shape[1]
    idx = jnp.arange(T, dtype=F32)
    diff = idx[:, None] - idx[None, :]
    dmask = jnp.where(diff[None] >= 0, jnp.exp(log_g[:, None, None] * jnp.maximum(diff, 0.0)[None]), 0.0)
    scores = jnp.einsum('bihd,bjhd->bhij', q, k) * dmask[None]
    o = jnp.einsum('bhij,bjhe->bihe', scores, v)
    q_decay = jnp.exp(log_g[None, :] * (idx[:, None] + 1.0))
    o = o + jnp.einsum('bihd,bhde->bihe', q, state) * q_decay[None, :, :, None]
    k_decay = jnp.exp(log_g[None, :] * (T - 1.0 - idx[:, None]))
    new_state = state * jnp.exp(log_g * T)[None, :, None, None] + jnp.einsum(
        'bjhd,bjhe->bhde', k * k_decay[None, :, :, None], v)
    return o, new_state


def retention_prompt(q, k, v, log_g):
    B, S, H, dk = q.shape
    nc = S // CHUNK

    def to_chunks(t):
        return t.astype(F32).reshape(B, nc, CHUNK, H, t.shape[-1]).transpose(1, 0, 2, 3, 4)

    def step(s, qkv):
        o, s = retention_block(qkv[0], qkv[1], qkv[2], s, log_g)
        return s, o

    s0 = jnp.zeros((B, H, dk, DV_B), F32)
    s_final, o = lax.scan(step, s0, (to_chunks(q), to_chunks(k), to_chunks(v)))
    return o.transpose(1, 0, 2, 3, 4).reshape(B, S, H, DV_B), s_final


def project_in(x, g_norm, w_in, pos):
    B, S, _ = x.shape
    h = rmsnorm(x, g_norm)
    z = h @ w_in
    qa, ka, va, qb, kb, vb, gb, gate_a, gate_b = jnp.split(z, _IN_CUTS, axis=-1)
    qa = qa.reshape(B, S, N_HEADS_A, HEAD_DIM_A)
    ka = ka.reshape(B, S, N_HEADS_A, HEAD_DIM_A)
    va = va.reshape(B, S, N_HEADS_A, HEAD_DIM_A)
    qb = rope(qb.reshape(B, S, N_HEADS_B, DK_B), pos)
    kb = rope(kb.reshape(B, S, N_HEADS_B, DK_B), pos) * (DK_B ** -0.5)
    vb = vb.reshape(B, S, N_HEADS_B, DV_B)
    return qa, ka, va, qb, kb, vb, gb, gate_a, gate_b


def merge_out(att_a, ret_b, gb, gate_a, gate_b, ret_norm_g, w_proj_a, w_proj_b, w_out, dtype):
    B, S = att_a.shape[:2]
    a = att_a.reshape(B, S, W_A) @ w_proj_a
    mu = jnp.mean(ret_b, axis=-1, keepdims=True)
    var = jnp.mean(jnp.square(ret_b - mu), axis=-1, keepdims=True)
    rb = ((ret_b - mu) * lax.rsqrt(var + EPS)).reshape(B, S, V_B) * ret_norm_g.astype(F32)
    b = (jax.nn.silu(gb) * rb.astype(dtype)) @ w_proj_b
    m = jax.nn.sigmoid(gate_a) * a + jax.nn.sigmoid(gate_b) * b
    return m @ w_out


def hier_moe(h, w_rg, b_rg, w_re, b_re, w_gate, w_up, w_down):
    N = h.shape[0]
    rows = jnp.arange(N)
    lg = (h @ w_rg).astype(F32) + b_rg.astype(F32)
    grp = jnp.argmax(lg, axis=-1)
    p_grp = jax.nn.softmax(lg, axis=-1)[rows, grp]
    le = ((h @ w_re).astype(F32) + b_re.astype(F32)).reshape(N, N_GROUPS, EXPERTS_PER_GROUP)
    top_l, top_i = lax.top_k(le[rows, grp], TOP_K)
    wts = jax.nn.softmax(top_l, axis=-1) * p_grp[:, None]
    flat_e = (grp[:, None] * EXPERTS_PER_GROUP + top_i).reshape(-1)
    order = jnp.argsort(flat_e)
    tok = order // TOP_K
    xs = h[tok]
    sizes = jnp.bincount(flat_e, length=N_EXPERTS).astype(jnp.int32)
    g = lax.ragged_dot(xs, w_gate, sizes)
    u = lax.ragged_dot(xs, w_up, sizes)
    y = lax.ragged_dot(jax.nn.silu(g) * u, w_down, sizes)
    y = y * wts.reshape(-1)[order][:, None].astype(y.dtype)
    return jnp.zeros_like(h).at[tok].add(y)


def channel_and_ple(x, p, g_ffn, w_rg, b_rg, w_re, b_re, w_gate, w_up, w_down,
                    g_ple, w_ple_gate, w_ple_proj):
    B, S, D = x.shape
    h = rmsnorm(x, g_ffn).reshape(B * S, D)
    x = x + hier_moe(h, w_rg, b_rg, w_re, b_re, w_gate, w_up, w_down).reshape(B, S, D)
    e = (p.astype(x.dtype) @ w_ple_proj) * jax.nn.sigmoid(rmsnorm(x, g_ple) @ w_ple_gate)
    return x + e


def setup_inputs(seed: int = 0) -> dict:
    key = jax.random.key(seed)
    ks = jax.random.split(key, 32)
    n = lambda k, s, sc: jax.random.normal(k, s, F32) * sc
    L_A = min(WINDOW_A, PAST_LEN)
    return {
        "x_prompt": n(ks[0], (BATCH, SEQ, D_MODEL), 1.0),
        "x_sample": n(ks[1], (DEC_BATCH, DEC_SEQ, D_MODEL), 1.0),
        "cache_k_a": n(ks[2], (DEPTH, DEC_BATCH, L_A, N_HEADS_A, HEAD_DIM_A), 1.0),
        "cache_v_a": n(ks[3], (DEPTH, DEC_BATCH, L_A, N_HEADS_A, HEAD_DIM_A), 1.0),
        "state_ret": n(ks[4], (DEPTH, DEC_BATCH, N_HEADS_B, DK_B, DV_B), 0.3),
        "p_prompt": n(ks[5], (DEPTH, BATCH, SEQ, D_PLE), 1.0),
        "p_sample": n(ks[6], (DEPTH, DEC_BATCH, DEC_SEQ, D_PLE), 1.0),
        "norm_mix_g": 1.0 + n(ks[7], (DEPTH, D_MODEL), 0.01),
        "w_in": n(ks[8], (DEPTH, D_MODEL, IN_WIDTH), D_MODEL ** -0.5),
        "rel_bias": n(ks[9], (DEPTH, N_HEADS_A, 2 * REL_CLIP + 1), 0.1),
        "ret_norm_g": 1.0 + n(ks[10], (DEPTH, V_B), 0.01),
        "w_proj_a": n(ks[11], (DEPTH, W_A, D_MODEL), W_A ** -0.5),
        "w_proj_b": n(ks[12], (DEPTH, V_B, D_MODEL), V_B ** -0.5),
        "w_out": n(ks[13], (DEPTH, D_MODEL, D_MODEL), D_MODEL ** -0.5),
        "norm_ffn_g": 1.0 + n(ks[14], (DEPTH, D_MODEL), 0.01),
        "w_router_group": n(ks[15], (DEPTH, D_MODEL, N_GROUPS), D_MODEL ** -0.5),
        "b_router_group": n(ks[16], (DEPTH, N_GROUPS), 0.01),
        "w_router_expert": n(ks[17], (DEPTH, D_MODEL, N_EXPERTS), D_MODEL ** -0.5),
        "b_router_expert": n(ks[18], (DEPTH, N_EXPERTS), 0.01),
        "w_gate_e": n(ks[19], (DEPTH, N_EXPERTS, D_MODEL, D_EXPERT), D_MODEL ** -0.5),
        "w_up_e": n(ks[20], (DEPTH, N_EXPERTS, D_MODEL, D_EXPERT), D_MODEL ** -0.5),
        "w_down_e": n(ks[21], (DEPTH, N_EXPERTS, D_EXPERT, D_MODEL), D_EXPERT ** -0.5),
        "norm_ple_g": 1.0 + n(ks[22], (DEPTH, D_MODEL), 0.01),
        "w_ple_gate": n(ks[23], (DEPTH, D_MODEL, D_MODEL), D_MODEL ** -0.5),
        "w_ple_proj": n(ks[24], (DEPTH, D_PLE, D_MODEL), D_PLE ** -0.5),
        "final_norm_g": 1.0 + n(ks[25], (D_MODEL,), 0.01),
    }


def reference(x_prompt, x_sample, cache_k_a, cache_v_a, state_ret, p_prompt, p_sample,
              norm_mix_g, w_in, rel_bias, ret_norm_g, w_proj_a, w_proj_b, w_out,
              norm_ffn_g, w_router_group, b_router_group, w_router_expert, b_router_expert,
              w_gate_e, w_up_e, w_down_e, norm_ple_g, w_ple_gate, w_ple_proj, final_norm_g):
    S_p = x_prompt.shape[1]
    S_s = x_sample.shape[1]
    pos_p = jnp.arange(S_p)
    pos_s = PAST_LEN + jnp.arange(S_s)
    log_g = jnp.log(1.0 - 2.0 ** (-5.0 - jnp.arange(N_HEADS_B, dtype=F32)))
    keep = min(WINDOW_A, S_p)
    xp, xs = x_prompt, x_sample
    kp_l, vp_l, sp_l, ks_l, vs_l, ss_l = [], [], [], [], [], []
    for i in range(DEPTH):
        qa, ka, va, qb, kb, vb, gb, ga, gbt = project_in(xp, norm_mix_g[i], w_in[i], pos_p)
        att = band_attention_prompt(qa, ka, va, rel_bias[i])
        ret, s_new = retention_prompt(qb, kb, vb, log_g)
        xp = xp + merge_out(att, ret, gb, ga, gbt, ret_norm_g[i], w_proj_a[i], w_proj_b[i],
                            w_out[i], xp.dtype)
        xp = channel_and_ple(xp, p_prompt[i], norm_ffn_g[i], w_router_group[i], b_router_group[i],
                             w_router_expert[i], b_router_expert[i], w_gate_e[i], w_up_e[i],
                             w_down_e[i], norm_ple_g[i], w_ple_gate[i], w_ple_proj[i])
        kp_l.append(ka[:, S_p - keep:])
        vp_l.append(va[:, S_p - keep:])
        sp_l.append(s_new)
        qa, ka, va, qb, kb, vb, gb, ga, gbt = project_in(xs, norm_mix_g[i], w_in[i], pos_s)
        att = band_attention_sample(qa, ka, va, cache_k_a[i], cache_v_a[i], rel_bias[i])
        ret, s_upd = retention_block(qb.astype(F32), kb.astype(F32), vb.astype(F32),
                                     state_ret[i].astype(F32), log_g)
        xs = xs + merge_out(att, ret, gb, ga, gbt, ret_norm_g[i], w_proj_a[i], w_proj_b[i],
                            w_out[i], xs.dtype)
        xs = channel_and_ple(xs, p_sample[i], norm_ffn_g[i], w_router_group[i], b_router_group[i],
                             w_router_expert[i], b_router_expert[i], w_gate_e[i], w_up_e[i],
                             w_down_e[i], norm_ple_g[i], w_ple_gate[i], w_ple_proj[i])
        ks_l.append(ka)
        vs_l.append(va)
        ss_l.append(s_upd.astype(state_ret.dtype))
    y_prompt = rmsnorm(xp, final_norm_g)
    y_sample = rmsnorm(xs, final_norm_g)
    return (y_prompt, y_sample, jnp.stack(kp_l), jnp.stack(vp_l), jnp.stack(sp_l),
            jnp.stack(ks_l), jnp.stack(vs_l), jnp.stack(ss_l))
```

```python
import functools

import jax
import jax.numpy as jnp
from jax import lax
from jax.experimental import pallas as pl
from jax.experimental.pallas import tpu as pltpu

F32 = jnp.float32
BF16 = jnp.bfloat16

D_MODEL = 1024
PAST_LEN = 1024
CHUNK = 64
BAND_CHUNKS = 8
WINDOW_A = BAND_CHUNKS * CHUNK
N_HEADS_A = 8
HEAD_DIM_A = 64
W_A = N_HEADS_A * HEAD_DIM_A
REL_CLIP = 128
N_HEADS_B = 4
DK_B = 128
DV_B = 256
QK_B = N_HEADS_B * DK_B
V_B = N_HEADS_B * DV_B
ROPE_BASE = 10000.0
N_GROUPS = 4
EXPERTS_PER_GROUP = 8
N_EXPERTS = N_GROUPS * EXPERTS_PER_GROUP
TOP_K = 2
D_EXPERT = 512
D_PLE = 256
EPS = 1e-6
_IN_SIZES = (W_A, W_A, W_A, QK_B, QK_B, V_B, V_B, D_MODEL, D_MODEL)
_IN_OFFS = tuple(sum(_IN_SIZES[:i]) for i in range(len(_IN_SIZES) + 1))

LANES = 128
ATT_QBLK = 2 * CHUNK
ATT_WIN = (BAND_CHUNKS + 2) * CHUNK
ROUTER_ROWS = 8 + N_EXPERTS
NEG_BIG = -1e30
VMEM_LIMIT = 56 * 1024 * 1024


def _params(sem):
    return pltpu.CompilerParams(dimension_semantics=sem, vmem_limit_bytes=VMEM_LIMIT)


def _rms(x, g):
    return x * lax.rsqrt(jnp.mean(x * x, axis=-1, keepdims=True) + EPS) * g


def _proj_kernel(x_ref, g_ref, w_ref, wkt_ref, cos_ref, sin_ref, cost_ref, sint_ref,
                 qa_ref, ka_ref, va_ref, qb_ref, kbt_ref, vb_ref, gb_ref, ka32_ref, va32_ref,
                 *, tiles_per_keep):
    h = _rms(x_ref[...], g_ref[...]).astype(BF16)

    def seg(lo, hi):
        return jnp.dot(h, w_ref[:, lo:hi], preferred_element_type=F32)

    qa_ref[...] = seg(0, W_A).astype(BF16)
    ka = seg(W_A, 2 * W_A)
    va = seg(2 * W_A, 3 * W_A)
    ka_ref[...] = ka.astype(BF16)
    va_ref[...] = va.astype(BF16)

    @pl.when(pl.program_id(0) % tiles_per_keep == tiles_per_keep - 1)
    def _():
        ka32_ref[...] = ka
        va32_ref[...] = va

    qb = seg(3 * W_A, 3 * W_A + QK_B)
    cos = cos_ref[...]
    sin = sin_ref[...]
    for hd in range(N_HEADS_B):
        xh = qb[:, hd * DK_B:(hd + 1) * DK_B]
        qb_ref[:, hd * DK_B:(hd + 1) * DK_B] = (xh * cos + pltpu.roll(xh, DK_B // 2, axis=1) * sin).astype(BF16)

    vb_ref[...] = seg(3 * W_A + QK_B, 3 * W_A + QK_B + V_B).astype(BF16)
    gb_ref[...] = seg(3 * W_A + QK_B + V_B, 3 * W_A + QK_B + 2 * V_B)

    kt = lax.dot_general(wkt_ref[...], h, (((1,), (1,)), ((), ())), preferred_element_type=F32)
    cost = cost_ref[...]
    sint = sint_ref[...]
    half = DK_B // 2
    scale = DK_B ** -0.5
    for hd in range(N_HEADS_B):
        x1 = kt[hd * DK_B:hd * DK_B + half, :]
        x2 = kt[hd * DK_B + half:(hd + 1) * DK_B, :]
        kbt_ref[0, hd * DK_B:hd * DK_B + half, :] = (x1 * cost - x2 * sint) * scale
        kbt_ref[0, hd * DK_B + half:(hd + 1) * DK_B, :] = (x2 * cost + x1 * sint) * scale


def _rope_tables(pos):
    half = DK_B // 2
    freqs = ROPE_BASE ** (-jnp.arange(half, dtype=F32) / half)
    ang = pos.astype(F32)[:, None] * freqs[None, :]
    cos = jnp.cos(ang)
    sin = jnp.sin(ang)
    return (jnp.concatenate([cos, cos], axis=1), jnp.concatenate([-sin, sin], axis=1), cos.T, sin.T)


def _project(x2d, pos_rows, g_norm, w_main, w_kt, *, tm, tiles_per_keep):
    n = x2d.shape[0]
    period = pos_rows.shape[0]
    nt = n // tm
    ppt = period // tm
    cos2, sin2, cost, sint = _rope_tables(pos_rows)
    n_keep = n // tiles_per_keep
    row = lambda i: (i, 0)
    const = lambda i: (0, 0)
    outs = (
        jax.ShapeDtypeStruct((n, W_A), BF16), jax.ShapeDtypeStruct((n, W_A), BF16),
        jax.ShapeDtypeStruct((n, W_A), BF16), jax.ShapeDtypeStruct((n, QK_B), BF16),
        jax.ShapeDtypeStruct((nt, QK_B, tm), F32), jax.ShapeDtypeStruct((n, V_B), BF16),
        jax.ShapeDtypeStruct((n, V_B), F32),
        jax.ShapeDtypeStruct((n_keep, W_A), F32), jax.ShapeDtypeStruct((n_keep, W_A), F32),
    )
    keep_spec = pl.BlockSpec((tm, W_A), lambda i: (i // tiles_per_keep, 0))
    return pl.pallas_call(
        functools.partial(_proj_kernel, tiles_per_keep=tiles_per_keep),
        grid=(nt,),
        in_specs=[
            pl.BlockSpec((tm, D_MODEL), row),
            pl.BlockSpec((1, D_MODEL), const),
            pl.BlockSpec(w_main.shape, const),
            pl.BlockSpec(w_kt.shape, const),
            pl.BlockSpec((tm, DK_B), lambda i: (i % ppt, 0)),
            pl.BlockSpec((tm, DK_B), lambda i: (i % ppt, 0)),
            pl.BlockSpec((DK_B // 2, tm), lambda i: (0, i % ppt)),
            pl.BlockSpec((DK_B // 2, tm), lambda i: (0, i % ppt)),
        ],
        out_specs=(
            pl.BlockSpec((tm, W_A), row), pl.BlockSpec((tm, W_A), row), pl.BlockSpec((tm, W_A), row),
            pl.BlockSpec((tm, QK_B), row), pl.BlockSpec((1, QK_B, tm), lambda i: (i, 0, 0)),
            pl.BlockSpec((tm, V_B), row), pl.BlockSpec((tm, V_B), row), keep_spec, keep_spec,
        ),
        out_shape=outs,
        compiler_params=_params(("arbitrary",)),
        name="proj",
    )(x2d, g_norm.reshape(1, D_MODEL), w_main, w_kt, cos2, sin2, cost, sint)


def _attend_pairs(q_of, k_of, v_of, bias_of, store):
    for hp in range(N_HEADS_A // 2):
        qp = q_of(hp)
        kw = k_of(hp)
        vw = v_of(hp)
        lane = lax.broadcasted_iota(jnp.int32, qp.shape, 1)
        outs = []
        for hh in range(2):
            in_head = (lane >= hh * HEAD_DIM_A) & (lane < (hh + 1) * HEAD_DIM_A)
            qh = jnp.where(in_head, qp, jnp.zeros_like(qp))
            s = lax.dot_general(qh, kw, (((1,), (1,)), ((), ())), preferred_element_type=F32)
            s = s * (HEAD_DIM_A ** -0.5) + bias_of(2 * hp + hh)
            m = jnp.max(s, axis=-1, keepdims=True)
            p = jnp.exp(s - m)
            l = jnp.sum(p, axis=-1, keepdims=True)
            o = jnp.dot(p.astype(BF16), vw, preferred_element_type=F32)
            outs.append(o / l)
        lane_o = lax.broadcasted_iota(jnp.int32, outs[0].shape, 1)
        store(hp, jnp.where(lane_o < HEAD_DIM_A, outs[0], outs[1]).astype(BF16))


def _attn_prompt_kernel(q_ref, k_ref, v_ref, bias_ref, o_ref):
    j = pl.program_id(1)
    start = pl.multiple_of(jnp.maximum(2 * j - BAND_CHUNKS, 0) * CHUNK, CHUNK)

    def sl(hp):
        return slice(hp * LANES, (hp + 1) * LANES)

    def store(hp, val):
        o_ref[0, :, sl(hp)] = val

    _attend_pairs(
        lambda hp: q_ref[0, :, sl(hp)],
        lambda hp: k_ref[0, pl.ds(start, ATT_WIN), sl(hp)],
        lambda hp: v_ref[0, pl.ds(start, ATT_WIN), sl(hp)],
        lambda hd: bias_ref[0, hd],
        store)


def _band_bias(table):
    i = jnp.arange(ATT_QBLK)[:, None]
    jk = jnp.arange(ATT_WIN)[None, :]
    out = []
    for v in range(BAND_CHUNKS // 2 + 1):
        off_chunks = 2 * v if v < BAND_CHUNKS // 2 else BAND_CHUNKS
        dist = off_chunks * CHUNK + i - jk
        dchunk = (off_chunks + i // CHUNK) - jk // CHUNK
        valid = (dchunk >= 0) & (dchunk <= BAND_CHUNKS)
        idx = jnp.clip(dist, -REL_CLIP, REL_CLIP) + REL_CLIP
        b = table[:, idx].astype(F32)
        out.append(jnp.where(valid[None], b, jnp.float32(NEG_BIG)))
    return jnp.stack(out)


def _attention_prompt(qa, ka, va, table):
    b, s, _ = qa.shape
    nq = s // ATT_QBLK
    bias = _band_bias(table)
    nvar = bias.shape[0]
    return pl.pallas_call(
        _attn_prompt_kernel,
        grid=(b, nq),
        in_specs=[
            pl.BlockSpec((1, ATT_QBLK, W_A), lambda bi, j: (bi, j, 0)),
            pl.BlockSpec((1, s, W_A), lambda bi, j: (bi, 0, 0)),
            pl.BlockSpec((1, s, W_A), lambda bi, j: (bi, 0, 0)),
            pl.BlockSpec((1, N_HEADS_A, ATT_QBLK, ATT_WIN), lambda bi, j: (jnp.minimum(j, nvar - 1), 0, 0, 0)),
        ],
        out_specs=pl.BlockSpec((1, ATT_QBLK, W_A), lambda bi, j: (bi, j, 0)),
        out_shape=jax.ShapeDtypeStruct((b, s, W_A), BF16),
        compiler_params=_params(("arbitrary", "arbitrary")),
        name="attn_prompt",
    )(qa, ka, va, bias)


def _attn_sample_kernel(q_ref, k_ref, v_ref, bias_ref, o_ref):
    def sl(hp):
        return slice(hp * LANES, (hp + 1) * LANES)

    def store(hp, val):
        o_ref[0, :, sl(hp)] = val

    _attend_pairs(
        lambda hp: q_ref[0, :, sl(hp)],
        lambda hp: k_ref[0, :, sl(hp)],
        lambda hp: v_ref[0, :, sl(hp)],
        lambda hd: bias_ref[hd],
        store)


def _attention_sample(qa, k_all, v_all, table, n_cache):
    b, n, _ = qa.shape
    nk = k_all.shape[1]
    dist = jnp.arange(n)[:, None] + n_cache - jnp.arange(nk)[None, :]
    bias = table[:, jnp.clip(dist, -REL_CLIP, REL_CLIP) + REL_CLIP].astype(F32)
    return pl.pallas_call(
        _attn_sample_kernel,
        grid=(b,),
        in_specs=[
            pl.BlockSpec((1, n, W_A), lambda bi: (bi, 0, 0)),
            pl.BlockSpec((1, nk, W_A), lambda bi: (bi, 0, 0)),
            pl.BlockSpec((1, nk, W_A), lambda bi: (bi, 0, 0)),
            pl.BlockSpec((N_HEADS_A, n, nk), lambda bi: (0, 0, 0)),
        ],
        out_specs=pl.BlockSpec((1, n, W_A), lambda bi: (bi, 0, 0)),
        out_shape=jax.ShapeDtypeStruct((b, n, W_A), BF16),
        compiler_params=_params(("arbitrary",)),
        name="attn_sample",
    )(qa, k_all, v_all, bias)


def _ret_kernel(gt_ref, q_ref, kt_ref, v_ref, gb_ref, s0_ref, dmask_ref, qd_ref, kd_ref, gn_ref,
                out_ref, state_ref):
    @pl.when(pl.program_id(1) == 0)
    def _():
        state_ref[...] = s0_ref[...]

    for hd in range(N_HEADS_B):
        qs = slice(hd * DK_B, (hd + 1) * DK_B)
        vs = slice(hd * DV_B, (hd + 1) * DV_B)
        q = q_ref[0, :, qs]
        kt = kt_ref[0, qs, :]
        v = v_ref[0, :, vs]
        state = state_ref[0, hd]
        scores = jnp.dot(q, kt.astype(BF16), preferred_element_type=F32) * dmask_ref[hd]
        o = jnp.dot(scores.astype(BF16), v, preferred_element_type=F32)
        o = o + jnp.dot(q, state.astype(BF16), preferred_element_type=F32) * qd_ref[hd]
        kd = (kt * kd_ref[hd]).astype(BF16)
        state_ref[0, hd] = state * gt_ref[hd] + jnp.dot(kd, v, preferred_element_type=F32)
        mu = jnp.mean(o, axis=-1, keepdims=True)
        var = jnp.mean(jnp.square(o - mu), axis=-1, keepdims=True)
        rb = (o - mu) * lax.rsqrt(var + EPS) * gn_ref[:, vs]
        gb = gb_ref[0, :, vs]
        out_ref[0, :, vs] = (gb * jax.nn.sigmoid(gb) * rb).astype(BF16)


def _retention(qb, kbt, vb, gb, state0, log_g, ret_norm_g, *, t, kt_index):
    b, s, _ = qb.shape
    nc = s // t
    idx = jnp.arange(t, dtype=F32)
    diff = idx[:, None] - idx[None, :]
    dmask = jnp.where(diff[None] >= 0, jnp.exp(log_g[:, None, None] * jnp.maximum(diff, 0.0)[None]), 0.0)
    q_decay = jnp.exp(log_g[:, None] * (idx[None, :] + 1.0))
    k_decay = jnp.exp(log_g[:, None] * (t - 1.0 - idx[None, :]))
    g_t = jnp.exp(log_g * t)
    qd = jnp.broadcast_to(q_decay[:, :, None], (N_HEADS_B, t, DV_B))
    kd = k_decay[:, None, :]
    const3 = lambda bi, c: (0, 0, 0)
    return pl.pallas_call(
        _ret_kernel,
        grid=(b, nc),
        in_specs=[
            pl.BlockSpec(memory_space=pltpu.SMEM),
            pl.BlockSpec((1, t, QK_B), lambda bi, c: (bi, c, 0)),
            pl.BlockSpec((1, QK_B, t), kt_index),
            pl.BlockSpec((1, t, V_B), lambda bi, c: (bi, c, 0)),
            pl.BlockSpec((1, t, V_B), lambda bi, c: (bi, c, 0)),
            pl.BlockSpec((1, N_HEADS_B, DK_B, DV_B), lambda bi, c: (bi, 0, 0, 0)),
            pl.BlockSpec((N_HEADS_B, t, t), const3),
            pl.BlockSpec((N_HEADS_B, t, DV_B), const3),
            pl.BlockSpec((N_HEADS_B, 1, t), const3),
            pl.BlockSpec((1, V_B), lambda bi, c: (0, 0)),
        ],
        out_specs=(
            pl.BlockSpec((1, t, V_B), lambda bi, c: (bi, c, 0)),
            pl.BlockSpec((1, N_HEADS_B, DK_B, DV_B), lambda bi, c: (bi, 0, 0, 0)),
        ),
        out_shape=(jax.ShapeDtypeStruct((b, s, V_B), BF16),
                   jax.ShapeDtypeStruct((b, N_HEADS_B, DK_B, DV_B), F32)),
        compiler_params=_params(("arbitrary", "arbitrary")),
        name="retention",
    )(g_t, qb, kbt, vb, gb, state0, dmask, qd, kd, ret_norm_g.reshape(1, V_B))


def _split_bf16(x):
    hi = x.astype(BF16)
    lo = (x - hi.astype(F32)).astype(BF16)
    return hi, lo


def _merge_kernel(x_ref, att_ref, bin_ref, gmix_ref, wga_ref, wgb_ref, wpa_ref, wpb_ref, wout_ref,
                  gffn_ref, wr_hi_ref, wr_lo_ref, br_ref, x1_ref, h2_ref, ids_ref, wts_ref):
    x = x_ref[...]
    h = _rms(x, gmix_ref[...]).astype(BF16)
    gate_a = jnp.dot(h, wga_ref[...], preferred_element_type=F32)
    gate_b = jnp.dot(h, wgb_ref[...], preferred_element_type=F32)
    a = jnp.dot(att_ref[...], wpa_ref[...], preferred_element_type=F32)
    b = jnp.dot(bin_ref[...], wpb_ref[...], preferred_element_type=F32)
    m = jax.nn.sigmoid(gate_a) * a + jax.nn.sigmoid(gate_b) * b
    x1 = x + jnp.dot(m.astype(BF16), wout_ref[...], preferred_element_type=F32)
    x1_ref[...] = x1
    h2 = _rms(x1, gffn_ref[...])
    h2_ref[...] = h2

    h_hi, h_lo = _split_bf16(h2)
    nt = (((1,), (1,)), ((), ()))
    lt = (lax.dot_general(wr_hi_ref[...], h_hi, nt, preferred_element_type=F32)
          + lax.dot_general(wr_hi_ref[...], h_lo, nt, preferred_element_type=F32)
          + lax.dot_general(wr_lo_ref[...], h_hi, nt, preferred_element_type=F32)) + br_ref[...]
    tm = lt.shape[1]
    row = lax.broadcasted_iota(jnp.int32, (8, tm), 0)
    lg = lt[0:8, :]
    mg = jnp.max(lg, axis=0, keepdims=True)
    grp = jnp.min(jnp.where(lg == mg, row, 8), axis=0, keepdims=True)
    p_grp = 1.0 / jnp.sum(jnp.exp(lg - mg), axis=0, keepdims=True)
    le = jnp.zeros((8, tm), F32)
    for g in range(N_GROUPS):
        le = jnp.where(grp == g, lt[8 + 8 * g:16 + 8 * g, :], le)
    m0 = jnp.max(le, axis=0, keepdims=True)
    i0 = jnp.min(jnp.where(le == m0, row, 8), axis=0, keepdims=True)
    rest = jnp.where(row == i0, jnp.float32(-jnp.inf), le)
    m1 = jnp.max(rest, axis=0, keepdims=True)
    i1 = jnp.min(jnp.where(rest == m1, row, 8), axis=0, keepdims=True)
    e = jnp.exp(m1 - m0)
    w0 = (1.0 / (1.0 + e)) * p_grp
    w1 = (e / (1.0 + e)) * p_grp
    e0 = grp * EXPERTS_PER_GROUP + i0
    e1 = grp * EXPERTS_PER_GROUP + i1
    ids_ref[...] = jnp.where(row == 0, e0, jnp.where(row == 1, e1, 0))
    wts_ref[...] = jnp.where(row == 0, w0, jnp.where(row == 1, w1, 0.0))


def _merge(x2d, att, b_in, lw, *, tm):
    n = x2d.shape[0]
    row = lambda i: (i, 0)
    const = lambda i: (0, 0)
    full = lambda a: pl.BlockSpec(a.shape, const)
    return pl.pallas_call(
        _merge_kernel,
        grid=(n // tm,),
        in_specs=[
            pl.BlockSpec((tm, D_MODEL), row), pl.BlockSpec((tm, W_A), row), pl.BlockSpec((tm, V_B), row),
            full(lw["g_mix"]), full(lw["w_ga"]), full(lw["w_gb"]), full(lw["w_pa"]), full(lw["w_pb"]),
            full(lw["w_out"]), full(lw["g_ffn"]), full(lw["wr_hi"]), full(lw["wr_lo"]), full(lw["b_r"]),
        ],
        out_specs=(pl.BlockSpec((tm, D_MODEL), row), pl.BlockSpec((tm, D_MODEL), row),
                   pl.BlockSpec((8, tm), lambda i: (0, i)), pl.BlockSpec((8, tm), lambda i: (0, i))),
        out_shape=(jax.ShapeDtypeStruct((n, D_MODEL), F32), jax.ShapeDtypeStruct((n, D_MODEL), F32),
                   jax.ShapeDtypeStruct((8, n), jnp.int32), jax.ShapeDtypeStruct((8, n), F32)),
        compiler_params=_params(("arbitrary",)),
        name="merge",
    )(x2d, att, b_in, lw["g_mix"], lw["w_ga"], lw["w_gb"], lw["w_pa"], lw["w_pb"], lw["w_out"],
      lw["g_ffn"], lw["wr_hi"], lw["wr_lo"], lw["b_r"])


def _moe_kernel(te_ref, nv_ref, src_ref, srcn_ref, dst_ref, wt_ref, h_hbm, wg_ref, wu_ref, wd_ref,
                y_hbm, xbuf, ybuf, gsem, ssem, *, tm):
    i = pl.program_id(0)
    nv = nv_ref[0]
    slot = i % 2

    def gather_row(idx_ref, r, s):
        return pltpu.make_async_copy(h_hbm.at[pl.ds(idx_ref[0, 0, r], 1)], xbuf.at[s, pl.ds(r, 1)], gsem.at[s])

    def scatter_row(r, s):
        return pltpu.make_async_copy(ybuf.at[s, pl.ds(r, 1)], y_hbm.at[pl.ds(dst_ref[0, 0, r], 1)], ssem.at[s])

    def start_gather(idx_ref, s):
        def body(r, c):
            gather_row(idx_ref, r, s).start()
            return c
        lax.fori_loop(0, tm, body, 0)

    def wait_rows(make, s):
        def body(r, c):
            make(r, s).wait()
            return c
        lax.fori_loop(0, tm, body, 0)

    @pl.when((i == 0) & (nv > 0))
    def _():
        start_gather(src_ref, 0)

    @pl.when(i + 1 < nv)
    def _():
        start_gather(srcn_ref, 1 - slot)

    @pl.when(i < nv)
    def _():
        wait_rows(functools.partial(gather_row, src_ref), slot)

        @pl.when(i >= 2)
        def _():
            wait_rows(scatter_row, slot)

        x = xbuf[slot].astype(BF16)
        g = jnp.dot(x, wg_ref[0], preferred_element_type=F32)
        u = jnp.dot(x, wu_ref[0], preferred_element_type=F32)
        act = (g * jax.nn.sigmoid(g) * u).astype(BF16)
        y = jnp.dot(act, wd_ref[0], preferred_element_type=F32)
        ybuf[slot] = y * wt_ref[...]

        def body(r, c):
            scatter_row(r, slot).start()
            return c
        lax.fori_loop(0, tm, body, 0)

        @pl.when(i == nv - 1)
        def _():
            @pl.when(i >= 1)
            def _():
                wait_rows(scatter_row, 1 - slot)
            wait_rows(scatter_row, slot)

    @pl.when(i >= nv)
    def _():
        ybuf[slot] = jnp.zeros((tm, D_MODEL), F32)
        first = pl.multiple_of(dst_ref[0, 0, 0], tm)
        fill = pltpu.make_async_copy(ybuf.at[slot], y_hbm.at[pl.ds(first, tm)], ssem.at[slot])
        fill.start()
        fill.wait()


def _moe_plan(ids, wts, tm):
    n = ids.shape[1]
    na = TOP_K * n
    assert na % tm == 0
    e_flat = ids[:TOP_K].T.reshape(-1)
    w_flat = wts[:TOP_K].T.reshape(-1)
    order = jnp.argsort(e_flat).astype(jnp.int32)
    sizes = jnp.bincount(e_flat, length=N_EXPERTS).astype(jnp.int32)
    starts = jnp.cumsum(sizes) - sizes
    psizes = ((sizes + tm - 1) // tm) * tm
    pends = jnp.cumsum(psizes)
    pstarts = pends - psizes
    n_tiles = na // tm + N_EXPERTS
    n_slots = n_tiles * tm
    nv = (pends[-1] // tm).astype(jnp.int32)
    tile_e = jnp.minimum(jnp.searchsorted(pends, jnp.arange(n_tiles, dtype=jnp.int32) * tm, side="right"),
                         N_EXPERTS - 1).astype(jnp.int32)
    tile_e = jnp.where(jnp.arange(n_tiles) < nv, tile_e, tile_e[jnp.maximum(nv - 1, 0)])
    slot_e = jnp.repeat(tile_e, tm)
    pos = jnp.arange(n_slots, dtype=jnp.int32) - pstarts[slot_e]
    valid = (pos < sizes[slot_e]) & (jnp.arange(n_slots) < nv * tm)
    a = order[jnp.clip(starts[slot_e] + pos, 0, na - 1)]
    pad_rank = jnp.cumsum(jnp.logical_not(valid).astype(jnp.int32)) - 1
    src = jnp.where(valid, a // TOP_K, 0).astype(jnp.int32)
    dst = jnp.where(valid, a, na + pad_rank).astype(jnp.int32)
    wt = jnp.where(valid, w_flat[a], 0.0)
    return (tile_e, nv.reshape(1), src.reshape(n_tiles, 1, tm), dst.reshape(n_tiles, 1, tm),
            wt.reshape(n_slots, 1), n_tiles)


def _moe(h2, ids, wts, w_gate, w_up, w_down, *, tm):
    tile_e, nv, src, dst, wt, n_tiles = _moe_plan(ids, wts, tm)
    idx_spec = lambda f: pl.BlockSpec((1, 1, tm), f, memory_space=pltpu.SMEM)
    wspec = lambda shp: pl.BlockSpec((1,) + shp, lambda i, te, nv_: (te[i], 0, 0))
    grid_spec = pltpu.PrefetchScalarGridSpec(
        num_scalar_prefetch=2,
        grid=(n_tiles,),
        in_specs=[
            idx_spec(lambda i, te, nv_: (i, 0, 0)),
            idx_spec(lambda i, te, nv_: (jnp.minimum(i + 1, n_tiles - 1), 0, 0)),
            idx_spec(lambda i, te, nv_: (i, 0, 0)),
            pl.BlockSpec((tm, 1), lambda i, te, nv_: (i, 0)),
            pl.BlockSpec(memory_space=pl.ANY),
            wspec((D_MODEL, D_EXPERT)), wspec((D_MODEL, D_EXPERT)), wspec((D_EXPERT, D_MODEL)),
        ],
        out_specs=pl.BlockSpec(memory_space=pl.ANY),
        scratch_shapes=[
            pltpu.VMEM((2, tm, D_MODEL), F32), pltpu.VMEM((2, tm, D_MODEL), F32),
            pltpu.SemaphoreType.DMA((2,)), pltpu.SemaphoreType.DMA((2,)),
        ],
    )
    return pl.pallas_call(
        functools.partial(_moe_kernel, tm=tm),
        grid_spec=grid_spec,
        out_shape=jax.ShapeDtypeStruct((n_tiles * tm, D_MODEL), F32),
        compiler_params=_params(("arbitrary",)),
        name="moe",
    )(tile_e, nv, src, src, dst, wt, h2, w_gate, w_up, w_down)


def _ple_kernel(x1_ref, y2_ref, p_ref, wproj_ref, wgate_ref, gple_ref, gfin_ref, y_ref):
    y2 = y2_ref[...]
    x2 = x1_ref[...] + (y2[:, :D_MODEL] + y2[:, D_MODEL:])
    proj = jnp.dot(p_ref[...].astype(BF16), wproj_ref[...], preferred_element_type=F32)
    gate = jnp.dot(_rms(x2, gple_ref[...]).astype(BF16), wgate_ref[...], preferred_element_type=F32)
    x3 = x2 + proj * jax.nn.sigmoid(gate)
    y_ref[...] = _rms(x3, gfin_ref[...])


def _ple(x1, y2, p2d, lw, g_final, *, tm):
    n = x1.shape[0]
    y2w = y2.reshape(-1, TOP_K * D_MODEL)
    row = lambda i: (i, 0)
    const = lambda i: (0, 0)
    return pl.pallas_call(
        _ple_kernel,
        grid=(n // tm,),
        in_specs=[
            pl.BlockSpec((tm, D_MODEL), row), pl.BlockSpec((tm, TOP_K * D_MODEL), row),
            pl.BlockSpec((tm, D_PLE), row),
            pl.BlockSpec((D_PLE, D_MODEL), const), pl.BlockSpec((D_MODEL, D_MODEL), const),
            pl.BlockSpec((1, D_MODEL), const), pl.BlockSpec((1, D_MODEL), const),
        ],
        out_specs=pl.BlockSpec((tm, D_MODEL), row),
        out_shape=jax.ShapeDtypeStruct((n, D_MODEL), F32),
        compiler_params=_params(("arbitrary",)),
        name="ple",
    )(x1, y2w, p2d, lw["w_ple_proj"], lw["w_ple_gate"], lw["g_ple"], g_final.reshape(1, D_MODEL))


def _layer_weights(i, norm_mix_g, w_in, ret_norm_g, w_proj_a, w_proj_b, w_out, norm_ffn_g,
                   w_router_group, b_router_group, w_router_expert, b_router_expert,
                   w_gate_e, w_up_e, w_down_e, norm_ple_g, w_ple_gate, w_ple_proj):
    w = w_in[i]
    o = _IN_OFFS
    cols = lambda k: w[:, o[k]:o[k + 1]]
    w_main = jnp.concatenate([cols(0), cols(1), cols(2), cols(3), cols(5), cols(6)], axis=1).astype(BF16)
    wr = jnp.zeros((ROUTER_ROWS, D_MODEL), F32)
    wr = wr.at[:N_GROUPS].set(w_router_group[i].T).at[8:].set(w_router_expert[i].T)
    br = jnp.full((ROUTER_ROWS,), NEG_BIG, F32)
    br = br.at[:N_GROUPS].set(b_router_group[i].astype(F32)).at[8:].set(b_router_expert[i].astype(F32))
    wr_hi, wr_lo = _split_bf16(wr)
    return {
        "g_mix": norm_mix_g[i].reshape(1, D_MODEL), "w_main": w_main, "w_kt": cols(4).T.astype(BF16),
        "w_ga": cols(7).astype(BF16), "w_gb": cols(8).astype(BF16),
        "ret_norm_g": ret_norm_g[i], "w_pa": w_proj_a[i].astype(BF16), "w_pb": w_proj_b[i].astype(BF16),
        "w_out": w_out[i].astype(BF16), "g_ffn": norm_ffn_g[i].reshape(1, D_MODEL),
        "wr_hi": wr_hi, "wr_lo": wr_lo, "b_r": br.reshape(ROUTER_ROWS, 1),
        "w_gate": w_gate_e[i].astype(BF16), "w_up": w_up_e[i].astype(BF16), "w_down": w_down_e[i].astype(BF16),
        "g_ple": norm_ple_g[i].reshape(1, D_MODEL), "w_ple_gate": w_ple_gate[i].astype(BF16),
        "w_ple_proj": w_ple_proj[i].astype(BF16),
    }


def _row_tile(n):
    return 512 if n % 512 == 0 else n


def _channel(x2d, att, b_in, p2d, lw, g_final, *, tm, moe_tm):
    x1, h2, ids, wts = _merge(x2d, att, b_in, lw, tm=tm)
    y2 = _moe(h2, ids, wts, lw["w_gate"], lw["w_up"], lw["w_down"], tm=moe_tm)
    return _ple(x1, y2, p2d, lw, g_final, tm=tm)


def kernel(x_prompt, x_sample, cache_k_a, cache_v_a, state_ret, p_prompt, p_sample, norm_mix_g, w_in, rel_bias, ret_norm_g, w_proj_a, w_proj_b, w_out, norm_ffn_g, w_router_group, b_router_group, w_router_expert, b_router_expert, w_gate_e, w_up_e, w_down_e, norm_ple_g, w_ple_gate, w_ple_proj, final_norm_g):
    depth = w_in.shape[0]
    assert depth == 1, "the final norm is fused into the last layer; deeper stacks are not supported"
    bp, sp, _ = x_prompt.shape
    bs, ss, _ = x_sample.shape
    keep = min(WINDOW_A, sp)
    n_cache = cache_k_a.shape[2]
    log_g = jnp.log(1.0 - 2.0 ** (-5.0 - jnp.arange(N_HEADS_B, dtype=F32)))
    i = 0
    lw = _layer_weights(i, norm_mix_g, w_in, ret_norm_g, w_proj_a, w_proj_b, w_out, norm_ffn_g,
                        w_router_group, b_router_group, w_router_expert, b_router_expert,
                        w_gate_e, w_up_e, w_down_e, norm_ple_g, w_ple_gate, w_ple_proj)

    tm = 512
    assert sp % tm == 0 and keep == tm and sp >= ATT_WIN
    t_ret = 128
    xp2 = x_prompt.reshape(bp * sp, D_MODEL)
    qa, ka, va, qb, kbt, vb, gb, ka32, va32 = _project(
        xp2, jnp.arange(sp), lw["g_mix"], lw["w_main"], lw["w_kt"], tm=tm, tiles_per_keep=sp // tm)
    r3 = lambda a: a.reshape(bp, sp, a.shape[-1])
    att = _attention_prompt(r3(qa), r3(ka), r3(va), rel_bias[i])
    per_tile = tm // t_ret
    tiles_per_b = sp // tm
    b_in, s_prompt = _retention(
        r3(qb), kbt, r3(vb), r3(gb), jnp.zeros((bp, N_HEADS_B, DK_B, DV_B), F32), log_g, lw["ret_norm_g"],
        t=t_ret, kt_index=lambda bi, c: (bi * tiles_per_b + c // per_tile, 0, c % per_tile))
    y_prompt = _channel(xp2, att.reshape(bp * sp, W_A), b_in.reshape(bp * sp, V_B),
                        p_prompt[i].reshape(bp * sp, D_PLE), lw, final_norm_g, tm=tm, moe_tm=256)
    k_a_prompt = ka32.reshape(bp, keep, N_HEADS_A, HEAD_DIM_A)
    v_a_prompt = va32.reshape(bp, keep, N_HEADS_A, HEAD_DIM_A)

    ns = bs * ss
    xs2 = x_sample.reshape(ns, D_MODEL)
    pos_s = jnp.tile(PAST_LEN + jnp.arange(ss), bs)
    qa, ka, va, qb, kbt, vb, gb, ka32, va32 = _project(
        xs2, pos_s, lw["g_mix"], lw["w_main"], lw["w_kt"], tm=ns, tiles_per_keep=1)
    r3 = lambda a: a.reshape(bs, ss, a.shape[-1])
    k_all = jnp.concatenate([cache_k_a[i].reshape(bs, n_cache, W_A).astype(BF16), r3(ka)], axis=1)
    v_all = jnp.concatenate([cache_v_a[i].reshape(bs, n_cache, W_A).astype(BF16), r3(va)], axis=1)
    att = _attention_sample(r3(qa), k_all, v_all, rel_bias[i], n_cache)
    kbt_s = kbt.reshape(QK_B, bs, ss).transpose(1, 0, 2)
    b_in, s_sample = _retention(
        r3(qb), kbt_s, r3(vb), r3(gb), state_ret[i].astype(F32), log_g, lw["ret_norm_g"],
        t=ss, kt_index=lambda bi, c: (bi, 0, 0))
    y_sample = _channel(xs2, att.reshape(ns, W_A), b_in.reshape(ns, V_B),
                        p_sample[i].reshape(ns, D_PLE), lw, final_norm_g, tm=ns, moe_tm=16)
    k_a_sample = ka32.reshape(bs, ss, N_HEADS_A, HEAD_DIM_A)
    v_a_sample = va32.reshape(bs, ss, N_HEADS_A, HEAD_DIM_A)

    return (y_prompt.reshape(bp, sp, D_MODEL), y_sample.reshape(bs, ss, D_MODEL),
            k_a_prompt[None], v_a_prompt[None], s_prompt[None],
            k_a_sample[None], v_a_sample[None], s_sample.astype(state_ret.dtype)[None])
```

```python
import functools

import numpy as np
import jax
import jax.numpy as jnp
from jax import lax
from jax.experimental import pallas as pl
from jax.experimental.pallas import tpu as pltpu

F32 = jnp.float32
BF16 = jnp.bfloat16

D_MODEL = 1024
PAST_LEN = 1024
CHUNK = 64
BAND_CHUNKS = 8
WINDOW_A = BAND_CHUNKS * CHUNK
N_HEADS_A = 8
HEAD_DIM_A = 64
W_A = N_HEADS_A * HEAD_DIM_A
REL_CLIP = 128
N_HEADS_B = 4
DK_B = 128
DV_B = 256
QK_B = N_HEADS_B * DK_B
V_B = N_HEADS_B * DV_B
ROPE_BASE = 10000.0
N_GROUPS = 4
EXPERTS_PER_GROUP = 8
N_EXPERTS = N_GROUPS * EXPERTS_PER_GROUP
TOP_K = 2
D_EXPERT = 512
D_PLE = 256
EPS = 1e-6
_IN_SIZES = (W_A, W_A, W_A, QK_B, QK_B, V_B, V_B, D_MODEL, D_MODEL)
_IN_OFFS = tuple(sum(_IN_SIZES[:i]) for i in range(len(_IN_SIZES) + 1))

LANES = 128
ATT_QBLK = 2 * CHUNK
ATT_WIN = (BAND_CHUNKS + 2) * CHUNK
ROUTER_ROWS = 8 + N_EXPERTS
NEG_BIG = -1e30
PAIRS_PER_GROUP = EXPERTS_PER_GROUP * (EXPERTS_PER_GROUP - 1) // 2
N_CLASSES = N_GROUPS * PAIRS_PER_GROUP
CLASS_ROWS = 128
HX_W = D_MODEL + LANES
PLAN_T = 512
MOE_TM = 256
VMEM_LIMIT = 56 * 1024 * 1024


def _params(sem):
    return pltpu.CompilerParams(dimension_semantics=sem, vmem_limit_bytes=VMEM_LIMIT)


def _rms(x, g):
    return x * lax.rsqrt(jnp.mean(x * x, axis=-1, keepdims=True) + EPS) * g


def _proj_kernel(x_ref, g_ref, w_ref, wkt_ref, cos_ref, sin_ref, cost_ref, sint_ref,
                 qa_ref, ka_ref, va_ref, qb_ref, kbt_ref, vb_ref, gb_ref, ka32_ref, va32_ref,
                 *, tiles_per_keep):
    h = _rms(x_ref[...], g_ref[...]).astype(BF16)

    def seg(lo, hi):
        return jnp.dot(h, w_ref[:, lo:hi], preferred_element_type=F32)

    qa_ref[...] = seg(0, W_A).astype(BF16)
    ka = seg(W_A, 2 * W_A)
    va = seg(2 * W_A, 3 * W_A)
    ka_ref[...] = ka.astype(BF16)
    va_ref[...] = va.astype(BF16)

    @pl.when(pl.program_id(0) % tiles_per_keep == tiles_per_keep - 1)
    def _():
        ka32_ref[...] = ka
        va32_ref[...] = va

    qb = seg(3 * W_A, 3 * W_A + QK_B)
    cos = cos_ref[...]
    sin = sin_ref[...]
    for hd in range(N_HEADS_B):
        xh = qb[:, hd * DK_B:(hd + 1) * DK_B]
        qb_ref[:, hd * DK_B:(hd + 1) * DK_B] = (xh * cos + pltpu.roll(xh, DK_B // 2, axis=1) * sin).astype(BF16)

    vb_ref[...] = seg(3 * W_A + QK_B, 3 * W_A + QK_B + V_B).astype(BF16)
    gb_ref[...] = seg(3 * W_A + QK_B + V_B, 3 * W_A + QK_B + 2 * V_B)

    kt = lax.dot_general(wkt_ref[...], h, (((1,), (1,)), ((), ())), preferred_element_type=F32)
    cost = cost_ref[...]
    sint = sint_ref[...]
    half = DK_B // 2
    scale = DK_B ** -0.5
    for hd in range(N_HEADS_B):
        x1 = kt[hd * DK_B:hd * DK_B + half, :]
        x2 = kt[hd * DK_B + half:(hd + 1) * DK_B, :]
        kbt_ref[0, hd * DK_B:hd * DK_B + half, :] = (x1 * cost - x2 * sint) * scale
        kbt_ref[0, hd * DK_B + half:(hd + 1) * DK_B, :] = (x2 * cost + x1 * sint) * scale


def _rope_tables(pos):
    half = DK_B // 2
    freqs = ROPE_BASE ** (-jnp.arange(half, dtype=F32) / half)
    ang = pos.astype(F32)[:, None] * freqs[None, :]
    cos = jnp.cos(ang)
    sin = jnp.sin(ang)
    return (jnp.concatenate([cos, cos], axis=1), jnp.concatenate([-sin, sin], axis=1), cos.T, sin.T)


def _project(x2d, pos_rows, g_norm, w_main, w_kt, *, tm, tiles_per_keep):
    n = x2d.shape[0]
    period = pos_rows.shape[0]
    nt = n // tm
    ppt = period // tm
    cos2, sin2, cost, sint = _rope_tables(pos_rows)
    n_keep = n // tiles_per_keep
    row = lambda i: (i, 0)
    const = lambda i: (0, 0)
    outs = (
        jax.ShapeDtypeStruct((n, W_A), BF16), jax.ShapeDtypeStruct((n, W_A), BF16),
        jax.ShapeDtypeStruct((n, W_A), BF16), jax.ShapeDtypeStruct((n, QK_B), BF16),
        jax.ShapeDtypeStruct((nt, QK_B, tm), F32), jax.ShapeDtypeStruct((n, V_B), BF16),
        jax.ShapeDtypeStruct((n, V_B), F32),
        jax.ShapeDtypeStruct((n_keep, W_A), F32), jax.ShapeDtypeStruct((n_keep, W_A), F32),
    )
    keep_spec = pl.BlockSpec((tm, W_A), lambda i: (i // tiles_per_keep, 0))
    return pl.pallas_call(
        functools.partial(_proj_kernel, tiles_per_keep=tiles_per_keep),
        grid=(nt,),
        in_specs=[
            pl.BlockSpec((tm, D_MODEL), row),
            pl.BlockSpec((1, D_MODEL), const),
            pl.BlockSpec(w_main.shape, const),
            pl.BlockSpec(w_kt.shape, const),
            pl.BlockSpec((tm, DK_B), lambda i: (i % ppt, 0)),
            pl.BlockSpec((tm, DK_B), lambda i: (i % ppt, 0)),
            pl.BlockSpec((DK_B // 2, tm), lambda i: (0, i % ppt)),
            pl.BlockSpec((DK_B // 2, tm), lambda i: (0, i % ppt)),
        ],
        out_specs=(
            pl.BlockSpec((tm, W_A), row), pl.BlockSpec((tm, W_A), row), pl.BlockSpec((tm, W_A), row),
            pl.BlockSpec((tm, QK_B), row), pl.BlockSpec((1, QK_B, tm), lambda i: (i, 0, 0)),
            pl.BlockSpec((tm, V_B), row), pl.BlockSpec((tm, V_B), row), keep_spec, keep_spec,
        ),
        out_shape=outs,
        compiler_params=_params(("arbitrary",)),
        name="proj",
    )(x2d, g_norm.reshape(1, D_MODEL), w_main, w_kt, cos2, sin2, cost, sint)


def _attend_pairs(q_of, k_of, v_of, bias_of, store):
    for hp in range(N_HEADS_A // 2):
        qp = q_of(hp)
        kw = k_of(hp)
        vw = v_of(hp)
        lane = lax.broadcasted_iota(jnp.int32, qp.shape, 1)
        outs = []
        for hh in range(2):
            in_head = (lane >= hh * HEAD_DIM_A) & (lane < (hh + 1) * HEAD_DIM_A)
            qh = jnp.where(in_head, qp, jnp.zeros_like(qp))
            s = lax.dot_general(qh, kw, (((1,), (1,)), ((), ())), preferred_element_type=F32)
            s = s * (HEAD_DIM_A ** -0.5) + bias_of(2 * hp + hh)
            m = jnp.max(s, axis=-1, keepdims=True)
            p = jnp.exp(s - m)
            l = jnp.sum(p, axis=-1, keepdims=True)
            o = jnp.dot(p.astype(BF16), vw, preferred_element_type=F32)
            outs.append(o / l)
        lane_o = lax.broadcasted_iota(jnp.int32, outs[0].shape, 1)
        store(hp, jnp.where(lane_o < HEAD_DIM_A, outs[0], outs[1]).astype(BF16))


def _attn_prompt_kernel(q_ref, k_ref, v_ref, bias_ref, o_ref):
    j = pl.program_id(1)
    start = pl.multiple_of(jnp.maximum(2 * j - BAND_CHUNKS, 0) * CHUNK, CHUNK)

    def sl(hp):
        return slice(hp * LANES, (hp + 1) * LANES)

    def store(hp, val):
        o_ref[0, :, sl(hp)] = val

    _attend_pairs(
        lambda hp: q_ref[0, :, sl(hp)],
        lambda hp: k_ref[0, pl.ds(start, ATT_WIN), sl(hp)],
        lambda hp: v_ref[0, pl.ds(start, ATT_WIN), sl(hp)],
        lambda hd: bias_ref[0, hd],
        store)


def _band_bias(table):
    i = np.arange(ATT_QBLK)[:, None]
    jk = np.arange(ATT_WIN)[None, :]
    out = []
    for v in range(BAND_CHUNKS // 2 + 1):
        off_chunks = 2 * v if v < BAND_CHUNKS // 2 else BAND_CHUNKS
        dchunk = (off_chunks + i // CHUNK) - jk // CHUNK
        valid = (dchunk >= 0) & (dchunk <= BAND_CHUNKS)
        m = np.arange(ATT_WIN + ATT_QBLK - 1)
        idx = np.clip(off_chunks * CHUNK + (ATT_QBLK - 1) - m, -REL_CLIP, REL_CLIP) + REL_CLIP
        f = table[:, idx].astype(F32)
        b = jnp.stack([f[:, ATT_QBLK - 1 - r:ATT_QBLK - 1 - r + ATT_WIN] for r in range(ATT_QBLK)], axis=1)
        out.append(jnp.where(valid[None], b, jnp.float32(NEG_BIG)))
    return jnp.stack(out)


def _attention_prompt(qa, ka, va, table):
    b, s, _ = qa.shape
    nq = s // ATT_QBLK
    bias = _band_bias(table)
    nvar = bias.shape[0]
    return pl.pallas_call(
        _attn_prompt_kernel,
        grid=(b, nq),
        in_specs=[
            pl.BlockSpec((1, ATT_QBLK, W_A), lambda bi, j: (bi, j, 0)),
            pl.BlockSpec((1, s, W_A), lambda bi, j: (bi, 0, 0)),
            pl.BlockSpec((1, s, W_A), lambda bi, j: (bi, 0, 0)),
            pl.BlockSpec((1, N_HEADS_A, ATT_QBLK, ATT_WIN), lambda bi, j: (jnp.minimum(j, nvar - 1), 0, 0, 0)),
        ],
        out_specs=pl.BlockSpec((1, ATT_QBLK, W_A), lambda bi, j: (bi, j, 0)),
        out_shape=jax.ShapeDtypeStruct((b, s, W_A), BF16),
        compiler_params=_params(("arbitrary", "arbitrary")),
        name="attn_prompt",
    )(qa, ka, va, bias)


def _attn_sample_kernel(q_ref, k_ref, v_ref, bias_ref, o_ref):
    def sl(hp):
        return slice(hp * LANES, (hp + 1) * LANES)

    def store(hp, val):
        o_ref[0, :, sl(hp)] = val

    _attend_pairs(
        lambda hp: q_ref[0, :, sl(hp)],
        lambda hp: k_ref[0, :, sl(hp)],
        lambda hp: v_ref[0, :, sl(hp)],
        lambda hd: bias_ref[hd],
        store)


def _attention_sample(qa, k_all, v_all, table, n_cache):
    b, n, _ = qa.shape
    nk = k_all.shape[1]
    dist = jnp.arange(n)[:, None] + n_cache - jnp.arange(nk)[None, :]
    bias = table[:, jnp.clip(dist, -REL_CLIP, REL_CLIP) + REL_CLIP].astype(F32)
    return pl.pallas_call(
        _attn_sample_kernel,
        grid=(b,),
        in_specs=[
            pl.BlockSpec((1, n, W_A), lambda bi: (bi, 0, 0)),
            pl.BlockSpec((1, nk, W_A), lambda bi: (bi, 0, 0)),
            pl.BlockSpec((1, nk, W_A), lambda bi: (bi, 0, 0)),
            pl.BlockSpec((N_HEADS_A, n, nk), lambda bi: (0, 0, 0)),
        ],
        out_specs=pl.BlockSpec((1, n, W_A), lambda bi: (bi, 0, 0)),
        out_shape=jax.ShapeDtypeStruct((b, n, W_A), BF16),
        compiler_params=_params(("arbitrary",)),
        name="attn_sample",
    )(qa, k_all, v_all, bias)


def _ret_kernel(gt_ref, q_ref, kt_ref, v_ref, gb_ref, s0_ref, dmask_ref, qd_ref, kd_ref, gn_ref,
                out_ref, state_ref):
    @pl.when(pl.program_id(1) == 0)
    def _():
        state_ref[...] = s0_ref[...]

    for hd in range(N_HEADS_B):
        qs = slice(hd * DK_B, (hd + 1) * DK_B)
        vs = slice(hd * DV_B, (hd + 1) * DV_B)
        q = q_ref[0, :, qs]
        kt = kt_ref[0, qs, :]
        v = v_ref[0, :, vs]
        state = state_ref[0, hd]
        scores = jnp.dot(q, kt.astype(BF16), preferred_element_type=F32) * dmask_ref[hd]
        o = jnp.dot(scores.astype(BF16), v, preferred_element_type=F32)
        o = o + jnp.dot(q, state.astype(BF16), preferred_element_type=F32) * qd_ref[hd]
        kd = (kt * kd_ref[hd]).astype(BF16)
        state_ref[0, hd] = state * gt_ref[hd] + jnp.dot(kd, v, preferred_element_type=F32)
        mu = jnp.mean(o, axis=-1, keepdims=True)
        var = jnp.mean(jnp.square(o - mu), axis=-1, keepdims=True)
        rb = (o - mu) * lax.rsqrt(var + EPS) * gn_ref[:, vs]
        gb = gb_ref[0, :, vs]
        out_ref[0, :, vs] = (gb * jax.nn.sigmoid(gb) * rb).astype(BF16)


def _retention(qb, kbt, vb, gb, state0, log_g, ret_norm_g, *, t, kt_index):
    b, s, _ = qb.shape
    nc = s // t
    idx = jnp.arange(t, dtype=F32)
    diff = idx[:, None] - idx[None, :]
    dmask = jnp.where(diff[None] >= 0, jnp.exp(log_g[:, None, None] * jnp.maximum(diff, 0.0)[None]), 0.0)
    q_decay = jnp.exp(log_g[:, None] * (idx[None, :] + 1.0))
    k_decay = jnp.exp(log_g[:, None] * (t - 1.0 - idx[None, :]))
    g_t = jnp.exp(log_g * t)
    qd = jnp.broadcast_to(q_decay[:, :, None], (N_HEADS_B, t, DV_B))
    kd = k_decay[:, None, :]
    const3 = lambda bi, c: (0, 0, 0)
    return pl.pallas_call(
        _ret_kernel,
        grid=(b, nc),
        in_specs=[
            pl.BlockSpec(memory_space=pltpu.SMEM),
            pl.BlockSpec((1, t, QK_B), lambda bi, c: (bi, c, 0)),
            pl.BlockSpec((1, QK_B, t), kt_index),
            pl.BlockSpec((1, t, V_B), lambda bi, c: (bi, c, 0)),
            pl.BlockSpec((1, t, V_B), lambda bi, c: (bi, c, 0)),
            pl.BlockSpec((1, N_HEADS_B, DK_B, DV_B), lambda bi, c: (bi, 0, 0, 0)),
            pl.BlockSpec((N_HEADS_B, t, t), const3),
            pl.BlockSpec((N_HEADS_B, t, DV_B), const3),
            pl.BlockSpec((N_HEADS_B, 1, t), const3),
            pl.BlockSpec((1, V_B), lambda bi, c: (0, 0)),
        ],
        out_specs=(
            pl.BlockSpec((1, t, V_B), lambda bi, c: (bi, c, 0)),
            pl.BlockSpec((1, N_HEADS_B, DK_B, DV_B), lambda bi, c: (bi, 0, 0, 0)),
        ),
        out_shape=(jax.ShapeDtypeStruct((b, s, V_B), BF16),
                   jax.ShapeDtypeStruct((b, N_HEADS_B, DK_B, DV_B), F32)),
        compiler_params=_params(("arbitrary", "arbitrary")),
        name="retention",
    )(g_t, qb, kbt, vb, gb, state0, dmask, qd, kd, ret_norm_g.reshape(1, V_B))


def _split_bf16(x):
    hi = x.astype(BF16)
    lo = (x - hi.astype(F32)).astype(BF16)
    return hi, lo


def _merge_kernel(x_ref, att_ref, bin_ref, gmix_ref, wga_ref, wgb_ref, wpa_ref, wpb_ref, wout_ref,
                  gffn_ref, wr_hi_ref, wr_lo_ref, br_ref, x1_ref, hx_ref, cls_ref):
    x = x_ref[...]
    h = _rms(x, gmix_ref[...]).astype(BF16)
    gate_a = jnp.dot(h, wga_ref[...], preferred_element_type=F32)
    gate_b = jnp.dot(h, wgb_ref[...], preferred_element_type=F32)
    a = jnp.dot(att_ref[...], wpa_ref[...], preferred_element_type=F32)
    b = jnp.dot(bin_ref[...], wpb_ref[...], preferred_element_type=F32)
    m = jax.nn.sigmoid(gate_a) * a + jax.nn.sigmoid(gate_b) * b
    x1 = x + jnp.dot(m.astype(BF16), wout_ref[...], preferred_element_type=F32)
    x1_ref[...] = x1
    h2 = _rms(x1, gffn_ref[...])
    hx_ref[:, :D_MODEL] = h2

    h_hi, h_lo = _split_bf16(h2)
    nt = (((1,), (1,)), ((), ()))
    lt = (lax.dot_general(wr_hi_ref[...], h_hi, nt, preferred_element_type=F32)
          + lax.dot_general(wr_hi_ref[...], h_lo, nt, preferred_element_type=F32)
          + lax.dot_general(wr_lo_ref[...], h_hi, nt, preferred_element_type=F32)) + br_ref[...]
    tm = lt.shape[1]
    row = lax.broadcasted_iota(jnp.int32, (8, tm), 0)
    lg = lt[0:8, :]
    mg = jnp.max(lg, axis=0, keepdims=True)
    grp = jnp.min(jnp.where(lg == mg, row, 8), axis=0, keepdims=True)
    p_grp = 1.0 / jnp.sum(jnp.exp(lg - mg), axis=0, keepdims=True)
    le = jnp.zeros((8, tm), F32)
    for g in range(N_GROUPS):
        le = jnp.where(grp == g, lt[8 + 8 * g:16 + 8 * g, :], le)
    m0 = jnp.max(le, axis=0, keepdims=True)
    i0 = jnp.min(jnp.where(le == m0, row, 8), axis=0, keepdims=True)
    rest = jnp.where(row == i0, jnp.float32(-jnp.inf), le)
    m1 = jnp.max(rest, axis=0, keepdims=True)
    i1 = jnp.min(jnp.where(rest == m1, row, 8), axis=0, keepdims=True)
    e = jnp.exp(m1 - m0)
    w0 = (1.0 / (1.0 + e)) * p_grp
    w1 = (e / (1.0 + e)) * p_grp
    ea = jnp.minimum(i0, i1)
    eb = jnp.maximum(i0, i1)
    pair = ((ea * (2 * EXPERTS_PER_GROUP - 1 - ea)) >> 1) + (eb - ea - 1)
    cls_ref[...] = jnp.where(row == 0, grp * PAIRS_PER_GROUP + pair, 0)
    wa = jnp.where(i0 < i1, w0, w1)
    wb = jnp.where(i0 < i1, w1, w0)
    wrow = lax.broadcasted_iota(jnp.int32, (LANES, tm), 0)
    wslab = jnp.where(wrow == 0, wa, jnp.where(wrow == 1, wb, 0.0))
    hx_ref[:, D_MODEL:] = wslab.T


def _merge(x2d, att, b_in, lw, *, tm):
    n = x2d.shape[0]
    row = lambda i: (i, 0)
    const = lambda i: (0, 0)
    full = lambda a: pl.BlockSpec(a.shape, const)
    return pl.pallas_call(
        _merge_kernel,
        grid=(n // tm,),
        in_specs=[
            pl.BlockSpec((tm, D_MODEL), row), pl.BlockSpec((tm, W_A), row), pl.BlockSpec((tm, V_B), row),
            full(lw["g_mix"]), full(lw["w_ga"]), full(lw["w_gb"]), full(lw["w_pa"]), full(lw["w_pb"]),
            full(lw["w_out"]), full(lw["g_ffn"]), full(lw["wr_hi"]), full(lw["wr_lo"]), full(lw["b_r"]),
        ],
        out_specs=(pl.BlockSpec((tm, D_MODEL), row), pl.BlockSpec((tm, HX_W), row),
                   pl.BlockSpec((8, tm), lambda i: (0, i))),
        out_shape=(jax.ShapeDtypeStruct((n, D_MODEL), F32), jax.ShapeDtypeStruct((n, HX_W), F32),
                   jax.ShapeDtypeStruct((8, n), jnp.int32)),
        compiler_params=_params(("arbitrary",)),
        name="merge",
    )(x2d, att, b_in, lw["g_mix"], lw["w_ga"], lw["w_gb"], lw["w_pa"], lw["w_pb"], lw["w_out"],
      lw["g_ffn"], lw["wr_hi"], lw["wr_lo"], lw["b_r"])


def _class_onehot(cls_row, base, n_real):
    t = cls_row.shape[1]
    crow = lax.broadcasted_iota(jnp.int32, (CLASS_ROWS, t), 0)
    tok = base + lax.broadcasted_iota(jnp.int32, (CLASS_ROWS, t), 1)
    return (cls_row == crow) & (tok < n_real)


def _rank_kernel(cls_ref, tri_ref, rank_ref, counts_ref, *, n_real):
    i = pl.program_id(0)

    @pl.when(i == 0)
    def _():
        counts_ref[...] = jnp.zeros_like(counts_ref)

    t = tri_ref.shape[0]
    hot = _class_onehot(cls_ref[0:1, :], i * t, n_real)
    incl = jnp.dot(jnp.where(hot, 1.0, 0.0).astype(BF16), tri_ref[...], preferred_element_type=F32)
    carry = counts_ref[:, 0:1]
    rank = jnp.sum(jnp.where(hot, incl + carry, 0.0), axis=0, keepdims=True) - 1.0
    rank_ref[...] = jnp.broadcast_to(rank, rank_ref.shape).astype(jnp.int32)
    counts_ref[...] = counts_ref[...] + incl[:, t - 1:t]


def _slot_kernel(cls_ref, rank_ref, pstart_ref, slot_ref, *, n_real):
    t = cls_ref.shape[1]
    hot = _class_onehot(cls_ref[0:1, :], pl.program_id(0) * t, n_real)
    start = jnp.sum(jnp.where(hot, pstart_ref[...], 0.0), axis=0, keepdims=True)
    slot_ref[...] = jnp.broadcast_to(start.astype(jnp.int32) + rank_ref[0:1, :], slot_ref.shape)


def _routing_plan(cls_all, n_real):
    n_pad = cls_all.shape[1]
    nblk = n_pad // PLAN_T
    tri = jnp.asarray(np.triu(np.ones((PLAN_T, PLAN_T), np.float32)), BF16)
    tok = lambda i: (0, i)
    const = lambda i: (0, 0)
    rank, counts = pl.pallas_call(
        functools.partial(_rank_kernel, n_real=n_real),
        grid=(nblk,),
        in_specs=[pl.BlockSpec((8, PLAN_T), tok), pl.BlockSpec((PLAN_T, PLAN_T), const)],
        out_specs=(pl.BlockSpec((8, PLAN_T), tok), pl.BlockSpec((CLASS_ROWS, LANES), const)),
        out_shape=(jax.ShapeDtypeStruct((8, n_pad), jnp.int32), jax.ShapeDtypeStruct((CLASS_ROWS, LANES), F32)),
        compiler_params=_params(("arbitrary",)),
        name="moe_rank",
    )(cls_all, tri)
    counts = counts[:, 0].astype(jnp.int32)
    psizes = ((counts + MOE_TM - 1) // MOE_TM) * MOE_TM
    pends = jnp.cumsum(psizes)
    pstart = (pends - psizes).astype(F32).reshape(CLASS_ROWS, 1)
    slot = pl.pallas_call(
        functools.partial(_slot_kernel, n_real=n_real),
        grid=(nblk,),
        in_specs=[pl.BlockSpec((8, PLAN_T), tok), pl.BlockSpec((8, PLAN_T), tok),
                  pl.BlockSpec((CLASS_ROWS, 1), const)],
        out_specs=pl.BlockSpec((8, PLAN_T), tok),
        out_shape=jax.ShapeDtypeStruct((8, n_pad), jnp.int32),
        compiler_params=_params(("arbitrary",)),
        name="moe_slot",
    )(cls_all, rank, pstart)
    return slot[0], pends


def _dispatch_kernel(slot_ref, hxp_hbm, hxs_hbm, xs_in_hbm, xs_hbm, sem, *, nb_p, n_s):
    del xs_in_hbm
    i = pl.program_id(0)
    par = i % 2

    def row_copy(src_hbm, base, r, s):
        return pltpu.make_async_copy(src_hbm.at[pl.ds(base + r, 1)], xs_hbm.at[pl.ds(slot_ref[0, 0, r], 1)],
                                     sem.at[s])

    def for_rows(n, fn):
        def body(r, c):
            fn(r)
            return c
        lax.fori_loop(0, n, body, 0, unroll=8)

    def wait_block(n, s):
        for_rows(n, lambda r: row_copy(hxp_hbm, 0, 0, s).wait())

    @pl.when(i < nb_p)
    def _():
        for_rows(PLAN_T, lambda r: row_copy(hxp_hbm, i * PLAN_T, r, par).start())

    @pl.when(i == nb_p)
    def _():
        for_rows(n_s, lambda r: row_copy(hxs_hbm, 0, r, par).start())

    @pl.when(i >= 1)
    def _():
        wait_block(PLAN_T, 1 - par)

    @pl.when(i == nb_p)
    def _():
        wait_block(n_s, par)


def _dispatch(slot, hx_p, hx_s, n_slots):
    n_p, n_s = hx_p.shape[0], hx_s.shape[0]
    nb_p = n_p // PLAN_T
    assert n_p % PLAN_T == 0 and 0 < n_s <= PLAN_T
    slot3 = slot.reshape(-1, 1, PLAN_T)
    assert slot3.shape[0] == nb_p + 1
    any_spec = pl.BlockSpec(memory_space=pl.ANY)
    return pl.pallas_call(
        functools.partial(_dispatch_kernel, nb_p=nb_p, n_s=n_s),
        grid=(nb_p + 1,),
        in_specs=[pl.BlockSpec((1, 1, PLAN_T), lambda i: (i, 0, 0), memory_space=pltpu.SMEM),
                  any_spec, any_spec, any_spec],
        out_specs=any_spec,
        out_shape=jax.ShapeDtypeStruct((n_slots, HX_W), F32),
        scratch_shapes=[pltpu.SemaphoreType.DMA((2,))],
        input_output_aliases={3: 0},
        compiler_params=_params(("arbitrary",)),
        name="moe_dispatch",
    )(slot3, hx_p, hx_s, jnp.zeros((n_slots, HX_W), F32))


def _expert_kernel(ea_ref, eb_ref, nv_ref, xs_ref, wga_ref, wua_ref, wda_ref, wgb_ref, wub_ref, wdb_ref, ys_ref):
    del ea_ref, eb_ref

    def expert(x, wg_ref, wu_ref, wd_ref):
        g = jnp.dot(x, wg_ref[0], preferred_element_type=F32)
        u = jnp.dot(x, wu_ref[0], preferred_element_type=F32)
        act = (g * jax.nn.sigmoid(g) * u).astype(BF16)
        return jnp.dot(act, wd_ref[0], preferred_element_type=F32)

    @pl.when(pl.program_id(0) < nv_ref[0])
    def _():
        x = xs_ref[:, :D_MODEL].astype(BF16)
        wa = xs_ref[:, D_MODEL:D_MODEL + 1]
        wb = xs_ref[:, D_MODEL + 1:D_MODEL + 2]
        ys_ref[...] = expert(x, wga_ref, wua_ref, wda_ref) * wa + expert(x, wgb_ref, wub_ref, wdb_ref) * wb

    @pl.when(pl.program_id(0) >= nv_ref[0])
    def _():
        ys_ref[...] = jnp.zeros_like(ys_ref)


def _class_experts():
    ea, eb = [], []
    for g in range(N_GROUPS):
        for a in range(EXPERTS_PER_GROUP):
            for b in range(a + 1, EXPERTS_PER_GROUP):
                ea.append(g * EXPERTS_PER_GROUP + a)
                eb.append(g * EXPERTS_PER_GROUP + b)
    return np.asarray(ea, np.int32), np.asarray(eb, np.int32)


def _experts(xs, pends, w_gate, w_up, w_down):
    n_tiles = xs.shape[0] // MOE_TM
    nv = (pends[N_CLASSES - 1] // MOE_TM).astype(jnp.int32)
    tile_cls = jnp.sum(pends[None, :N_CLASSES] <= (jnp.arange(n_tiles) * MOE_TM)[:, None], axis=1)
    tile_cls = jnp.minimum(tile_cls, N_CLASSES - 1)
    cls_ea, cls_eb = _class_experts()
    tile_ea = jnp.asarray(cls_ea)[tile_cls]
    tile_eb = jnp.asarray(cls_eb)[tile_cls]
    wa_spec = lambda shp: pl.BlockSpec((1,) + shp, lambda i, ea, eb, nv_: (ea[i], 0, 0))
    wb_spec = lambda shp: pl.BlockSpec((1,) + shp, lambda i, ea, eb, nv_: (eb[i], 0, 0))
    up, down = (D_MODEL, D_EXPERT), (D_EXPERT, D_MODEL)
    grid_spec = pltpu.PrefetchScalarGridSpec(
        num_scalar_prefetch=3,
        grid=(n_tiles,),
        in_specs=[pl.BlockSpec((MOE_TM, HX_W), lambda i, ea, eb, nv_: (i, 0)),
                  wa_spec(up), wa_spec(up), wa_spec(down), wb_spec(up), wb_spec(up), wb_spec(down)],
        out_specs=pl.BlockSpec((MOE_TM, D_MODEL), lambda i, ea, eb, nv_: (i, 0)),
    )
    return pl.pallas_call(
        _expert_kernel,
        grid_spec=grid_spec,
        out_shape=jax.ShapeDtypeStruct((n_tiles * MOE_TM, D_MODEL), F32),
        compiler_params=_params(("arbitrary",)),
        name="moe_experts",
    )(tile_ea, tile_eb, nv.reshape(1), xs, w_gate, w_up, w_down, w_gate, w_up, w_down)


def _ple_kernel(slot_ref, slotn_ref, x1_ref, p_ref, wproj_ref, wgate_ref, gple_ref, gfin_ref, ys_hbm,
                y_ref, ybuf, sem, *, tm):
    i = pl.program_id(0)
    par = i % 2

    def row_copy(idx_ref, r, s):
        return pltpu.make_async_copy(ys_hbm.at[pl.ds(idx_ref[0, 0, r], 1)], ybuf.at[s, pl.ds(r, 1)], sem.at[s])

    def for_rows(fn):
        def body(r, c):
            fn(r)
            return c
        lax.fori_loop(0, tm, body, 0, unroll=8)

    @pl.when(i == 0)
    def _():
        for_rows(lambda r: row_copy(slot_ref, r, 0).start())

    @pl.when(i + 1 < pl.num_programs(0))
    def _():
        for_rows(lambda r: row_copy(slotn_ref, r, 1 - par).start())

    for_rows(lambda r: row_copy(slot_ref, 0, par).wait())

    x2 = x1_ref[...] + ybuf[par]
    proj = jnp.dot(p_ref[...].astype(BF16), wproj_ref[...], preferred_element_type=F32)
    gate = jnp.dot(_rms(x2, gple_ref[...]).astype(BF16), wgate_ref[...], preferred_element_type=F32)
    x3 = x2 + proj * jax.nn.sigmoid(gate)
    y_ref[...] = _rms(x3, gfin_ref[...])


def _ple(x1, ys, slot, p2d, lw, g_final, *, tm):
    n = x1.shape[0]
    nt = n // tm
    slot3 = slot.reshape(nt, 1, tm)
    row = lambda i: (i, 0)
    const = lambda i: (0, 0)
    idx_spec = lambda f: pl.BlockSpec((1, 1, tm), f, memory_space=pltpu.SMEM)
    return pl.pallas_call(
        functools.partial(_ple_kernel, tm=tm),
        grid=(nt,),
        in_specs=[
            idx_spec(lambda i: (i, 0, 0)), idx_spec(lambda i: (jnp.minimum(i + 1, nt - 1), 0, 0)),
            pl.BlockSpec((tm, D_MODEL), row), pl.BlockSpec((tm, D_PLE), row),
            pl.BlockSpec((D_PLE, D_MODEL), const), pl.BlockSpec((D_MODEL, D_MODEL), const),
            pl.BlockSpec((1, D_MODEL), const), pl.BlockSpec((1, D_MODEL), const),
            pl.BlockSpec(memory_space=pl.ANY),
        ],
        out_specs=pl.BlockSpec((tm, D_MODEL), row),
        out_shape=jax.ShapeDtypeStruct((n, D_MODEL), F32),
        scratch_shapes=[pltpu.VMEM((2, tm, D_MODEL), F32), pltpu.SemaphoreType.DMA((2,))],
        compiler_params=_params(("arbitrary",)),
        name="ple",
    )(slot3, slot3, x1, p2d, lw["w_ple_proj"], lw["w_ple_gate"], lw["g_ple"], g_final.reshape(1, D_MODEL), ys)


def _layer_weights(i, norm_mix_g, w_in, ret_norm_g, w_proj_a, w_proj_b, w_out, norm_ffn_g,
                   w_router_group, b_router_group, w_router_expert, b_router_expert,
                   w_gate_e, w_up_e, w_down_e, norm_ple_g, w_ple_gate, w_ple_proj):
    w = w_in[i]
    o = _IN_OFFS
    cols = lambda k: w[:, o[k]:o[k + 1]]
    w_main = jnp.concatenate([cols(0), cols(1), cols(2), cols(3), cols(5), cols(6)], axis=1).astype(BF16)
    wr = jnp.zeros((ROUTER_ROWS, D_MODEL), F32)
    wr = wr.at[:N_GROUPS].set(w_router_group[i].T).at[8:].set(w_router_expert[i].T)
    br = jnp.full((ROUTER_ROWS,), NEG_BIG, F32)
    br = br.at[:N_GROUPS].set(b_router_group[i].astype(F32)).at[8:].set(b_router_expert[i].astype(F32))
    wr_hi, wr_lo = _split_bf16(wr)
    return {
        "g_mix": norm_mix_g[i].reshape(1, D_MODEL), "w_main": w_main, "w_kt": cols(4).T.astype(BF16),
        "w_ga": cols(7).astype(BF16), "w_gb": cols(8).astype(BF16),
        "ret_norm_g": ret_norm_g[i], "w_pa": w_proj_a[i].astype(BF16), "w_pb": w_proj_b[i].astype(BF16),
        "w_out": w_out[i].astype(BF16), "g_ffn": norm_ffn_g[i].reshape(1, D_MODEL),
        "wr_hi": wr_hi, "wr_lo": wr_lo, "b_r": br.reshape(ROUTER_ROWS, 1),
        "w_gate": w_gate_e[i].astype(BF16), "w_up": w_up_e[i].astype(BF16), "w_down": w_down_e[i].astype(BF16),
        "g_ple": norm_ple_g[i].reshape(1, D_MODEL), "w_ple_gate": w_ple_gate[i].astype(BF16),
        "w_ple_proj": w_ple_proj[i].astype(BF16),
    }


def _moe(hx_p, cls_p, hx_s, cls_s, lw):
    n_p, n_s = hx_p.shape[0], hx_s.shape[0]
    n_real = n_p + n_s
    n_pad = (n_p // PLAN_T + 1) * PLAN_T
    cls_all = jnp.concatenate([cls_p, cls_s, jnp.zeros((8, n_pad - n_real), jnp.int32)], axis=1)
    slot, pends = _routing_plan(cls_all, n_real)
    n_slots = (pl.cdiv(n_real, MOE_TM) + N_CLASSES) * MOE_TM
    xs = _dispatch(slot, hx_p, hx_s, n_slots)
    ys = _experts(xs, pends, lw["w_gate"], lw["w_up"], lw["w_down"])
    return ys, slot[:n_p], slot[n_p:n_real]


def kernel(x_prompt, x_sample, cache_k_a, cache_v_a, state_ret, p_prompt, p_sample, norm_mix_g, w_in, rel_bias, ret_norm_g, w_proj_a, w_proj_b, w_out, norm_ffn_g, w_router_group, b_router_group, w_router_expert, b_router_expert, w_gate_e, w_up_e, w_down_e, norm_ple_g, w_ple_gate, w_ple_proj, final_norm_g):
    depth = w_in.shape[0]
    assert depth == 1, "the final norm is fused into the last layer; deeper stacks are not supported"
    bp, sp, _ = x_prompt.shape
    bs, ss, _ = x_sample.shape
    keep = min(WINDOW_A, sp)
    n_cache = cache_k_a.shape[2]
    log_g = jnp.log(1.0 - 2.0 ** (-5.0 - jnp.arange(N_HEADS_B, dtype=F32)))
    i = 0
    lw = _layer_weights(i, norm_mix_g, w_in, ret_norm_g, w_proj_a, w_proj_b, w_out, norm_ffn_g,
                        w_router_group, b_router_group, w_router_expert, b_router_expert,
                        w_gate_e, w_up_e, w_down_e, norm_ple_g, w_ple_gate, w_ple_proj)

    tm = 512
    assert sp % tm == 0 and keep == tm and sp >= ATT_WIN
    t_ret = 128
    xp2 = x_prompt.reshape(bp * sp, D_MODEL)
    qa, ka, va, qb, kbt, vb, gb, ka32, va32 = _project(
        xp2, jnp.arange(sp), lw["g_mix"], lw["w_main"], lw["w_kt"], tm=tm, tiles_per_keep=sp // tm)
    r3 = lambda a: a.reshape(bp, sp, a.shape[-1])
    att = _attention_prompt(r3(qa), r3(ka), r3(va), rel_bias[i])
    per_tile = tm // t_ret
    tiles_per_b = sp // tm
    b_in, s_prompt = _retention(
        r3(qb), kbt, r3(vb), r3(gb), jnp.zeros((bp, N_HEADS_B, DK_B, DV_B), F32), log_g, lw["ret_norm_g"],
        t=t_ret, kt_index=lambda bi, c: (bi * tiles_per_b + c // per_tile, 0, c % per_tile))
    x1_p, hx_p, cls_p = _merge(xp2, att.reshape(bp * sp, W_A), b_in.reshape(bp * sp, V_B), lw, tm=tm)
    k_a_prompt = ka32.reshape(bp, keep, N_HEADS_A, HEAD_DIM_A)
    v_a_prompt = va32.reshape(bp, keep, N_HEADS_A, HEAD_DIM_A)

    ns = bs * ss
    xs2 = x_sample.reshape(ns, D_MODEL)
    pos_s = jnp.tile(PAST_LEN + jnp.arange(ss), bs)
    qa, ka, va, qb, kbt, vb, gb, ka32, va32 = _project(
        xs2, pos_s, lw["g_mix"], lw["w_main"], lw["w_kt"], tm=ns, tiles_per_keep=1)
    r3 = lambda a: a.reshape(bs, ss, a.shape[-1])
    k_all = jnp.concatenate([cache_k_a[i].reshape(bs, n_cache, W_A).astype(BF16), r3(ka)], axis=1)
    v_all = jnp.concatenate([cache_v_a[i].reshape(bs, n_cache, W_A).astype(BF16), r3(va)], axis=1)
    att = _attention_sample(r3(qa), k_all, v_all, rel_bias[i], n_cache)
    kbt_s = kbt.reshape(QK_B, bs, ss).transpose(1, 0, 2)
    b_in, s_sample = _retention(
        r3(qb), kbt_s, r3(vb), r3(gb), state_ret[i].astype(F32), log_g, lw["ret_norm_g"],
        t=ss, kt_index=lambda bi, c: (bi, 0, 0))
    x1_s, hx_s, cls_s = _merge(xs2, att.reshape(ns, W_A), b_in.reshape(ns, V_B), lw, tm=ns)

    ys, slot_p, slot_s = _moe(hx_p, cls_p, hx_s, cls_s, lw)
    y_prompt = _ple(x1_p, ys, slot_p, p_prompt[i].reshape(bp * sp, D_PLE), lw, final_norm_g, tm=tm)
    y_sample = _ple(x1_s, ys, slot_s, p_sample[i].reshape(ns, D_PLE), lw, final_norm_g, tm=ns)
    k_a_sample = ka32.reshape(bs, ss, N_HEADS_A, HEAD_DIM_A)
    v_a_sample = va32.reshape(bs, ss, N_HEADS_A, HEAD_DIM_A)

    return (y_prompt.reshape(bp, sp, D_MODEL), y_sample.reshape(bs, ss, D_MODEL),
            k_a_prompt[None], v_a_prompt[None], s_prompt[None],
            k_a_sample[None], v_a_sample[None], s_sample.astype(state_ret.dtype)[None])
```

```python
import functools

import numpy as np
import jax
import jax.numpy as jnp
from jax import lax
from jax.experimental import pallas as pl
from jax.experimental.pallas import tpu as pltpu

F32 = jnp.float32
BF16 = jnp.bfloat16

D_MODEL = 1024
PAST_LEN = 1024
CHUNK = 64
BAND_CHUNKS = 8
WINDOW_A = BAND_CHUNKS * CHUNK
N_HEADS_A = 8
HEAD_DIM_A = 64
W_A = N_HEADS_A * HEAD_DIM_A
REL_CLIP = 128
N_HEADS_B = 4
DK_B = 128
DV_B = 256
QK_B = N_HEADS_B * DK_B
V_B = N_HEADS_B * DV_B
ROPE_BASE = 10000.0
N_GROUPS = 4
EXPERTS_PER_GROUP = 8
N_EXPERTS = N_GROUPS * EXPERTS_PER_GROUP
TOP_K = 2
D_EXPERT = 512
D_PLE = 256
EPS = 1e-6
_IN_SIZES = (W_A, W_A, W_A, QK_B, QK_B, V_B, V_B, D_MODEL, D_MODEL)
_IN_OFFS = tuple(sum(_IN_SIZES[:i]) for i in range(len(_IN_SIZES) + 1))

LANES = 128
ATT_QBLK = 2 * CHUNK
ATT_WIN = (BAND_CHUNKS + 2) * CHUNK
ROUTER_ROWS = 8 + N_EXPERTS
NEG_BIG = -1e30
PAIRS_PER_GROUP = EXPERTS_PER_GROUP * (EXPERTS_PER_GROUP - 1) // 2
N_CLASSES = N_GROUPS * PAIRS_PER_GROUP
CLASS_ROWS = 128
HX_W = D_MODEL + LANES
PLAN_T = 512
MOE_TM = 256
VMEM_LIMIT = 56 * 1024 * 1024


def _params(sem):
    return pltpu.CompilerParams(dimension_semantics=sem, vmem_limit_bytes=VMEM_LIMIT)


def _rms(x, g):
    return x * lax.rsqrt(jnp.mean(x * x, axis=-1, keepdims=True) + EPS) * g


def _proj_kernel(x_ref, g_ref, w_ref, wkt_ref, cos_ref, sin_ref, cost_ref, sint_ref,
                 qa_ref, ka_ref, va_ref, qb_ref, kbt_ref, vb_ref, gb_ref, ka32_ref, va32_ref,
                 *, tiles_per_keep):
    h = _rms(x_ref[...], g_ref[...]).astype(BF16)

    def seg(lo, hi):
        return jnp.dot(h, w_ref[:, lo:hi], preferred_element_type=F32)

    qa_ref[...] = seg(0, W_A).astype(BF16)
    ka = seg(W_A, 2 * W_A)
    va = seg(2 * W_A, 3 * W_A)
    ka_ref[...] = ka.astype(BF16)
    va_ref[...] = va.astype(BF16)

    @pl.when(pl.program_id(0) % tiles_per_keep == tiles_per_keep - 1)
    def _():
        ka32_ref[...] = ka
        va32_ref[...] = va

    qb = seg(3 * W_A, 3 * W_A + QK_B)
    cos = cos_ref[...]
    sin = sin_ref[...]
    for hd in range(N_HEADS_B):
        xh = qb[:, hd * DK_B:(hd + 1) * DK_B]
        qb_ref[:, hd * DK_B:(hd + 1) * DK_B] = (xh * cos + pltpu.roll(xh, DK_B // 2, axis=1) * sin).astype(BF16)

    vb_ref[...] = seg(3 * W_A + QK_B, 3 * W_A + QK_B + V_B).astype(BF16)
    gb_ref[...] = seg(3 * W_A + QK_B + V_B, 3 * W_A + QK_B + 2 * V_B)

    kt = lax.dot_general(wkt_ref[...], h, (((1,), (1,)), ((), ())), preferred_element_type=F32)
    cost = cost_ref[...]
    sint = sint_ref[...]
    half = DK_B // 2
    scale = DK_B ** -0.5
    for hd in range(N_HEADS_B):
        x1 = kt[hd * DK_B:hd * DK_B + half, :]
        x2 = kt[hd * DK_B + half:(hd + 1) * DK_B, :]
        kbt_ref[0, hd * DK_B:hd * DK_B + half, :] = (x1 * cost - x2 * sint) * scale
        kbt_ref[0, hd * DK_B + half:(hd + 1) * DK_B, :] = (x2 * cost + x1 * sint) * scale


def _rope_tables(pos):
    half = DK_B // 2
    freqs = ROPE_BASE ** (-jnp.arange(half, dtype=F32) / half)
    ang = pos.astype(F32)[:, None] * freqs[None, :]
    cos = jnp.cos(ang)
    sin = jnp.sin(ang)
    return (jnp.concatenate([cos, cos], axis=1), jnp.concatenate([-sin, sin], axis=1), cos.T, sin.T)


def _project(x2d, pos_rows, g_norm, w_main, w_kt, *, tm, tiles_per_keep):
    n = x2d.shape[0]
    period = pos_rows.shape[0]
    nt = n // tm
    ppt = period // tm
    cos2, sin2, cost, sint = _rope_tables(pos_rows)
    n_keep = n // tiles_per_keep
    row = lambda i: (i, 0)
    const = lambda i: (0, 0)
    outs = (
        jax.ShapeDtypeStruct((n, W_A), BF16), jax.ShapeDtypeStruct((n, W_A), BF16),
        jax.ShapeDtypeStruct((n, W_A), BF16), jax.ShapeDtypeStruct((n, QK_B), BF16),
        jax.ShapeDtypeStruct((nt, QK_B, tm), F32), jax.ShapeDtypeStruct((n, V_B), BF16),
        jax.ShapeDtypeStruct((n, V_B), F32),
        jax.ShapeDtypeStruct((n_keep, W_A), F32), jax.ShapeDtypeStruct((n_keep, W_A), F32),
    )
    keep_spec = pl.BlockSpec((tm, W_A), lambda i: (i // tiles_per_keep, 0))
    return pl.pallas_call(
        functools.partial(_proj_kernel, tiles_per_keep=tiles_per_keep),
        grid=(nt,),
        in_specs=[
            pl.BlockSpec((tm, D_MODEL), row),
            pl.BlockSpec((1, D_MODEL), const),
            pl.BlockSpec(w_main.shape, const),
            pl.BlockSpec(w_kt.shape, const),
            pl.BlockSpec((tm, DK_B), lambda i: (i % ppt, 0)),
            pl.BlockSpec((tm, DK_B), lambda i: (i % ppt, 0)),
            pl.BlockSpec((DK_B // 2, tm), lambda i: (0, i % ppt)),
            pl.BlockSpec((DK_B // 2, tm), lambda i: (0, i % ppt)),
        ],
        out_specs=(
            pl.BlockSpec((tm, W_A), row), pl.BlockSpec((tm, W_A), row), pl.BlockSpec((tm, W_A), row),
            pl.BlockSpec((tm, QK_B), row), pl.BlockSpec((1, QK_B, tm), lambda i: (i, 0, 0)),
            pl.BlockSpec((tm, V_B), row), pl.BlockSpec((tm, V_B), row), keep_spec, keep_spec,
        ),
        out_shape=outs,
        compiler_params=_params(("arbitrary",)),
        name="proj",
    )(x2d, g_norm.reshape(1, D_MODEL), w_main, w_kt, cos2, sin2, cost, sint)


def _attend_pairs(q_of, k_of, v_of, bias_of, store):
    for hp in range(N_HEADS_A // 2):
        qp = q_of(hp)
        kw = k_of(hp)
        vw = v_of(hp)
        lane = lax.broadcasted_iota(jnp.int32, qp.shape, 1)
        outs = []
        for hh in range(2):
            in_head = (lane >= hh * HEAD_DIM_A) & (lane < (hh + 1) * HEAD_DIM_A)
            qh = jnp.where(in_head, qp, jnp.zeros_like(qp))
            s = lax.dot_general(qh, kw, (((1,), (1,)), ((), ())), preferred_element_type=F32)
            s = s * (HEAD_DIM_A ** -0.5) + bias_of(2 * hp + hh)
            m = jnp.max(s, axis=-1, keepdims=True)
            p = jnp.exp(s - m)
            l = jnp.sum(p, axis=-1, keepdims=True)
            o = jnp.dot(p.astype(BF16), vw, preferred_element_type=F32)
            outs.append(o / l)
        lane_o = lax.broadcasted_iota(jnp.int32, outs[0].shape, 1)
        store(hp, jnp.where(lane_o < HEAD_DIM_A, outs[0], outs[1]).astype(BF16))


def _attn_prompt_kernel(q_ref, k_ref, v_ref, bias_ref, o_ref):
    j = pl.program_id(1)
    start = pl.multiple_of(jnp.maximum(2 * j - BAND_CHUNKS, 0) * CHUNK, CHUNK)

    def sl(hp):
        return slice(hp * LANES, (hp + 1) * LANES)

    def store(hp, val):
        o_ref[0, :, sl(hp)] = val

    _attend_pairs(
        lambda hp: q_ref[0, :, sl(hp)],
        lambda hp: k_ref[0, pl.ds(start, ATT_WIN), sl(hp)],
        lambda hp: v_ref[0, pl.ds(start, ATT_WIN), sl(hp)],
        lambda hd: bias_ref[0, hd],
        store)


def _band_bias(table):
    i = np.arange(ATT_QBLK)[:, None]
    jk = np.arange(ATT_WIN)[None, :]
    out = []
    for v in range(BAND_CHUNKS // 2 + 1):
        off_chunks = 2 * v if v < BAND_CHUNKS // 2 else BAND_CHUNKS
        dchunk = (off_chunks + i // CHUNK) - jk // CHUNK
        valid = (dchunk >= 0) & (dchunk <= BAND_CHUNKS)
        m = np.arange(ATT_WIN + ATT_QBLK - 1)
        idx = np.clip(off_chunks * CHUNK + (ATT_QBLK - 1) - m, -REL_CLIP, REL_CLIP) + REL_CLIP
        f = table[:, idx].astype(F32)
        b = jnp.stack([f[:, ATT_QBLK - 1 - r:ATT_QBLK - 1 - r + ATT_WIN] for r in range(ATT_QBLK)], axis=1)
        out.append(jnp.where(valid[None], b, jnp.float32(NEG_BIG)))
    return jnp.stack(out)


def _attention_prompt(qa, ka, va, table):
    b, s, _ = qa.shape
    nq = s // ATT_QBLK
    bias = _band_bias(table)
    nvar = bias.shape[0]
    return pl.pallas_call(
        _attn_prompt_kernel,
        grid=(b, nq),
        in_specs=[
            pl.BlockSpec((1, ATT_QBLK, W_A), lambda bi, j: (bi, j, 0)),
            pl.BlockSpec((1, s, W_A), lambda bi, j: (bi, 0, 0)),
            pl.BlockSpec((1, s, W_A), lambda bi, j: (bi, 0, 0)),
            pl.BlockSpec((1, N_HEADS_A, ATT_QBLK, ATT_WIN), lambda bi, j: (jnp.minimum(j, nvar - 1), 0, 0, 0)),
        ],
        out_specs=pl.BlockSpec((1, ATT_QBLK, W_A), lambda bi, j: (bi, j, 0)),
        out_shape=jax.ShapeDtypeStruct((b, s, W_A), BF16),
        compiler_params=_params(("arbitrary", "arbitrary")),
        name="attn_prompt",
    )(qa, ka, va, bias)


def _attn_sample_kernel(q_ref, k_ref, v_ref, bias_ref, o_ref):
    def sl(hp):
        return slice(hp * LANES, (hp + 1) * LANES)

    def store(hp, val):
        o_ref[0, :, sl(hp)] = val

    _attend_pairs(
        lambda hp: q_ref[0, :, sl(hp)],
        lambda hp: k_ref[0, :, sl(hp)],
        lambda hp: v_ref[0, :, sl(hp)],
        lambda hd: bias_ref[hd],
        store)


def _attention_sample(qa, k_all, v_all, table, n_cache):
    b, n, _ = qa.shape
    nk = k_all.shape[1]
    dist = jnp.arange(n)[:, None] + n_cache - jnp.arange(nk)[None, :]
    bias = table[:, jnp.clip(dist, -REL_CLIP, REL_CLIP) + REL_CLIP].astype(F32)
    return pl.pallas_call(
        _attn_sample_kernel,
        grid=(b,),
        in_specs=[
            pl.BlockSpec((1, n, W_A), lambda bi: (bi, 0, 0)),
            pl.BlockSpec((1, nk, W_A), lambda bi: (bi, 0, 0)),
            pl.BlockSpec((1, nk, W_A), lambda bi: (bi, 0, 0)),
            pl.BlockSpec((N_HEADS_A, n, nk), lambda bi: (0, 0, 0)),
        ],
        out_specs=pl.BlockSpec((1, n, W_A), lambda bi: (bi, 0, 0)),
        out_shape=jax.ShapeDtypeStruct((b, n, W_A), BF16),
        compiler_params=_params(("arbitrary",)),
        name="attn_sample",
    )(qa, k_all, v_all, bias)


def _ret_kernel(gt_ref, q_ref, kt_ref, v_ref, gb_ref, s0_ref, dmask_ref, qd_ref, kd_ref, gn_ref,
                out_ref, state_ref):
    @pl.when(pl.program_id(1) == 0)
    def _():
        state_ref[...] = s0_ref[...]

    for hd in range(N_HEADS_B):
        qs = slice(hd * DK_B, (hd + 1) * DK_B)
        vs = slice(hd * DV_B, (hd + 1) * DV_B)
        q = q_ref[0, :, qs]
        kt = kt_ref[0, qs, :]
        v = v_ref[0, :, vs]
        state = state_ref[0, hd]
        scores = jnp.dot(q, kt.astype(BF16), preferred_element_type=F32) * dmask_ref[hd]
        o = jnp.dot(scores.astype(BF16), v, preferred_element_type=F32)
        o = o + jnp.dot(q, state.astype(BF16), preferred_element_type=F32) * qd_ref[hd]
        kd = (kt * kd_ref[hd]).astype(BF16)
        state_ref[0, hd] = state * gt_ref[hd] + jnp.dot(kd, v, preferred_element_type=F32)
        mu = jnp.mean(o, axis=-1, keepdims=True)
        var = jnp.mean(jnp.square(o - mu), axis=-1, keepdims=True)
        rb = (o - mu) * lax.rsqrt(var + EPS) * gn_ref[:, vs]
        gb = gb_ref[0, :, vs]
        out_ref[0, :, vs] = (gb * jax.nn.sigmoid(gb) * rb).astype(BF16)


def _retention(qb, kbt, vb, gb, state0, log_g, ret_norm_g, *, t, kt_index):
    b, s, _ = qb.shape
    nc = s // t
    idx = jnp.arange(t, dtype=F32)
    diff = idx[:, None] - idx[None, :]
    dmask = jnp.where(diff[None] >= 0, jnp.exp(log_g[:, None, None] * jnp.maximum(diff, 0.0)[None]), 0.0)
    q_decay = jnp.exp(log_g[:, None] * (idx[None, :] + 1.0))
    k_decay = jnp.exp(log_g[:, None] * (t - 1.0 - idx[None, :]))
    g_t = jnp.exp(log_g * t)
    qd = jnp.broadcast_to(q_decay[:, :, None], (N_HEADS_B, t, DV_B))
    kd = k_decay[:, None, :]
    const3 = lambda bi, c: (0, 0, 0)
    return pl.pallas_call(
        _ret_kernel,
        grid=(b, nc),
        in_specs=[
            pl.BlockSpec(memory_space=pltpu.SMEM),
            pl.BlockSpec((1, t, QK_B), lambda bi, c: (bi, c, 0)),
            pl.BlockSpec((1, QK_B, t), kt_index),
            pl.BlockSpec((1, t, V_B), lambda bi, c: (bi, c, 0)),
            pl.BlockSpec((1, t, V_B), lambda bi, c: (bi, c, 0)),
            pl.BlockSpec((1, N_HEADS_B, DK_B, DV_B), lambda bi, c: (bi, 0, 0, 0)),
            pl.BlockSpec((N_HEADS_B, t, t), const3),
            pl.BlockSpec((N_HEADS_B, t, DV_B), const3),
            pl.BlockSpec((N_HEADS_B, 1, t), const3),
            pl.BlockSpec((1, V_B), lambda bi, c: (0, 0)),
        ],
        out_specs=(
            pl.BlockSpec((1, t, V_B), lambda bi, c: (bi, c, 0)),
            pl.BlockSpec((1, N_HEADS_B, DK_B, DV_B), lambda bi, c: (bi, 0, 0, 0)),
        ),
        out_shape=(jax.ShapeDtypeStruct((b, s, V_B), BF16),
                   jax.ShapeDtypeStruct((b, N_HEADS_B, DK_B, DV_B), F32)),
        compiler_params=_params(("arbitrary", "arbitrary")),
        name="retention",
    )(g_t, qb, kbt, vb, gb, state0, dmask, qd, kd, ret_norm_g.reshape(1, V_B))


def _split_bf16(x):
    hi = x.astype(BF16)
    lo = (x - hi.astype(F32)).astype(BF16)
    return hi, lo


def _merge_kernel(x_ref, att_ref, bin_ref, gmix_ref, wga_ref, wgb_ref, wpa_ref, wpb_ref, wout_ref,
                  gffn_ref, wr_hi_ref, wr_lo_ref, br_ref, x1_ref, hx_ref, cls_ref):
    x = x_ref[...]
    h = _rms(x, gmix_ref[...]).astype(BF16)
    gate_a = jnp.dot(h, wga_ref[...], preferred_element_type=F32)
    gate_b = jnp.dot(h, wgb_ref[...], preferred_element_type=F32)
    a = jnp.dot(att_ref[...], wpa_ref[...], preferred_element_type=F32)
    b = jnp.dot(bin_ref[...], wpb_ref[...], preferred_element_type=F32)
    m = jax.nn.sigmoid(gate_a) * a + jax.nn.sigmoid(gate_b) * b
    x1 = x + jnp.dot(m.astype(BF16), wout_ref[...], preferred_element_type=F32)
    x1_ref[...] = x1
    h2 = _rms(x1, gffn_ref[...])
    hx_ref[:, :D_MODEL] = h2

    h_hi, h_lo = _split_bf16(h2)
    nt = (((1,), (1,)), ((), ()))
    lt = (lax.dot_general(wr_hi_ref[...], h_hi, nt, preferred_element_type=F32)
          + lax.dot_general(wr_hi_ref[...], h_lo, nt, preferred_element_type=F32)
          + lax.dot_general(wr_lo_ref[...], h_hi, nt, preferred_element_type=F32)) + br_ref[...]
    tm = lt.shape[1]
    row = lax.broadcasted_iota(jnp.int32, (8, tm), 0)
    lg = lt[0:8, :]
    mg = jnp.max(lg, axis=0, keepdims=True)
    grp = jnp.min(jnp.where(lg == mg, row, 8), axis=0, keepdims=True)
    p_grp = 1.0 / jnp.sum(jnp.exp(lg - mg), axis=0, keepdims=True)
    le = jnp.zeros((8, tm), F32)
    for g in range(N_GROUPS):
        le = jnp.where(grp == g, lt[8 + 8 * g:16 + 8 * g, :], le)
    m0 = jnp.max(le, axis=0, keepdims=True)
    i0 = jnp.min(jnp.where(le == m0, row, 8), axis=0, keepdims=True)
    rest = jnp.where(row == i0, jnp.float32(-jnp.inf), le)
    m1 = jnp.max(rest, axis=0, keepdims=True)
    i1 = jnp.min(jnp.where(rest == m1, row, 8), axis=0, keepdims=True)
    e = jnp.exp(m1 - m0)
    w0 = (1.0 / (1.0 + e)) * p_grp
    w1 = (e / (1.0 + e)) * p_grp
    ea = jnp.minimum(i0, i1)
    eb = jnp.maximum(i0, i1)
    pair = ((ea * (2 * EXPERTS_PER_GROUP - 1 - ea)) >> 1) + (eb - ea - 1)
    cls_ref[...] = jnp.where(row == 0, grp * PAIRS_PER_GROUP + pair, 0)
    wa = jnp.where(i0 < i1, w0, w1)
    wb = jnp.where(i0 < i1, w1, w0)
    wrow = lax.broadcasted_iota(jnp.int32, (LANES, tm), 0)
    wslab = jnp.where(wrow == 0, wa, jnp.where(wrow == 1, wb, 0.0))
    hx_ref[:, D_MODEL:] = wslab.T


def _merge(x2d, att, b_in, lw, *, tm):
    n = x2d.shape[0]
    row = lambda i: (i, 0)
    const = lambda i: (0, 0)
    full = lambda a: pl.BlockSpec(a.shape, const)
    return pl.pallas_call(
        _merge_kernel,
        grid=(n // tm,),
        in_specs=[
            pl.BlockSpec((tm, D_MODEL), row), pl.BlockSpec((tm, W_A), row), pl.BlockSpec((tm, V_B), row),
            full(lw["g_mix"]), full(lw["w_ga"]), full(lw["w_gb"]), full(lw["w_pa"]), full(lw["w_pb"]),
            full(lw["w_out"]), full(lw["g_ffn"]), full(lw["wr_hi"]), full(lw["wr_lo"]), full(lw["b_r"]),
        ],
        out_specs=(pl.BlockSpec((tm, D_MODEL), row), pl.BlockSpec((tm, HX_W), row),
                   pl.BlockSpec((8, tm), lambda i: (0, i))),
        out_shape=(jax.ShapeDtypeStruct((n, D_MODEL), F32), jax.ShapeDtypeStruct((n, HX_W), F32),
                   jax.ShapeDtypeStruct((8, n), jnp.int32)),
        compiler_params=_params(("arbitrary",)),
        name="merge",
    )(x2d, att, b_in, lw["g_mix"], lw["w_ga"], lw["w_gb"], lw["w_pa"], lw["w_pb"], lw["w_out"],
      lw["g_ffn"], lw["wr_hi"], lw["wr_lo"], lw["b_r"])


def _class_onehot(cls_row, base, n_real):
    t = cls_row.shape[1]
    crow = lax.broadcasted_iota(jnp.int32, (CLASS_ROWS, t), 0)
    tok = base + lax.broadcasted_iota(jnp.int32, (CLASS_ROWS, t), 1)
    return (cls_row == crow) & (tok < n_real)


def _rank_kernel(cls_ref, tri_ref, rank_ref, counts_ref, *, n_real):
    i = pl.program_id(0)

    @pl.when(i == 0)
    def _():
        counts_ref[...] = jnp.zeros_like(counts_ref)

    t = tri_ref.shape[0]
    hot = _class_onehot(cls_ref[0:1, :], i * t, n_real)
    incl = jnp.dot(jnp.where(hot, 1.0, 0.0).astype(BF16), tri_ref[...], preferred_element_type=F32)
    carry = counts_ref[:, 0:1]
    rank = jnp.sum(jnp.where(hot, incl + carry, 0.0), axis=0, keepdims=True) - 1.0
    rank_ref[...] = jnp.broadcast_to(rank, rank_ref.shape).astype(jnp.int32)
    counts_ref[...] = counts_ref[...] + incl[:, t - 1:t]


def _slot_kernel(cls_ref, rank_ref, pstart_ref, slot_ref, *, n_real):
    t = cls_ref.shape[1]
    hot = _class_onehot(cls_ref[0:1, :], pl.program_id(0) * t, n_real)
    start = jnp.sum(jnp.where(hot, pstart_ref[...], 0.0), axis=0, keepdims=True)
    slot_ref[...] = jnp.broadcast_to(start.astype(jnp.int32) + rank_ref[0:1, :], slot_ref.shape)


def _routing_plan(cls_all, n_real):
    n_pad = cls_all.shape[1]
    nblk = n_pad // PLAN_T
    tri = jnp.asarray(np.triu(np.ones((PLAN_T, PLAN_T), np.float32)), BF16)
    tok = lambda i: (0, i)
    const = lambda i: (0, 0)
    rank, counts = pl.pallas_call(
        functools.partial(_rank_kernel, n_real=n_real),
        grid=(nblk,),
        in_specs=[pl.BlockSpec((8, PLAN_T), tok), pl.BlockSpec((PLAN_T, PLAN_T), const)],
        out_specs=(pl.BlockSpec((8, PLAN_T), tok), pl.BlockSpec((CLASS_ROWS, LANES), const)),
        out_shape=(jax.ShapeDtypeStruct((8, n_pad), jnp.int32), jax.ShapeDtypeStruct((CLASS_ROWS, LANES), F32)),
        compiler_params=_params(("arbitrary",)),
        name="moe_rank",
    )(cls_all, tri)
    counts = counts[:, 0].astype(jnp.int32)
    psizes = ((counts + MOE_TM - 1) // MOE_TM) * MOE_TM
    pends = jnp.cumsum(psizes)
    pstart = (pends - psizes).astype(F32).reshape(CLASS_ROWS, 1)
    slot = pl.pallas_call(
        functools.partial(_slot_kernel, n_real=n_real),
        grid=(nblk,),
        in_specs=[pl.BlockSpec((8, PLAN_T), tok), pl.BlockSpec((8, PLAN_T), tok),
                  pl.BlockSpec((CLASS_ROWS, 1), const)],
        out_specs=pl.BlockSpec((8, PLAN_T), tok),
        out_shape=jax.ShapeDtypeStruct((8, n_pad), jnp.int32),
        compiler_params=_params(("arbitrary",)),
        name="moe_slot",
    )(cls_all, rank, pstart)
    return slot[0], pends


def _dispatch_kernel(slot_ref, hxp_ref, hxs_ref, xs_in_hbm, xs_hbm, buf, sem, *, nb_p, n_s):
    del xs_in_hbm
    i = pl.program_id(0)
    par = i % 2

    def row_copy(r, s):
        return pltpu.make_async_copy(buf.at[s, pl.ds(r, 1)], xs_hbm.at[pl.ds(slot_ref[0, 0, r], 1)], sem.at[s])

    def for_rows(n, fn):
        def body(r, c):
            fn(r)
            return c
        lax.fori_loop(0, n, body, 0, unroll=8)

    def wait_block(n, s):
        for_rows(n, lambda r: row_copy(0, s).wait())

    @pl.when(i >= 2)
    def _():
        wait_block(PLAN_T, par)

    @pl.when(i < nb_p)
    def _():
        buf[par] = hxp_ref[...]
        for_rows(PLAN_T, lambda r: row_copy(r, par).start())

    @pl.when(i == nb_p)
    def _():
        buf[par, :n_s] = hxs_ref[...]
        for_rows(n_s, lambda r: row_copy(r, par).start())
        if nb_p >= 1:
            wait_block(PLAN_T, 1 - par)
        wait_block(n_s, par)


def _dispatch(slot, hx_p, hx_s, n_slots):
    n_p, n_s = hx_p.shape[0], hx_s.shape[0]
    nb_p = n_p // PLAN_T
    assert n_p % PLAN_T == 0 and 0 < n_s <= PLAN_T
    slot3 = slot.reshape(-1, 1, PLAN_T)
    assert slot3.shape[0] == nb_p + 1
    any_spec = pl.BlockSpec(memory_space=pl.ANY)
    return pl.pallas_call(
        functools.partial(_dispatch_kernel, nb_p=nb_p, n_s=n_s),
        grid=(nb_p + 1,),
        in_specs=[pl.BlockSpec((1, 1, PLAN_T), lambda i: (i, 0, 0), memory_space=pltpu.SMEM),
                  pl.BlockSpec((PLAN_T, HX_W), lambda i: (jnp.minimum(i, nb_p - 1), 0)),
                  pl.BlockSpec((n_s, HX_W), lambda i: (0, 0)), any_spec],
        out_specs=any_spec,
        out_shape=jax.ShapeDtypeStruct((n_slots, HX_W), F32),
        scratch_shapes=[pltpu.VMEM((2, PLAN_T, HX_W), F32), pltpu.SemaphoreType.DMA((2,))],
        input_output_aliases={3: 0},
        compiler_params=_params(("arbitrary",)),
        name="moe_dispatch",
    )(slot3, hx_p, hx_s, jnp.zeros((n_slots, HX_W), F32))


def _expert_kernel(ea_ref, eb_ref, nv_ref, xs_ref, wga_ref, wua_ref, wda_ref, wgb_ref, wub_ref, wdb_ref, ys_ref):
    del ea_ref, eb_ref

    def expert(x, wg_ref, wu_ref, wd_ref):
        g = jnp.dot(x, wg_ref[0], preferred_element_type=F32)
        u = jnp.dot(x, wu_ref[0], preferred_element_type=F32)
        act = (g * jax.nn.sigmoid(g) * u).astype(BF16)
        return jnp.dot(act, wd_ref[0], preferred_element_type=F32)

    @pl.when(pl.program_id(0) < nv_ref[0])
    def _():
        x = xs_ref[:, :D_MODEL].astype(BF16)
        wa = xs_ref[:, D_MODEL:D_MODEL + 1]
        wb = xs_ref[:, D_MODEL + 1:D_MODEL + 2]
        ys_ref[...] = expert(x, wga_ref, wua_ref, wda_ref) * wa + expert(x, wgb_ref, wub_ref, wdb_ref) * wb

    @pl.when(pl.program_id(0) >= nv_ref[0])
    def _():
        ys_ref[...] = jnp.zeros_like(ys_ref)


def _class_experts():
    ea, eb = [], []
    for g in range(N_GROUPS):
        for a in range(EXPERTS_PER_GROUP):
            for b in range(a + 1, EXPERTS_PER_GROUP):
                ea.append(g * EXPERTS_PER_GROUP + a)
                eb.append(g * EXPERTS_PER_GROUP + b)
    return np.asarray(ea, np.int32), np.asarray(eb, np.int32)


def _experts(xs, pends, w_gate, w_up, w_down):
    n_tiles = xs.shape[0] // MOE_TM
    nv = (pends[N_CLASSES - 1] // MOE_TM).astype(jnp.int32)
    tile_cls = jnp.sum(pends[None, :N_CLASSES] <= (jnp.arange(n_tiles) * MOE_TM)[:, None], axis=1)
    tile_cls = jnp.minimum(tile_cls, N_CLASSES - 1)
    cls_ea, cls_eb = _class_experts()
    tile_ea = jnp.asarray(cls_ea)[tile_cls]
    tile_eb = jnp.asarray(cls_eb)[tile_cls]
    wa_spec = lambda shp: pl.BlockSpec((1,) + shp, lambda i, ea, eb, nv_: (ea[i], 0, 0))
    wb_spec = lambda shp: pl.BlockSpec((1,) + shp, lambda i, ea, eb, nv_: (eb[i], 0, 0))
    up, down = (D_MODEL, D_EXPERT), (D_EXPERT, D_MODEL)
    grid_spec = pltpu.PrefetchScalarGridSpec(
        num_scalar_prefetch=3,
        grid=(n_tiles,),
        in_specs=[pl.BlockSpec((MOE_TM, HX_W), lambda i, ea, eb, nv_: (i, 0)),
                  wa_spec(up), wa_spec(up), wa_spec(down), wb_spec(up), wb_spec(up), wb_spec(down)],
        out_specs=pl.BlockSpec((MOE_TM, D_MODEL), lambda i, ea, eb, nv_: (i, 0)),
    )
    return pl.pallas_call(
        _expert_kernel,
        grid_spec=grid_spec,
        out_shape=jax.ShapeDtypeStruct((n_tiles * MOE_TM, D_MODEL), F32),
        compiler_params=_params(("arbitrary",)),
        name="moe_experts",
    )(tile_ea, tile_eb, nv.reshape(1), xs, w_gate, w_up, w_down, w_gate, w_up, w_down)


def _ple_kernel(slot_ref, slotn_ref, x1_ref, p_ref, wproj_ref, wgate_ref, gple_ref, gfin_ref, ys_hbm,
                y_ref, ybuf, sem, *, tm):
    i = pl.program_id(0)
    par = i % 2

    def row_copy(idx_ref, r, s):
        return pltpu.make_async_copy(ys_hbm.at[pl.ds(idx_ref[0, 0, r], 1)], ybuf.at[s, pl.ds(r, 1)], sem.at[s])

    def for_rows(fn):
        def body(r, c):
            fn(r)
            return c
        lax.fori_loop(0, tm, body, 0, unroll=8)

    @pl.when(i == 0)
    def _():
        for_rows(lambda r: row_copy(slot_ref, r, 0).start())

    @pl.when(i + 1 < pl.num_programs(0))
    def _():
        for_rows(lambda r: row_copy(slotn_ref, r, 1 - par).start())

    for_rows(lambda r: row_copy(slot_ref, 0, par).wait())

    x2 = x1_ref[...] + ybuf[par]
    proj = jnp.dot(p_ref[...].astype(BF16), wproj_ref[...], preferred_element_type=F32)
    gate = jnp.dot(_rms(x2, gple_ref[...]).astype(BF16), wgate_ref[...], preferred_element_type=F32)
    x3 = x2 + proj * jax.nn.sigmoid(gate)
    y_ref[...] = _rms(x3, gfin_ref[...])


def _ple(x1, ys, slot, p2d, lw, g_final, *, tm):
    n = x1.shape[0]
    nt = n // tm
    slot3 = slot.reshape(nt, 1, tm)
    row = lambda i: (i, 0)
    const = lambda i: (0, 0)
    idx_spec = lambda f: pl.BlockSpec((1, 1, tm), f, memory_space=pltpu.SMEM)
    return pl.pallas_call(
        functools.partial(_ple_kernel, tm=tm),
        grid=(nt,),
        in_specs=[
            idx_spec(lambda i: (i, 0, 0)), idx_spec(lambda i: (jnp.minimum(i + 1, nt - 1), 0, 0)),
            pl.BlockSpec((tm, D_MODEL), row), pl.BlockSpec((tm, D_PLE), row),
            pl.BlockSpec((D_PLE, D_MODEL), const), pl.BlockSpec((D_MODEL, D_MODEL), const),
            pl.BlockSpec((1, D_MODEL), const), pl.BlockSpec((1, D_MODEL), const),
            pl.BlockSpec(memory_space=pl.ANY),
        ],
        out_specs=pl.BlockSpec((tm, D_MODEL), row),
        out_shape=jax.ShapeDtypeStruct((n, D_MODEL), F32),
        scratch_shapes=[pltpu.VMEM((2, tm, D_MODEL), F32), pltpu.SemaphoreType.DMA((2,))],
        compiler_params=_params(("arbitrary",)),
        name="ple",
    )(slot3, slot3, x1, p2d, lw["w_ple_proj"], lw["w_ple_gate"], lw["g_ple"], g_final.reshape(1, D_MODEL), ys)


def _layer_weights(i, norm_mix_g, w_in, ret_norm_g, w_proj_a, w_proj_b, w_out, norm_ffn_g,
                   w_router_group, b_router_group, w_router_expert, b_router_expert,
                   w_gate_e, w_up_e, w_down_e, norm_ple_g, w_ple_gate, w_ple_proj):
    w = w_in[i]
    o = _IN_OFFS
    cols = lambda k: w[:, o[k]:o[k + 1]]
    w_main = jnp.concatenate([cols(0), cols(1), cols(2), cols(3), cols(5), cols(6)], axis=1).astype(BF16)
    wr = jnp.zeros((ROUTER_ROWS, D_MODEL), F32)
    wr = wr.at[:N_GROUPS].set(w_router_group[i].T).at[8:].set(w_router_expert[i].T)
    br = jnp.full((ROUTER_ROWS,), NEG_BIG, F32)
    br = br.at[:N_GROUPS].set(b_router_group[i].astype(F32)).at[8:].set(b_router_expert[i].astype(F32))
    wr_hi, wr_lo = _split_bf16(wr)
    return {
        "g_mix": norm_mix_g[i].reshape(1, D_MODEL), "w_main": w_main, "w_kt": cols(4).T.astype(BF16),
        "w_ga": cols(7).astype(BF16), "w_gb": cols(8).astype(BF16),
        "ret_norm_g": ret_norm_g[i], "w_pa": w_proj_a[i].astype(BF16), "w_pb": w_proj_b[i].astype(BF16),
        "w_out": w_out[i].astype(BF16), "g_ffn": norm_ffn_g[i].reshape(1, D_MODEL),
        "wr_hi": wr_hi, "wr_lo": wr_lo, "b_r": br.reshape(ROUTER_ROWS, 1),
        "w_gate": w_gate_e[i].astype(BF16), "w_up": w_up_e[i].astype(BF16), "w_down": w_down_e[i].astype(BF16),
        "g_ple": norm_ple_g[i].reshape(1, D_MODEL), "w_ple_gate": w_ple_gate[i].astype(BF16),
        "w_ple_proj": w_ple_proj[i].astype(BF16),
    }


def _moe(hx_p, cls_p, hx_s, cls_s, lw):
    n_p, n_s = hx_p.shape[0], hx_s.shape[0]
    n_real = n_p + n_s
    n_pad = (n_p // PLAN_T + 1) * PLAN_T
    cls_all = jnp.concatenate([cls_p, cls_s, jnp.zeros((8, n_pad - n_real), jnp.int32)], axis=1)
    slot, pends = _routing_plan(cls_all, n_real)
    n_slots = (pl.cdiv(n_real, MOE_TM) + N_CLASSES) * MOE_TM
    xs = _dispatch(slot, hx_p, hx_s, n_slots)
    ys = _experts(xs, pends, lw["w_gate"], lw["w_up"], lw["w_down"])
    return ys, slot[:n_p], slot[n_p:n_real]


def kernel(x_prompt, x_sample, cache_k_a, cache_v_a, state_ret, p_prompt, p_sample, norm_mix_g, w_in, rel_bias, ret_norm_g, w_proj_a, w_proj_b, w_out, norm_ffn_g, w_router_group, b_router_group, w_router_expert, b_router_expert, w_gate_e, w_up_e, w_down_e, norm_ple_g, w_ple_gate, w_ple_proj, final_norm_g):
    depth = w_in.shape[0]
    assert depth == 1, "the final norm is fused into the last layer; deeper stacks are not supported"
    bp, sp, _ = x_prompt.shape
    bs, ss, _ = x_sample.shape
    keep = min(WINDOW_A, sp)
    n_cache = cache_k_a.shape[2]
    log_g = jnp.log(1.0 - 2.0 ** (-5.0 - jnp.arange(N_HEADS_B, dtype=F32)))
    i = 0
    lw = _layer_weights(i, norm_mix_g, w_in, ret_norm_g, w_proj_a, w_proj_b, w_out, norm_ffn_g,
                        w_router_group, b_router_group, w_router_expert, b_router_expert,
                        w_gate_e, w_up_e, w_down_e, norm_ple_g, w_ple_gate, w_ple_proj)

    tm = 512
    assert sp % tm == 0 and keep == tm and sp >= ATT_WIN
    t_ret = 128
    xp2 = x_prompt.reshape(bp * sp, D_MODEL)
    qa, ka, va, qb, kbt, vb, gb, ka32, va32 = _project(
        xp2, jnp.arange(sp), lw["g_mix"], lw["w_main"], lw["w_kt"], tm=tm, tiles_per_keep=sp // tm)
    r3 = lambda a: a.reshape(bp, sp, a.shape[-1])
    att = _attention_prompt(r3(qa), r3(ka), r3(va), rel_bias[i])
    per_tile = tm // t_ret
    tiles_per_b = sp // tm
    b_in, s_prompt = _retention(
        r3(qb), kbt, r3(vb), r3(gb), jnp.zeros((bp, N_HEADS_B, DK_B, DV_B), F32), log_g, lw["ret_norm_g"],
        t=t_ret, kt_index=lambda bi, c: (bi * tiles_per_b + c // per_tile, 0, c % per_tile))
    x1_p, hx_p, cls_p = _merge(xp2, att.reshape(bp * sp, W_A), b_in.reshape(bp * sp, V_B), lw, tm=tm)
    k_a_prompt = ka32.reshape(bp, keep, N_HEADS_A, HEAD_DIM_A)
    v_a_prompt = va32.reshape(bp, keep, N_HEADS_A, HEAD_DIM_A)

    ns = bs * ss
    xs2 = x_sample.reshape(ns, D_MODEL)
    pos_s = jnp.tile(PAST_LEN + jnp.arange(ss), bs)
    qa, ka, va, qb, kbt, vb, gb, ka32, va32 = _project(
        xs2, pos_s, lw["g_mix"], lw["w_main"], lw["w_kt"], tm=ns, tiles_per_keep=1)
    r3 = lambda a: a.reshape(bs, ss, a.shape[-1])
    k_all = jnp.concatenate([cache_k_a[i].reshape(bs, n_cache, W_A).astype(BF16), r3(ka)], axis=1)
    v_all = jnp.concatenate([cache_v_a[i].reshape(bs, n_cache, W_A).astype(BF16), r3(va)], axis=1)
    att = _attention_sample(r3(qa), k_all, v_all, rel_bias[i], n_cache)
    kbt_s = kbt.reshape(QK_B, bs, ss).transpose(1, 0, 2)
    b_in, s_sample = _retention(
        r3(qb), kbt_s, r3(vb), r3(gb), state_ret[i].astype(F32), log_g, lw["ret_norm_g"],
        t=ss, kt_index=lambda bi, c: (bi, 0, 0))
    x1_s, hx_s, cls_s = _merge(xs2, att.reshape(ns, W_A), b_in.reshape(ns, V_B), lw, tm=ns)

    ys, slot_p, slot_s = _moe(hx_p, cls_p, hx_s, cls_s, lw)
    y_prompt = _ple(x1_p, ys, slot_p, p_prompt[i].reshape(bp * sp, D_PLE), lw, final_norm_g, tm=tm)
    y_sample = _ple(x1_s, ys, slot_s, p_sample[i].reshape(ns, D_PLE), lw, final_norm_g, tm=ns)
    k_a_sample = ka32.reshape(bs, ss, N_HEADS_A, HEAD_DIM_A)
    v_a_sample = va32.reshape(bs, ss, N_HEADS_A, HEAD_DIM_A)

    return (y_prompt.reshape(bp, sp, D_MODEL), y_sample.reshape(bs, ss, D_MODEL),
            k_a_prompt[None], v_a_prompt[None], s_prompt[None],
            k_a_sample[None], v_a_sample[None], s_sample.astype(state_ret.dtype)[None])
```

```python
import functools

import numpy as np
import jax
import jax.numpy as jnp
from jax import lax
from jax.experimental import pallas as pl
from jax.experimental.pallas import tpu as pltpu

F32 = jnp.float32
BF16 = jnp.bfloat16

D_MODEL = 1024
PAST_LEN = 1024
CHUNK = 64
BAND_CHUNKS = 8
WINDOW_A = BAND_CHUNKS * CHUNK
N_HEADS_A = 8
HEAD_DIM_A = 64
W_A = N_HEADS_A * HEAD_DIM_A
REL_CLIP = 128
N_HEADS_B = 4
DK_B = 128
DV_B = 256
QK_B = N_HEADS_B * DK_B
V_B = N_HEADS_B * DV_B
ROPE_BASE = 10000.0
N_GROUPS = 4
EXPERTS_PER_GROUP = 8
N_EXPERTS = N_GROUPS * EXPERTS_PER_GROUP
TOP_K = 2
D_EXPERT = 512
D_PLE = 256
EPS = 1e-6
_IN_SIZES = (W_A, W_A, W_A, QK_B, QK_B, V_B, V_B, D_MODEL, D_MODEL)
_IN_OFFS = tuple(sum(_IN_SIZES[:i]) for i in range(len(_IN_SIZES) + 1))

LANES = 128
ATT_QBLK = 2 * CHUNK
ATT_WIN = (BAND_CHUNKS + 2) * CHUNK
ROUTER_ROWS = 8 + N_EXPERTS
NEG_BIG = -1e30
PAIRS_PER_GROUP = EXPERTS_PER_GROUP * (EXPERTS_PER_GROUP - 1) // 2
N_CLASSES = N_GROUPS * PAIRS_PER_GROUP
CLASS_ROWS = 128
HX_W = D_MODEL + LANES
PLAN_T = 512
MOE_TM = 256
VMEM_LIMIT = 56 * 1024 * 1024


def _params(sem):
    return pltpu.CompilerParams(dimension_semantics=sem, vmem_limit_bytes=VMEM_LIMIT)


def _rms(x, g):
    return x * lax.rsqrt(jnp.mean(x * x, axis=-1, keepdims=True) + EPS) * g


def _proj_kernel(x_ref, g_ref, w_ref, wkt_ref, wqvt_ref, cos_ref, sin_ref, cost_ref, sint_ref,
                 qa_ref, ka_ref, va_ref, qb_ref, kbt_ref, vb_ref, gb_ref, ka32_ref, va32_ref,
                 *, tiles_per_keep, feature_major_qv):
    h = _rms(x_ref[...], g_ref[...]).astype(BF16)
    nt_dims = (((1,), (1,)), ((), ()))

    def seg(lo, hi):
        return jnp.dot(h, w_ref[:, lo:hi], preferred_element_type=F32)

    ka = seg(W_A, 2 * W_A)
    ka_ref[...] = ka.astype(BF16)
    q_scale = HEAD_DIM_A ** -0.5
    if feature_major_qv:
        qvt = lax.dot_general(wqvt_ref[...], h, nt_dims, preferred_element_type=F32)
        for c in range(qa_ref.shape[0]):
            cs = slice(c * LANES, (c + 1) * LANES)
            qa_ref[c] = (qvt[:W_A, cs] * q_scale).astype(BF16)
            va_ref[c] = qvt[W_A:, cs].astype(BF16)
    else:
        qa_ref[...] = (seg(0, W_A) * q_scale).astype(BF16)
        va_ref[...] = seg(2 * W_A, 3 * W_A).astype(BF16)

    @pl.when(pl.program_id(0) % tiles_per_keep == tiles_per_keep - 1)
    def _():
        ka32_ref[...] = ka
        va32_ref[...] = seg(2 * W_A, 3 * W_A)

    qb = seg(3 * W_A, 3 * W_A + QK_B)
    cos = cos_ref[...]
    sin = sin_ref[...]
    for hd in range(N_HEADS_B):
        xh = qb[:, hd * DK_B:(hd + 1) * DK_B]
        qb_ref[:, hd * DK_B:(hd + 1) * DK_B] = (xh * cos + pltpu.roll(xh, DK_B // 2, axis=1) * sin).astype(BF16)

    vb_ref[...] = seg(3 * W_A + QK_B, 3 * W_A + QK_B + V_B).astype(BF16)
    gb_ref[...] = seg(3 * W_A + QK_B + V_B, 3 * W_A + QK_B + 2 * V_B)

    kt = lax.dot_general(wkt_ref[...], h, (((1,), (1,)), ((), ())), preferred_element_type=F32)
    cost = cost_ref[...]
    sint = sint_ref[...]
    half = DK_B // 2
    scale = DK_B ** -0.5
    for hd in range(N_HEADS_B):
        x1 = kt[hd * DK_B:hd * DK_B + half, :]
        x2 = kt[hd * DK_B + half:(hd + 1) * DK_B, :]
        kbt_ref[0, hd * DK_B:hd * DK_B + half, :] = (x1 * cost - x2 * sint) * scale
        kbt_ref[0, hd * DK_B + half:(hd + 1) * DK_B, :] = (x2 * cost + x1 * sint) * scale


def _rope_tables(pos):
    half = DK_B // 2
    freqs = ROPE_BASE ** (-jnp.arange(half, dtype=F32) / half)
    ang = pos.astype(F32)[:, None] * freqs[None, :]
    cos = jnp.cos(ang)
    sin = jnp.sin(ang)
    return (jnp.concatenate([cos, cos], axis=1), jnp.concatenate([-sin, sin], axis=1), cos.T, sin.T)


def _project(x2d, pos_rows, g_norm, w_main, w_kt, w_qvt, *, tm, tiles_per_keep, feature_major_qv):
    n = x2d.shape[0]
    period = pos_rows.shape[0]
    nt = n // tm
    ppt = period // tm
    cos2, sin2, cost, sint = _rope_tables(pos_rows)
    n_keep = n // tiles_per_keep
    row = lambda i: (i, 0)
    const = lambda i: (0, 0)
    if feature_major_qv:
        spt = tm // LANES
        qv_shape = jax.ShapeDtypeStruct((n // LANES, W_A, LANES), BF16)
        qv_spec = pl.BlockSpec((spt, W_A, LANES), lambda i: (i, 0, 0))
    else:
        qv_shape = jax.ShapeDtypeStruct((n, W_A), BF16)
        qv_spec = pl.BlockSpec((tm, W_A), row)
    outs = (
        qv_shape, jax.ShapeDtypeStruct((n, W_A), BF16), qv_shape, jax.ShapeDtypeStruct((n, QK_B), BF16),
        jax.ShapeDtypeStruct((nt, QK_B, tm), F32), jax.ShapeDtypeStruct((n, V_B), BF16),
        jax.ShapeDtypeStruct((n, V_B), F32),
        jax.ShapeDtypeStruct((n_keep, W_A), F32), jax.ShapeDtypeStruct((n_keep, W_A), F32),
    )
    keep_spec = pl.BlockSpec((tm, W_A), lambda i: (i // tiles_per_keep, 0))
    return pl.pallas_call(
        functools.partial(_proj_kernel, tiles_per_keep=tiles_per_keep, feature_major_qv=feature_major_qv),
        grid=(nt,),
        in_specs=[
            pl.BlockSpec((tm, D_MODEL), row),
            pl.BlockSpec((1, D_MODEL), const),
            pl.BlockSpec(w_main.shape, const),
            pl.BlockSpec(w_kt.shape, const),
            pl.BlockSpec(w_qvt.shape, const),
            pl.BlockSpec((tm, DK_B), lambda i: (i % ppt, 0)),
            pl.BlockSpec((tm, DK_B), lambda i: (i % ppt, 0)),
            pl.BlockSpec((DK_B // 2, tm), lambda i: (0, i % ppt)),
            pl.BlockSpec((DK_B // 2, tm), lambda i: (0, i % ppt)),
        ],
        out_specs=(
            qv_spec, pl.BlockSpec((tm, W_A), row), qv_spec,
            pl.BlockSpec((tm, QK_B), row), pl.BlockSpec((1, QK_B, tm), lambda i: (i, 0, 0)),
            pl.BlockSpec((tm, V_B), row), pl.BlockSpec((tm, V_B), row), keep_spec, keep_spec,
        ),
        out_shape=outs,
        compiler_params=_params(("arbitrary",)),
        name="proj",
    )(x2d, g_norm.reshape(1, D_MODEL), w_main, w_kt, w_qvt, cos2, sin2, cost, sint)


def _attend_pairs(q_of, k_of, v_of, bias_of, store):
    for hp in range(N_HEADS_A // 2):
        qp = q_of(hp)
        kw = k_of(hp)
        vw = v_of(hp)
        lane = lax.broadcasted_iota(jnp.int32, qp.shape, 1)
        outs = []
        for hh in range(2):
            in_head = (lane >= hh * HEAD_DIM_A) & (lane < (hh + 1) * HEAD_DIM_A)
            qh = jnp.where(in_head, qp, jnp.zeros_like(qp))
            s = lax.dot_general(qh, kw, (((1,), (1,)), ((), ())), preferred_element_type=F32)
            s = s + bias_of(2 * hp + hh)
            m = jnp.max(s, axis=-1, keepdims=True)
            p = jnp.exp(s - m)
            l = jnp.sum(p, axis=-1, keepdims=True)
            o = jnp.dot(p.astype(BF16), vw, preferred_element_type=F32)
            outs.append(o / l)
        lane_o = lax.broadcasted_iota(jnp.int32, outs[0].shape, 1)
        store(hp, jnp.where(lane_o < HEAD_DIM_A, outs[0], outs[1]).astype(BF16))


def _fold_rows(x, op, reduce_rows):
    r = x.shape[0]
    while r % 16 == 0:
        r //= 2
        x = op(x[:r], x[r:])
    parts = [x[a:a + 8] for a in range(0, r, 8)]
    while len(parts) > 1:
        parts = [op(parts[a], parts[a + 1]) if a + 1 < len(parts) else parts[a] for a in range(0, len(parts), 2)]
    return reduce_rows(parts[0], axis=0, keepdims=True)


def _attn_prompt_kernel(qt_ref, k_ref, vt_ref, bias_ref, o_ref):
    j = pl.program_id(1)
    first = jnp.maximum(j - BAND_CHUNKS // 2, 0)
    start = pl.multiple_of(first * ATT_QBLK, ATT_QBLK)
    n_slab = ATT_WIN // LANES
    n_pairs = N_HEADS_A // 2

    def scores(hp):
        rows = slice(hp * LANES, (hp + 1) * LANES)
        qt = qt_ref[0, rows, :]
        dim = lax.broadcasted_iota(jnp.int32, qt.shape, 0)
        zero = jnp.zeros_like(qt)
        w = jnp.concatenate([jnp.where(dim < HEAD_DIM_A, qt, zero), jnp.where(dim >= HEAD_DIM_A, qt, zero)], axis=1)
        kw = k_ref[0, pl.ds(start, ATT_WIN), rows]
        return jnp.dot(kw, w, preferred_element_type=F32)

    s_next = scores(0)
    for hp in range(n_pairs):
        rows = slice(hp * LANES, (hp + 1) * LANES)
        s = s_next + bias_ref[0, hp]
        if hp + 1 < n_pairs:
            s_next = scores(hp + 1)
        m = _fold_rows(s, jnp.maximum, jnp.max)
        p = jnp.exp(s - m)
        l = _fold_rows(p, jnp.add, jnp.sum)
        vt = jnp.concatenate([vt_ref[first + c, rows, :] for c in range(n_slab)], axis=1)
        ot = jnp.dot(vt, p.astype(BF16), preferred_element_type=F32) / l
        odim = lax.broadcasted_iota(jnp.int32, (LANES, LANES), 0)
        o_pair_t = jnp.where(odim < HEAD_DIM_A, ot[:, :LANES], ot[:, LANES:])
        o_ref[0, :, rows] = o_pair_t.T.astype(BF16)


def _band_bias(table):
    i = np.arange(ATT_QBLK)[None, :]
    jk = np.arange(ATT_WIN)[:, None]
    out = []
    for v in range(BAND_CHUNKS // 2 + 1):
        off_chunks = 2 * v if v < BAND_CHUNKS // 2 else BAND_CHUNKS
        dchunk = (off_chunks + i // CHUNK) - jk // CHUNK
        valid = (dchunk >= 0) & (dchunk <= BAND_CHUNKS)
        m = np.arange(ATT_WIN + ATT_QBLK - 1)
        idx = np.clip(off_chunks * CHUNK + (ATT_QBLK - 1) - m, -REL_CLIP, REL_CLIP) + REL_CLIP
        f = table[:, idx].astype(F32)
        b = jnp.stack([f[:, ATT_QBLK - 1 - r:ATT_QBLK - 1 - r + ATT_WIN] for r in range(ATT_QBLK)], axis=2)
        b = jnp.where(valid[None], b, jnp.float32(NEG_BIG))
        b = b.reshape(N_HEADS_A // 2, 2, ATT_WIN, ATT_QBLK).transpose(0, 2, 1, 3)
        out.append(b.reshape(N_HEADS_A // 2, ATT_WIN, 2 * ATT_QBLK))
    return jnp.stack(out)


def _attention_prompt(qat, ka, vat, table, b):
    s = ka.shape[1]
    nq = s // ATT_QBLK
    bias = _band_bias(table)
    nvar = bias.shape[0]
    return pl.pallas_call(
        _attn_prompt_kernel,
        grid=(b, nq),
        in_specs=[
            pl.BlockSpec((1, W_A, LANES), lambda bi, j: (bi * nq + j, 0, 0)),
            pl.BlockSpec((1, s, W_A), lambda bi, j: (bi, 0, 0)),
            pl.BlockSpec((nq, W_A, LANES), lambda bi, j: (bi, 0, 0)),
            pl.BlockSpec((1, N_HEADS_A // 2, ATT_WIN, 2 * ATT_QBLK),
                         lambda bi, j: (jnp.minimum(j, nvar - 1), 0, 0, 0)),
        ],
        out_specs=pl.BlockSpec((1, ATT_QBLK, W_A), lambda bi, j: (bi, j, 0)),
        out_shape=jax.ShapeDtypeStruct((b, s, W_A), BF16),
        compiler_params=_params(("arbitrary", "arbitrary")),
        name="attn_prompt",
    )(qat, ka, vat, bias)


def _attn_sample_kernel(q_ref, k_ref, v_ref, bias_ref, o_ref):
    def sl(hp):
        return slice(hp * LANES, (hp + 1) * LANES)

    def store(hp, val):
        o_ref[0, :, sl(hp)] = val

    _attend_pairs(
        lambda hp: q_ref[0, :, sl(hp)],
        lambda hp: k_ref[0, :, sl(hp)],
        lambda hp: v_ref[0, :, sl(hp)],
        lambda hd: bias_ref[hd],
        store)


def _attention_sample(qa, k_all, v_all, table, n_cache):
    b, n, _ = qa.shape
    nk = k_all.shape[1]
    dist = jnp.arange(n)[:, None] + n_cache - jnp.arange(nk)[None, :]
    bias = table[:, jnp.clip(dist, -REL_CLIP, REL_CLIP) + REL_CLIP].astype(F32)
    return pl.pallas_call(
        _attn_sample_kernel,
        grid=(b,),
        in_specs=[
            pl.BlockSpec((1, n, W_A), lambda bi: (bi, 0, 0)),
            pl.BlockSpec((1, nk, W_A), lambda bi: (bi, 0, 0)),
            pl.BlockSpec((1, nk, W_A), lambda bi: (bi, 0, 0)),
            pl.BlockSpec((N_HEADS_A, n, nk), lambda bi: (0, 0, 0)),
        ],
        out_specs=pl.BlockSpec((1, n, W_A), lambda bi: (bi, 0, 0)),
        out_shape=jax.ShapeDtypeStruct((b, n, W_A), BF16),
        compiler_params=_params(("arbitrary",)),
        name="attn_sample",
    )(qa, k_all, v_all, bias)


def _ret_kernel(gt_ref, q_ref, kt_ref, v_ref, gb_ref, s0_ref, dmask_ref, qd_ref, kd_ref, gn_ref,
                out_ref, state_ref):
    @pl.when(pl.program_id(1) == 0)
    def _():
        state_ref[...] = s0_ref[...]

    for hd in range(N_HEADS_B):
        qs = slice(hd * DK_B, (hd + 1) * DK_B)
        vs = slice(hd * DV_B, (hd + 1) * DV_B)
        q = q_ref[0, :, qs]
        kt = kt_ref[0, qs, :]
        v = v_ref[0, :, vs]
        state = state_ref[0, hd]
        scores = jnp.dot(q, kt.astype(BF16), preferred_element_type=F32) * dmask_ref[hd]
        o = jnp.dot(scores.astype(BF16), v, preferred_element_type=F32)
        o = o + jnp.dot(q, state.astype(BF16), preferred_element_type=F32) * qd_ref[hd]
        kd = (kt * kd_ref[hd]).astype(BF16)
        state_ref[0, hd] = state * gt_ref[hd] + jnp.dot(kd, v, preferred_element_type=F32)
        mu = jnp.mean(o, axis=-1, keepdims=True)
        var = jnp.mean(jnp.square(o - mu), axis=-1, keepdims=True)
        rb = (o - mu) * lax.rsqrt(var + EPS) * gn_ref[:, vs]
        gb = gb_ref[0, :, vs]
        out_ref[0, :, vs] = (gb * jax.nn.sigmoid(gb) * rb).astype(BF16)


def _retention(qb, kbt, vb, gb, state0, log_g, ret_norm_g, *, t, kt_index):
    b, s, _ = qb.shape
    nc = s // t
    idx = jnp.arange(t, dtype=F32)
    diff = idx[:, None] - idx[None, :]
    dmask = jnp.where(diff[None] >= 0, jnp.exp(log_g[:, None, None] * jnp.maximum(diff, 0.0)[None]), 0.0)
    q_decay = jnp.exp(log_g[:, None] * (idx[None, :] + 1.0))
    k_decay = jnp.exp(log_g[:, None] * (t - 1.0 - idx[None, :]))
    g_t = jnp.exp(log_g * t)
    qd = jnp.broadcast_to(q_decay[:, :, None], (N_HEADS_B, t, DV_B))
    kd = k_decay[:, None, :]
    const3 = lambda bi, c: (0, 0, 0)
    return pl.pallas_call(
        _ret_kernel,
        grid=(b, nc),
        in_specs=[
            pl.BlockSpec(memory_space=pltpu.SMEM),
            pl.BlockSpec((1, t, QK_B), lambda bi, c: (bi, c, 0)),
            pl.BlockSpec((1, QK_B, t), kt_index),
            pl.BlockSpec((1, t, V_B), lambda bi, c: (bi, c, 0)),
            pl.BlockSpec((1, t, V_B), lambda bi, c: (bi, c, 0)),
            pl.BlockSpec((1, N_HEADS_B, DK_B, DV_B), lambda bi, c: (bi, 0, 0, 0)),
            pl.BlockSpec((N_HEADS_B, t, t), const3),
            pl.BlockSpec((N_HEADS_B, t, DV_B), const3),
            pl.BlockSpec((N_HEADS_B, 1, t), const3),
            pl.BlockSpec((1, V_B), lambda bi, c: (0, 0)),
        ],
        out_specs=(
            pl.BlockSpec((1, t, V_B), lambda bi, c: (bi, c, 0)),
            pl.BlockSpec((1, N_HEADS_B, DK_B, DV_B), lambda bi, c: (bi, 0, 0, 0)),
        ),
        out_shape=(jax.ShapeDtypeStruct((b, s, V_B), BF16),
                   jax.ShapeDtypeStruct((b, N_HEADS_B, DK_B, DV_B), F32)),
        compiler_params=_params(("arbitrary", "arbitrary")),
        name="retention",
    )(g_t, qb, kbt, vb, gb, state0, dmask, qd, kd, ret_norm_g.reshape(1, V_B))


def _split_bf16(x):
    hi = x.astype(BF16)
    lo = (x - hi.astype(F32)).astype(BF16)
    return hi, lo


def _merge_kernel(x_ref, att_ref, bin_ref, gmix_ref, wga_ref, wgb_ref, wpa_ref, wpb_ref, wout_ref,
                  gffn_ref, wr_hi_ref, wr_lo_ref, br_ref, x1_ref, hx_ref, cls_ref):
    x = x_ref[...]
    h = _rms(x, gmix_ref[...]).astype(BF16)
    gate_a = jnp.dot(h, wga_ref[...], preferred_element_type=F32)
    gate_b = jnp.dot(h, wgb_ref[...], preferred_element_type=F32)
    a = jnp.dot(att_ref[...], wpa_ref[...], preferred_element_type=F32)
    b = jnp.dot(bin_ref[...], wpb_ref[...], preferred_element_type=F32)
    m = jax.nn.sigmoid(gate_a) * a + jax.nn.sigmoid(gate_b) * b
    x1 = x + jnp.dot(m.astype(BF16), wout_ref[...], preferred_element_type=F32)
    x1_ref[...] = x1
    h2 = _rms(x1, gffn_ref[...])
    hx_ref[:, :D_MODEL] = h2

    h_hi, h_lo = _split_bf16(h2)
    nt = (((1,), (1,)), ((), ()))
    lt = (lax.dot_general(wr_hi_ref[...], h_hi, nt, preferred_element_type=F32)
          + lax.dot_general(wr_hi_ref[...], h_lo, nt, preferred_element_type=F32)
          + lax.dot_general(wr_lo_ref[...], h_hi, nt, preferred_element_type=F32)) + br_ref[...]
    tm = lt.shape[1]
    row = lax.broadcasted_iota(jnp.int32, (8, tm), 0)
    lg = lt[0:8, :]
    mg = jnp.max(lg, axis=0, keepdims=True)
    grp = jnp.min(jnp.where(lg == mg, row, 8), axis=0, keepdims=True)
    p_grp = 1.0 / jnp.sum(jnp.exp(lg - mg), axis=0, keepdims=True)
    le = jnp.zeros((8, tm), F32)
    for g in range(N_GROUPS):
        le = jnp.where(grp == g, lt[8 + 8 * g:16 + 8 * g, :], le)
    m0 = jnp.max(le, axis=0, keepdims=True)
    i0 = jnp.min(jnp.where(le == m0, row, 8), axis=0, keepdims=True)
    rest = jnp.where(row == i0, jnp.float32(-jnp.inf), le)
    m1 = jnp.max(rest, axis=0, keepdims=True)
    i1 = jnp.min(jnp.where(rest == m1, row, 8), axis=0, keepdims=True)
    e = jnp.exp(m1 - m0)
    w0 = (1.0 / (1.0 + e)) * p_grp
    w1 = (e / (1.0 + e)) * p_grp
    ea = jnp.minimum(i0, i1)
    eb = jnp.maximum(i0, i1)
    pair = ((ea * (2 * EXPERTS_PER_GROUP - 1 - ea)) >> 1) + (eb - ea - 1)
    cls_ref[...] = jnp.where(row == 0, grp * PAIRS_PER_GROUP + pair, 0)
    wa = jnp.where(i0 < i1, w0, w1)
    wb = jnp.where(i0 < i1, w1, w0)
    wrow = lax.broadcasted_iota(jnp.int32, (LANES, tm), 0)
    wslab = jnp.where(wrow == 0, wa, jnp.where(wrow == 1, wb, 0.0))
    hx_ref[:, D_MODEL:] = wslab.T


def _merge(x2d, att, b_in, lw, *, tm):
    n = x2d.shape[0]
    row = lambda i: (i, 0)
    const = lambda i: (0, 0)
    full = lambda a: pl.BlockSpec(a.shape, const)
    return pl.pallas_call(
        _merge_kernel,
        grid=(n // tm,),
        in_specs=[
            pl.BlockSpec((tm, D_MODEL), row), pl.BlockSpec((tm, W_A), row), pl.BlockSpec((tm, V_B), row),
            full(lw["g_mix"]), full(lw["w_ga"]), full(lw["w_gb"]), full(lw["w_pa"]), full(lw["w_pb"]),
            full(lw["w_out"]), full(lw["g_ffn"]), full(lw["wr_hi"]), full(lw["wr_lo"]), full(lw["b_r"]),
        ],
        out_specs=(pl.BlockSpec((tm, D_MODEL), row), pl.BlockSpec((tm, HX_W), row),
                   pl.BlockSpec((8, tm), lambda i: (0, i))),
        out_shape=(jax.ShapeDtypeStruct((n, D_MODEL), F32), jax.ShapeDtypeStruct((n, HX_W), F32),
                   jax.ShapeDtypeStruct((8, n), jnp.int32)),
        compiler_params=_params(("arbitrary",)),
        name="merge",
    )(x2d, att, b_in, lw["g_mix"], lw["w_ga"], lw["w_gb"], lw["w_pa"], lw["w_pb"], lw["w_out"],
      lw["g_ffn"], lw["wr_hi"], lw["wr_lo"], lw["b_r"])


def _class_onehot(cls_row, base, n_real):
    t = cls_row.shape[1]
    crow = lax.broadcasted_iota(jnp.int32, (CLASS_ROWS, t), 0)
    tok = base + lax.broadcasted_iota(jnp.int32, (CLASS_ROWS, t), 1)
    return (cls_row == crow) & (tok < n_real)


def _rank_kernel(cls_ref, tri_ref, rank_ref, counts_ref, *, n_real):
    i = pl.program_id(0)

    @pl.when(i == 0)
    def _():
        counts_ref[...] = jnp.zeros_like(counts_ref)

    t = tri_ref.shape[0]
    hot = _class_onehot(cls_ref[0:1, :], i * t, n_real)
    incl = jnp.dot(jnp.where(hot, 1.0, 0.0).astype(BF16), tri_ref[...], preferred_element_type=F32)
    carry = counts_ref[:, 0:1]
    rank = jnp.sum(jnp.where(hot, incl + carry, 0.0), axis=0, keepdims=True) - 1.0
    rank_ref[...] = jnp.broadcast_to(rank, rank_ref.shape).astype(jnp.int32)
    counts_ref[...] = counts_ref[...] + incl[:, t - 1:t]


def _slot_kernel(cls_ref, rank_ref, pstart_ref, slot_ref, *, n_real):
    t = cls_ref.shape[1]
    hot = _class_onehot(cls_ref[0:1, :], pl.program_id(0) * t, n_real)
    start = jnp.sum(jnp.where(hot, pstart_ref[...], 0.0), axis=0, keepdims=True)
    slot_ref[...] = jnp.broadcast_to(start.astype(jnp.int32) + rank_ref[0:1, :], slot_ref.shape)


def _routing_plan(cls_all, n_real):
    n_pad = cls_all.shape[1]
    nblk = n_pad // PLAN_T
    tri = jnp.asarray(np.triu(np.ones((PLAN_T, PLAN_T), np.float32)), BF16)
    tok = lambda i: (0, i)
    const = lambda i: (0, 0)
    rank, counts = pl.pallas_call(
        functools.partial(_rank_kernel, n_real=n_real),
        grid=(nblk,),
        in_specs=[pl.BlockSpec((8, PLAN_T), tok), pl.BlockSpec((PLAN_T, PLAN_T), const)],
        out_specs=(pl.BlockSpec((8, PLAN_T), tok), pl.BlockSpec((CLASS_ROWS, LANES), const)),
        out_shape=(jax.ShapeDtypeStruct((8, n_pad), jnp.int32), jax.ShapeDtypeStruct((CLASS_ROWS, LANES), F32)),
        compiler_params=_params(("arbitrary",)),
        name="moe_rank",
    )(cls_all, tri)
    counts = counts[:, 0].astype(jnp.int32)
    psizes = ((counts + MOE_TM - 1) // MOE_TM) * MOE_TM
    pends = jnp.cumsum(psizes)
    pstart = (pends - psizes).astype(F32).reshape(CLASS_ROWS, 1)
    slot = pl.pallas_call(
        functools.partial(_slot_kernel, n_real=n_real),
        grid=(nblk,),
        in_specs=[pl.BlockSpec((8, PLAN_T), tok), pl.BlockSpec((8, PLAN_T), tok),
                  pl.BlockSpec((CLASS_ROWS, 1), const)],
        out_specs=pl.BlockSpec((8, PLAN_T), tok),
        out_shape=jax.ShapeDtypeStruct((8, n_pad), jnp.int32),
        compiler_params=_params(("arbitrary",)),
        name="moe_slot",
    )(cls_all, rank, pstart)
    return slot[0], pends


def _dispatch_kernel(slot_ref, hxp_ref, hxs_ref, xs_in_hbm, xs_hbm, buf, sem, *, nb_p, n_s):
    del xs_in_hbm
    i = pl.program_id(0)
    par = i % 2

    def row_copy(r, s):
        return pltpu.make_async_copy(buf.at[s, pl.ds(r, 1)], xs_hbm.at[pl.ds(slot_ref[0, 0, r], 1)], sem.at[s])

    def for_rows(n, fn):
        def body(r, c):
            fn(r)
            return c
        lax.fori_loop(0, n, body, 0, unroll=8)

    def wait_block(n, s):
        for_rows(n, lambda r: row_copy(0, s).wait())

    @pl.when(i >= 2)
    def _():
        wait_block(PLAN_T, par)

    @pl.when(i < nb_p)
    def _():
        buf[par] = hxp_ref[...]
        for r in range(PLAN_T):
            row_copy(r, par).start()

    @pl.when(i == nb_p)
    def _():
        buf[par, :n_s] = hxs_ref[...]
        for_rows(n_s, lambda r: row_copy(r, par).start())
        if nb_p >= 1:
            wait_block(PLAN_T, 1 - par)
        wait_block(n_s, par)


def _dispatch(slot, hx_p, hx_s, n_slots):
    n_p, n_s = hx_p.shape[0], hx_s.shape[0]
    nb_p = n_p // PLAN_T
    assert n_p % PLAN_T == 0 and 0 < n_s <= PLAN_T
    slot3 = slot.reshape(-1, 1, PLAN_T)
    assert slot3.shape[0] == nb_p + 1
    any_spec = pl.BlockSpec(memory_space=pl.ANY)
    return pl.pallas_call(
        functools.partial(_dispatch_kernel, nb_p=nb_p, n_s=n_s),
        grid=(nb_p + 1,),
        in_specs=[pl.BlockSpec((1, 1, PLAN_T), lambda i: (i, 0, 0), memory_space=pltpu.SMEM),
                  pl.BlockSpec((PLAN_T, HX_W), lambda i: (jnp.minimum(i, nb_p - 1), 0)),
                  pl.BlockSpec((n_s, HX_W), lambda i: (0, 0)), any_spec],
        out_specs=any_spec,
        out_shape=jax.ShapeDtypeStruct((n_slots, HX_W), F32),
        scratch_shapes=[pltpu.VMEM((2, PLAN_T, HX_W), F32), pltpu.SemaphoreType.DMA((2,))],
        input_output_aliases={3: 0},
        compiler_params=_params(("arbitrary",)),
        name="moe_dispatch",
    )(slot3, hx_p, hx_s, jnp.zeros((n_slots, HX_W), F32))


def _expert_kernel(ea_ref, eb_ref, nv_ref, xs_ref, wga_ref, wua_ref, wda_ref, wgb_ref, wub_ref, wdb_ref, ys_ref):
    del ea_ref, eb_ref

    def expert(x, wg_ref, wu_ref, wd_ref):
        g = jnp.dot(x, wg_ref[0], preferred_element_type=F32)
        u = jnp.dot(x, wu_ref[0], preferred_element_type=F32)
        act = (g * jax.nn.sigmoid(g) * u).astype(BF16)
        return jnp.dot(act, wd_ref[0], preferred_element_type=F32)

    @pl.when(pl.program_id(0) < nv_ref[0])
    def _():
        x = xs_ref[:, :D_MODEL].astype(BF16)
        wa = xs_ref[:, D_MODEL:D_MODEL + 1]
        wb = xs_ref[:, D_MODEL + 1:D_MODEL + 2]
        ys_ref[...] = expert(x, wga_ref, wua_ref, wda_ref) * wa + expert(x, wgb_ref, wub_ref, wdb_ref) * wb

    @pl.when(pl.program_id(0) >= nv_ref[0])
    def _():
        ys_ref[...] = jnp.zeros_like(ys_ref)


def _class_experts():
    ea, eb = [], []
    for g in range(N_GROUPS):
        for a in range(EXPERTS_PER_GROUP):
            for b in range(a + 1, EXPERTS_PER_GROUP):
                ea.append(g * EXPERTS_PER_GROUP + a)
                eb.append(g * EXPERTS_PER_GROUP + b)
    return np.asarray(ea, np.int32), np.asarray(eb, np.int32)


def _experts(xs, pends, w_gate, w_up, w_down):
    n_tiles = xs.shape[0] // MOE_TM
    nv = (pends[N_CLASSES - 1] // MOE_TM).astype(jnp.int32)
    tile_cls = jnp.sum(pends[None, :N_CLASSES] <= (jnp.arange(n_tiles) * MOE_TM)[:, None], axis=1)
    tile_cls = jnp.minimum(tile_cls, N_CLASSES - 1)
    cls_ea, cls_eb = _class_experts()
    tile_ea = jnp.asarray(cls_ea)[tile_cls]
    tile_eb = jnp.asarray(cls_eb)[tile_cls]
    wa_spec = lambda shp: pl.BlockSpec((1,) + shp, lambda i, ea, eb, nv_: (ea[i], 0, 0))
    wb_spec = lambda shp: pl.BlockSpec((1,) + shp, lambda i, ea, eb, nv_: (eb[i], 0, 0))
    up, down = (D_MODEL, D_EXPERT), (D_EXPERT, D_MODEL)
    grid_spec = pltpu.PrefetchScalarGridSpec(
        num_scalar_prefetch=3,
        grid=(n_tiles,),
        in_specs=[pl.BlockSpec((MOE_TM, HX_W), lambda i, ea, eb, nv_: (i, 0)),
                  wa_spec(up), wa_spec(up), wa_spec(down), wb_spec(up), wb_spec(up), wb_spec(down)],
        out_specs=pl.BlockSpec((MOE_TM, D_MODEL), lambda i, ea, eb, nv_: (i, 0)),
    )
    return pl.pallas_call(
        _expert_kernel,
        grid_spec=grid_spec,
        out_shape=jax.ShapeDtypeStruct((n_tiles * MOE_TM, D_MODEL), F32),
        compiler_params=_params(("arbitrary",)),
        name="moe_experts",
    )(tile_ea, tile_eb, nv.reshape(1), xs, w_gate, w_up, w_down, w_gate, w_up, w_down)


def _ple_kernel(slot_ref, slotn_ref, x1_ref, p_ref, wproj_ref, wgate_ref, gple_ref, gfin_ref, ys_hbm,
                y_ref, ybuf, sem, *, tm):
    i = pl.program_id(0)
    par = i % 2

    def row_copy(idx_ref, r, s):
        return pltpu.make_async_copy(ys_hbm.at[pl.ds(idx_ref[0, 0, r], 1)], ybuf.at[s, pl.ds(r, 1)], sem.at[s])

    def for_rows(fn):
        def body(r, c):
            fn(r)
            return c
        lax.fori_loop(0, tm, body, 0, unroll=8)

    @pl.when(i == 0)
    def _():
        for_rows(lambda r: row_copy(slot_ref, r, 0).start())

    for_rows(lambda r: row_copy(slot_ref, 0, par).wait())

    for r in range(tm):
        row_copy(slotn_ref, r, 1 - par).start()

    x2 = x1_ref[...] + ybuf[par]
    proj = jnp.dot(p_ref[...].astype(BF16), wproj_ref[...], preferred_element_type=F32)
    gate = jnp.dot(_rms(x2, gple_ref[...]).astype(BF16), wgate_ref[...], preferred_element_type=F32)
    x3 = x2 + proj * jax.nn.sigmoid(gate)
    y_ref[...] = _rms(x3, gfin_ref[...])

    @pl.when(i == pl.num_programs(0) - 1)
    def _():
        for_rows(lambda r: row_copy(slot_ref, 0, 1 - par).wait())


def _ple(x1, ys, slot, p2d, lw, g_final, *, tm):
    n = x1.shape[0]
    nt = n // tm
    slot3 = slot.reshape(nt, 1, tm)
    row = lambda i: (i, 0)
    const = lambda i: (0, 0)
    idx_spec = lambda f: pl.BlockSpec((1, 1, tm), f, memory_space=pltpu.SMEM)
    return pl.pallas_call(
        functools.partial(_ple_kernel, tm=tm),
        grid=(nt,),
        in_specs=[
            idx_spec(lambda i: (i, 0, 0)), idx_spec(lambda i: (jnp.minimum(i + 1, nt - 1), 0, 0)),
            pl.BlockSpec((tm, D_MODEL), row), pl.BlockSpec((tm, D_PLE), row),
            pl.BlockSpec((D_PLE, D_MODEL), const), pl.BlockSpec((D_MODEL, D_MODEL), const),
            pl.BlockSpec((1, D_MODEL), const), pl.BlockSpec((1, D_MODEL), const),
            pl.BlockSpec(memory_space=pl.ANY),
        ],
        out_specs=pl.BlockSpec((tm, D_MODEL), row),
        out_shape=jax.ShapeDtypeStruct((n, D_MODEL), F32),
        scratch_shapes=[pltpu.VMEM((2, tm, D_MODEL), F32), pltpu.SemaphoreType.DMA((2,))],
        compiler_params=_params(("arbitrary",)),
        name="ple",
    )(slot3, slot3, x1, p2d, lw["w_ple_proj"], lw["w_ple_gate"], lw["g_ple"], g_final.reshape(1, D_MODEL), ys)


def _layer_weights(i, norm_mix_g, w_in, ret_norm_g, w_proj_a, w_proj_b, w_out, norm_ffn_g,
                   w_router_group, b_router_group, w_router_expert, b_router_expert,
                   w_gate_e, w_up_e, w_down_e, norm_ple_g, w_ple_gate, w_ple_proj):
    w = w_in[i]
    o = _IN_OFFS
    cols = lambda k: w[:, o[k]:o[k + 1]]
    w_main = jnp.concatenate([cols(0), cols(1), cols(2), cols(3), cols(5), cols(6)], axis=1).astype(BF16)
    wr = jnp.zeros((ROUTER_ROWS, D_MODEL), F32)
    wr = wr.at[:N_GROUPS].set(w_router_group[i].T).at[8:].set(w_router_expert[i].T)
    br = jnp.full((ROUTER_ROWS,), NEG_BIG, F32)
    br = br.at[:N_GROUPS].set(b_router_group[i].astype(F32)).at[8:].set(b_router_expert[i].astype(F32))
    wr_hi, wr_lo = _split_bf16(wr)
    return {
        "g_mix": norm_mix_g[i].reshape(1, D_MODEL), "w_main": w_main, "w_kt": cols(4).T.astype(BF16),
        "w_qvt": jnp.concatenate([cols(0).T, cols(2).T], axis=0).astype(BF16),
        "w_ga": cols(7).astype(BF16), "w_gb": cols(8).astype(BF16),
        "ret_norm_g": ret_norm_g[i], "w_pa": w_proj_a[i].astype(BF16), "w_pb": w_proj_b[i].astype(BF16),
        "w_out": w_out[i].astype(BF16), "g_ffn": norm_ffn_g[i].reshape(1, D_MODEL),
        "wr_hi": wr_hi, "wr_lo": wr_lo, "b_r": br.reshape(ROUTER_ROWS, 1),
        "w_gate": w_gate_e[i].astype(BF16), "w_up": w_up_e[i].astype(BF16), "w_down": w_down_e[i].astype(BF16),
        "g_ple": norm_ple_g[i].reshape(1, D_MODEL), "w_ple_gate": w_ple_gate[i].astype(BF16),
        "w_ple_proj": w_ple_proj[i].astype(BF16),
    }


def _moe(hx_p, cls_p, hx_s, cls_s, lw):
    n_p, n_s = hx_p.shape[0], hx_s.shape[0]
    n_real = n_p + n_s
    n_pad = (n_p // PLAN_T + 1) * PLAN_T
    cls_all = jnp.concatenate([cls_p, cls_s, jnp.zeros((8, n_pad - n_real), jnp.int32)], axis=1)
    slot, pends = _routing_plan(cls_all, n_real)
    n_slots = (pl.cdiv(n_real, MOE_TM) + N_CLASSES) * MOE_TM
    xs = _dispatch(slot, hx_p, hx_s, n_slots)
    ys = _experts(xs, pends, lw["w_gate"], lw["w_up"], lw["w_down"])
    return ys, slot[:n_p], slot[n_p:n_real]


def kernel(x_prompt, x_sample, cache_k_a, cache_v_a, state_ret, p_prompt, p_sample, norm_mix_g, w_in, rel_bias, ret_norm_g, w_proj_a, w_proj_b, w_out, norm_ffn_g, w_router_group, b_router_group, w_router_expert, b_router_expert, w_gate_e, w_up_e, w_down_e, norm_ple_g, w_ple_gate, w_ple_proj, final_norm_g):
    depth = w_in.shape[0]
    assert depth == 1, "the final norm is fused into the last layer; deeper stacks are not supported"
    bp, sp, _ = x_prompt.shape
    bs, ss, _ = x_sample.shape
    keep = min(WINDOW_A, sp)
    n_cache = cache_k_a.shape[2]
    log_g = jnp.log(1.0 - 2.0 ** (-5.0 - jnp.arange(N_HEADS_B, dtype=F32)))
    i = 0
    lw = _layer_weights(i, norm_mix_g, w_in, ret_norm_g, w_proj_a, w_proj_b, w_out, norm_ffn_g,
                        w_router_group, b_router_group, w_router_expert, b_router_expert,
                        w_gate_e, w_up_e, w_down_e, norm_ple_g, w_ple_gate, w_ple_proj)

    tm = 512
    assert sp % tm == 0 and keep == tm and sp >= ATT_WIN
    t_ret = 128
    xp2 = x_prompt.reshape(bp * sp, D_MODEL)
    qa, ka, va, qb, kbt, vb, gb, ka32, va32 = _project(
        xp2, jnp.arange(sp), lw["g_mix"], lw["w_main"], lw["w_kt"], lw["w_qvt"], tm=tm,
        tiles_per_keep=sp // tm, feature_major_qv=True)
    r3 = lambda a: a.reshape(bp, sp, a.shape[-1])
    att = _attention_prompt(qa, r3(ka), va, rel_bias[i], bp)
    per_tile = tm // t_ret
    tiles_per_b = sp // tm
    b_in, s_prompt = _retention(
        r3(qb), kbt, r3(vb), r3(gb), jnp.zeros((bp, N_HEADS_B, DK_B, DV_B), F32), log_g, lw["ret_norm_g"],
        t=t_ret, kt_index=lambda bi, c: (bi * tiles_per_b + c // per_tile, 0, c % per_tile))
    x1_p, hx_p, cls_p = _merge(xp2, att.reshape(bp * sp, W_A), b_in.reshape(bp * sp, V_B), lw, tm=tm)
    k_a_prompt = ka32.reshape(bp, keep, N_HEADS_A, HEAD_DIM_A)
    v_a_prompt = va32.reshape(bp, keep, N_HEADS_A, HEAD_DIM_A)

    ns = bs * ss
    xs2 = x_sample.reshape(ns, D_MODEL)
    pos_s = jnp.tile(PAST_LEN + jnp.arange(ss), bs)
    qa, ka, va, qb, kbt, vb, gb, ka32, va32 = _project(
        xs2, pos_s, lw["g_mix"], lw["w_main"], lw["w_kt"], lw["w_qvt"], tm=ns, tiles_per_keep=1,
        feature_major_qv=False)
    r3 = lambda a: a.reshape(bs, ss, a.shape[-1])
    k_all = jnp.concatenate([cache_k_a[i].reshape(bs, n_cache, W_A).astype(BF16), r3(ka)], axis=1)
    v_all = jnp.concatenate([cache_v_a[i].reshape(bs, n_cache, W_A).astype(BF16), r3(va)], axis=1)
    att = _attention_sample(r3(qa), k_all, v_all, rel_bias[i], n_cache)
    kbt_s = kbt.reshape(QK_B, bs, ss).transpose(1, 0, 2)
    b_in, s_sample = _retention(
        r3(qb), kbt_s, r3(vb), r3(gb), state_ret[i].astype(F32), log_g, lw["ret_norm_g"],
        t=ss, kt_index=lambda bi, c: (bi, 0, 0))
    x1_s, hx_s, cls_s = _merge(xs2, att.reshape(ns, W_A), b_in.reshape(ns, V_B), lw, tm=ns)

    ys, slot_p, slot_s = _moe(hx_p, cls_p, hx_s, cls_s, lw)
    y_prompt = _ple(x1_p, ys, slot_p, p_prompt[i].reshape(bp * sp, D_PLE), lw, final_norm_g, tm=tm)
    y_sample = _ple(x1_s, ys, slot_s, p_sample[i].reshape(ns, D_PLE), lw, final_norm_g, tm=ns)
    k_a_sample = ka32.reshape(bs, ss, N_HEADS_A, HEAD_DIM_A)
    v_a_sample = va32.reshape(bs, ss, N_HEADS_A, HEAD_DIM_A)

    return (y_prompt.reshape(bp, sp, D_MODEL), y_sample.reshape(bs, ss, D_MODEL),
            k_a_prompt[None], v_a_prompt[None], s_prompt[None],
            k_a_sample[None], v_a_sample[None], s_sample.astype(state_ret.dtype)[None])
```

```python
import functools

import numpy as np
import jax
import jax.numpy as jnp
from jax import lax
from jax.experimental import pallas as pl
from jax.experimental.pallas import tpu as pltpu

F32 = jnp.float32
BF16 = jnp.bfloat16

D_MODEL = 1024
PAST_LEN = 1024
CHUNK = 64
BAND_CHUNKS = 8
WINDOW_A = BAND_CHUNKS * CHUNK
N_HEADS_A = 8
HEAD_DIM_A = 64
W_A = N_HEADS_A * HEAD_DIM_A
REL_CLIP = 128
N_HEADS_B = 4
DK_B = 128
DV_B = 256
QK_B = N_HEADS_B * DK_B
V_B = N_HEADS_B * DV_B
ROPE_BASE = 10000.0
N_GROUPS = 4
EXPERTS_PER_GROUP = 8
N_EXPERTS = N_GROUPS * EXPERTS_PER_GROUP
TOP_K = 2
D_EXPERT = 512
D_PLE = 256
EPS = 1e-6
_IN_SIZES = (W_A, W_A, W_A, QK_B, QK_B, V_B, V_B, D_MODEL, D_MODEL)
_IN_OFFS = tuple(sum(_IN_SIZES[:i]) for i in range(len(_IN_SIZES) + 1))

LANES = 128
ATT_QBLK = 2 * CHUNK
ATT_WIN = (BAND_CHUNKS + 2) * CHUNK
ROUTER_ROWS = 8 + N_EXPERTS
NEG_BIG = -1e30
PAIRS_PER_GROUP = EXPERTS_PER_GROUP * (EXPERTS_PER_GROUP - 1) // 2
N_CLASSES = N_GROUPS * PAIRS_PER_GROUP
CLASS_ROWS = 128
HX_W = D_MODEL + LANES
PLAN_T = 512
MOE_TM = 256
VMEM_LIMIT = 56 * 1024 * 1024


def _params(sem):
    return pltpu.CompilerParams(dimension_semantics=sem, vmem_limit_bytes=VMEM_LIMIT)


def _rms(x, g):
    return x * lax.rsqrt(jnp.mean(x * x, axis=-1, keepdims=True) + EPS) * g


def _proj_kernel(x_ref, g_ref, w_ref, wkt_ref, wqvt_ref, cos_ref, sin_ref, cost_ref, sint_ref,
                 qa_ref, ka_ref, va_ref, qb_ref, kbt_ref, vb_ref, gb_ref, ka32_ref, va32_ref,
                 *, tiles_per_keep, feature_major_qv):
    h = _rms(x_ref[...], g_ref[...]).astype(BF16)
    nt_dims = (((1,), (1,)), ((), ()))

    def seg(lo, hi):
        return jnp.dot(h, w_ref[:, lo:hi], preferred_element_type=F32)

    ka = seg(W_A, 2 * W_A)
    ka_ref[...] = ka.astype(BF16)
    q_scale = HEAD_DIM_A ** -0.5
    if feature_major_qv:
        qvt = lax.dot_general(wqvt_ref[...], h, nt_dims, preferred_element_type=F32)
        for c in range(qa_ref.shape[0]):
            cs = slice(c * LANES, (c + 1) * LANES)
            qa_ref[c] = (qvt[:W_A, cs] * q_scale).astype(BF16)
            va_ref[c] = qvt[W_A:, cs].astype(BF16)
    else:
        qa_ref[...] = (seg(0, W_A) * q_scale).astype(BF16)
        va_ref[...] = seg(2 * W_A, 3 * W_A).astype(BF16)

    @pl.when(pl.program_id(0) % tiles_per_keep == tiles_per_keep - 1)
    def _():
        ka32_ref[...] = ka
        va32_ref[...] = seg(2 * W_A, 3 * W_A)

    qb = seg(3 * W_A, 3 * W_A + QK_B)
    cos = cos_ref[...]
    sin = sin_ref[...]
    for hd in range(N_HEADS_B):
        xh = qb[:, hd * DK_B:(hd + 1) * DK_B]
        qb_ref[:, hd * DK_B:(hd + 1) * DK_B] = (xh * cos + pltpu.roll(xh, DK_B // 2, axis=1) * sin).astype(BF16)

    vb_ref[...] = seg(3 * W_A + QK_B, 3 * W_A + QK_B + V_B).astype(BF16)
    gb_ref[...] = seg(3 * W_A + QK_B + V_B, 3 * W_A + QK_B + 2 * V_B)

    kt = lax.dot_general(wkt_ref[...], h, (((1,), (1,)), ((), ())), preferred_element_type=F32)
    cost = cost_ref[...]
    sint = sint_ref[...]
    half = DK_B // 2
    scale = DK_B ** -0.5
    for hd in range(N_HEADS_B):
        x1 = kt[hd * DK_B:hd * DK_B + half, :]
        x2 = kt[hd * DK_B + half:(hd + 1) * DK_B, :]
        kbt_ref[0, hd * DK_B:hd * DK_B + half, :] = (x1 * cost - x2 * sint) * scale
        kbt_ref[0, hd * DK_B + half:(hd + 1) * DK_B, :] = (x2 * cost + x1 * sint) * scale


def _rope_tables(pos):
    half = DK_B // 2
    freqs = ROPE_BASE ** (-jnp.arange(half, dtype=F32) / half)
    ang = pos.astype(F32)[:, None] * freqs[None, :]
    cos = jnp.cos(ang)
    sin = jnp.sin(ang)
    return (jnp.concatenate([cos, cos], axis=1), jnp.concatenate([-sin, sin], axis=1), cos.T, sin.T)


def _project(x2d, pos_rows, g_norm, w_main, w_kt, w_qvt, *, tm, tiles_per_keep, feature_major_qv):
    n = x2d.shape[0]
    period = pos_rows.shape[0]
    nt = n // tm
    ppt = period // tm
    cos2, sin2, cost, sint = _rope_tables(pos_rows)
    n_keep = n // tiles_per_keep
    row = lambda i: (i, 0)
    const = lambda i: (0, 0)
    if feature_major_qv:
        spt = tm // LANES
        qv_shape = jax.ShapeDtypeStruct((n // LANES, W_A, LANES), BF16)
        qv_spec = pl.BlockSpec((spt, W_A, LANES), lambda i: (i, 0, 0))
    else:
        qv_shape = jax.ShapeDtypeStruct((n, W_A), BF16)
        qv_spec = pl.BlockSpec((tm, W_A), row)
    outs = (
        qv_shape, jax.ShapeDtypeStruct((n, W_A), BF16), qv_shape, jax.ShapeDtypeStruct((n, QK_B), BF16),
        jax.ShapeDtypeStruct((nt, QK_B, tm), F32), jax.ShapeDtypeStruct((n, V_B), BF16),
        jax.ShapeDtypeStruct((n, V_B), F32),
        jax.ShapeDtypeStruct((n_keep, W_A), F32), jax.ShapeDtypeStruct((n_keep, W_A), F32),
    )
    keep_spec = pl.BlockSpec((tm, W_A), lambda i: (i // tiles_per_keep, 0))
    return pl.pallas_call(
        functools.partial(_proj_kernel, tiles_per_keep=tiles_per_keep, feature_major_qv=feature_major_qv),
        grid=(nt,),
        in_specs=[
            pl.BlockSpec((tm, D_MODEL), row),
            pl.BlockSpec((1, D_MODEL), const),
            pl.BlockSpec(w_main.shape, const),
            pl.BlockSpec(w_kt.shape, const),
            pl.BlockSpec(w_qvt.shape, const),
            pl.BlockSpec((tm, DK_B), lambda i: (i % ppt, 0)),
            pl.BlockSpec((tm, DK_B), lambda i: (i % ppt, 0)),
            pl.BlockSpec((DK_B // 2, tm), lambda i: (0, i % ppt)),
            pl.BlockSpec((DK_B // 2, tm), lambda i: (0, i % ppt)),
        ],
        out_specs=(
            qv_spec, pl.BlockSpec((tm, W_A), row), qv_spec,
            pl.BlockSpec((tm, QK_B), row), pl.BlockSpec((1, QK_B, tm), lambda i: (i, 0, 0)),
            pl.BlockSpec((tm, V_B), row), pl.BlockSpec((tm, V_B), row), keep_spec, keep_spec,
        ),
        out_shape=outs,
        compiler_params=_params(("arbitrary",)),
        name="proj",
    )(x2d, g_norm.reshape(1, D_MODEL), w_main, w_kt, w_qvt, cos2, sin2, cost, sint)


def _attend_pairs(q_of, k_of, v_of, bias_of, store):
    for hp in range(N_HEADS_A // 2):
        qp = q_of(hp)
        kw = k_of(hp)
        vw = v_of(hp)
        lane = lax.broadcasted_iota(jnp.int32, qp.shape, 1)
        outs = []
        for hh in range(2):
            in_head = (lane >= hh * HEAD_DIM_A) & (lane < (hh + 1) * HEAD_DIM_A)
            qh = jnp.where(in_head, qp, jnp.zeros_like(qp))
            s = lax.dot_general(qh, kw, (((1,), (1,)), ((), ())), preferred_element_type=F32)
            s = s + bias_of(2 * hp + hh)
            m = jnp.max(s, axis=-1, keepdims=True)
            p = jnp.exp(s - m)
            l = jnp.sum(p, axis=-1, keepdims=True)
            o = jnp.dot(p.astype(BF16), vw, preferred_element_type=F32)
            outs.append(o / l)
        lane_o = lax.broadcasted_iota(jnp.int32, outs[0].shape, 1)
        store(hp, jnp.where(lane_o < HEAD_DIM_A, outs[0], outs[1]).astype(BF16))


def _fold_rows(x, op, reduce_rows):
    r = x.shape[0]
    while r % 16 == 0:
        r //= 2
        x = op(x[:r], x[r:])
    parts = [x[a:a + 8] for a in range(0, r, 8)]
    while len(parts) > 1:
        parts = [op(parts[a], parts[a + 1]) if a + 1 < len(parts) else parts[a] for a in range(0, len(parts), 2)]
    return reduce_rows(parts[0], axis=0, keepdims=True)


def _attn_prompt_kernel(qt_ref, k_ref, vt_ref, bias_ref, o_ref):
    j = pl.program_id(1)
    first = jnp.maximum(j - BAND_CHUNKS // 2, 0)
    start = pl.multiple_of(first * ATT_QBLK, ATT_QBLK)
    n_slab = ATT_WIN // LANES
    n_pairs = N_HEADS_A // 2

    def scores(hp):
        rows = slice(hp * LANES, (hp + 1) * LANES)
        qt = qt_ref[0, rows, :]
        dim = lax.broadcasted_iota(jnp.int32, qt.shape, 0)
        zero = jnp.zeros_like(qt)
        w = jnp.concatenate([jnp.where(dim < HEAD_DIM_A, qt, zero), jnp.where(dim >= HEAD_DIM_A, qt, zero)], axis=1)
        kw = k_ref[0, pl.ds(start, ATT_WIN), rows]
        return jnp.dot(kw, w, preferred_element_type=F32)

    s_next = scores(0)
    for hp in range(n_pairs):
        rows = slice(hp * LANES, (hp + 1) * LANES)
        s = s_next + bias_ref[0, hp]
        if hp + 1 < n_pairs:
            s_next = scores(hp + 1)
        m = _fold_rows(s, jnp.maximum, jnp.max)
        p = jnp.exp(s - m)
        l = _fold_rows(p, jnp.add, jnp.sum)
        vt = jnp.concatenate([vt_ref[first + c, rows, :] for c in range(n_slab)], axis=1)
        ot = jnp.dot(vt, p.astype(BF16), preferred_element_type=F32) / l
        odim = lax.broadcasted_iota(jnp.int32, (LANES, LANES), 0)
        o_pair_t = jnp.where(odim < HEAD_DIM_A, ot[:, :LANES], ot[:, LANES:])
        o_ref[0, :, rows] = o_pair_t.T.astype(BF16)


def _band_bias(table):
    i = np.arange(ATT_QBLK)[None, :]
    jk = np.arange(ATT_WIN)[:, None]
    out = []
    for v in range(BAND_CHUNKS // 2 + 1):
        off_chunks = 2 * v if v < BAND_CHUNKS // 2 else BAND_CHUNKS
        dchunk = (off_chunks + i // CHUNK) - jk // CHUNK
        valid = (dchunk >= 0) & (dchunk <= BAND_CHUNKS)
        n_f = ATT_WIN + ATT_QBLK - 1
        idx = np.clip(off_chunks * CHUNK + (ATT_QBLK - 1) - np.arange(n_f + 1), -REL_CLIP, REL_CLIP) + REL_CLIP
        g = table[:, idx].astype(F32)
        rows = jnp.tile(g, (1, ATT_QBLK))[:, :ATT_QBLK * n_f].reshape(N_HEADS_A, ATT_QBLK, n_f)
        b = rows[:, :, ATT_QBLK - 1:].transpose(0, 2, 1)
        b = jnp.where(valid[None], b, jnp.float32(NEG_BIG))
        b = b.reshape(N_HEADS_A // 2, 2, ATT_WIN, ATT_QBLK).transpose(0, 2, 1, 3)
        out.append(b.reshape(N_HEADS_A // 2, ATT_WIN, 2 * ATT_QBLK))
    return jnp.stack(out)


def _attention_prompt(qat, ka, vat, table, b):
    s = ka.shape[1]
    nq = s // ATT_QBLK
    bias = _band_bias(table)
    nvar = bias.shape[0]
    return pl.pallas_call(
        _attn_prompt_kernel,
        grid=(b, nq),
        in_specs=[
            pl.BlockSpec((1, W_A, LANES), lambda bi, j: (bi * nq + j, 0, 0)),
            pl.BlockSpec((1, s, W_A), lambda bi, j: (bi, 0, 0)),
            pl.BlockSpec((nq, W_A, LANES), lambda bi, j: (bi, 0, 0)),
            pl.BlockSpec((1, N_HEADS_A // 2, ATT_WIN, 2 * ATT_QBLK),
                         lambda bi, j: (jnp.minimum(j, nvar - 1), 0, 0, 0)),
        ],
        out_specs=pl.BlockSpec((1, ATT_QBLK, W_A), lambda bi, j: (bi, j, 0)),
        out_shape=jax.ShapeDtypeStruct((b, s, W_A), BF16),
        compiler_params=_params(("arbitrary", "arbitrary")),
        name="attn_prompt",
    )(qat, ka, vat, bias)


def _attn_sample_kernel(q_ref, k_ref, v_ref, bias_ref, o_ref):
    def sl(hp):
        return slice(hp * LANES, (hp + 1) * LANES)

    def store(hp, val):
        o_ref[0, :, sl(hp)] = val

    _attend_pairs(
        lambda hp: q_ref[0, :, sl(hp)],
        lambda hp: k_ref[0, :, sl(hp)],
        lambda hp: v_ref[0, :, sl(hp)],
        lambda hd: bias_ref[hd],
        store)


def _attention_sample(qa, k_all, v_all, table, n_cache):
    b, n, _ = qa.shape
    nk = k_all.shape[1]
    dist = jnp.arange(n)[:, None] + n_cache - jnp.arange(nk)[None, :]
    bias = table[:, jnp.clip(dist, -REL_CLIP, REL_CLIP) + REL_CLIP].astype(F32)
    return pl.pallas_call(
        _attn_sample_kernel,
        grid=(b,),
        in_specs=[
            pl.BlockSpec((1, n, W_A), lambda bi: (bi, 0, 0)),
            pl.BlockSpec((1, nk, W_A), lambda bi: (bi, 0, 0)),
            pl.BlockSpec((1, nk, W_A), lambda bi: (bi, 0, 0)),
            pl.BlockSpec((N_HEADS_A, n, nk), lambda bi: (0, 0, 0)),
        ],
        out_specs=pl.BlockSpec((1, n, W_A), lambda bi: (bi, 0, 0)),
        out_shape=jax.ShapeDtypeStruct((b, n, W_A), BF16),
        compiler_params=_params(("arbitrary",)),
        name="attn_sample",
    )(qa, k_all, v_all, bias)


def _ret_kernel(gt_ref, q_ref, *refs, nb):
    kt_refs = refs[:nb]
    v_ref, gb_ref, s0_ref, dmask_ref, qd_ref, kd_ref, gn_ref, out_ref, state_ref = refs[nb:]

    @pl.when(pl.program_id(1) == 0)
    def _():
        state_ref[...] = s0_ref[...]

    for bb in range(nb):
        for hd in range(N_HEADS_B):
            qs = slice(hd * DK_B, (hd + 1) * DK_B)
            vs = slice(hd * DV_B, (hd + 1) * DV_B)
            q = q_ref[bb, :, qs]
            kt = kt_refs[bb][0, qs, :]
            v = v_ref[bb, :, vs]
            state = state_ref[bb, hd]
            scores = jnp.dot(q, kt.astype(BF16), preferred_element_type=F32) * dmask_ref[hd]
            o = jnp.dot(scores.astype(BF16), v, preferred_element_type=F32)
            o = o + jnp.dot(q, state.astype(BF16), preferred_element_type=F32) * qd_ref[hd]
            kd = (kt * kd_ref[hd]).astype(BF16)
            state_ref[bb, hd] = state * gt_ref[hd] + jnp.dot(kd, v, preferred_element_type=F32)
            mu = jnp.mean(o, axis=-1, keepdims=True)
            var = jnp.mean(jnp.square(o - mu), axis=-1, keepdims=True)
            rb = (o - mu) * lax.rsqrt(var + EPS) * gn_ref[:, vs]
            gb = gb_ref[bb, :, vs]
            out_ref[bb, :, vs] = (gb * jax.nn.sigmoid(gb) * rb).astype(BF16)


def _retention(qb, kbt, vb, gb, state0, log_g, ret_norm_g, *, t, kt_index, nb=4):
    b, s, _ = qb.shape
    assert b % nb == 0
    nc = s // t
    idx = jnp.arange(t, dtype=F32)
    diff = idx[:, None] - idx[None, :]
    dmask = jnp.where(diff[None] >= 0, jnp.exp(log_g[:, None, None] * jnp.maximum(diff, 0.0)[None]), 0.0)
    q_decay = jnp.exp(log_g[:, None] * (idx[None, :] + 1.0))
    k_decay = jnp.exp(log_g[:, None] * (t - 1.0 - idx[None, :]))
    g_t = jnp.exp(log_g * t)
    qd = jnp.broadcast_to(q_decay[:, :, None], (N_HEADS_B, t, DV_B))
    kd = k_decay[:, None, :]
    const3 = lambda bi, c: (0, 0, 0)
    seq = lambda w: pl.BlockSpec((nb, t, w), lambda bi, c: (bi, c, 0))
    state_spec = pl.BlockSpec((nb, N_HEADS_B, DK_B, DV_B), lambda bi, c: (bi, 0, 0, 0))
    kt_specs = [pl.BlockSpec((1, QK_B, t), functools.partial(lambda bi, c, k: kt_index(nb * bi + k, c), k=k))
                for k in range(nb)]
    return pl.pallas_call(
        functools.partial(_ret_kernel, nb=nb),
        grid=(b // nb, nc),
        in_specs=[pl.BlockSpec(memory_space=pltpu.SMEM), seq(QK_B)] + kt_specs + [
            seq(V_B), seq(V_B), state_spec,
            pl.BlockSpec((N_HEADS_B, t, t), const3),
            pl.BlockSpec((N_HEADS_B, t, DV_B), const3),
            pl.BlockSpec((N_HEADS_B, 1, t), const3),
            pl.BlockSpec((1, V_B), lambda bi, c: (0, 0)),
        ],
        out_specs=(seq(V_B), state_spec),
        out_shape=(jax.ShapeDtypeStruct((b, s, V_B), BF16),
                   jax.ShapeDtypeStruct((b, N_HEADS_B, DK_B, DV_B), F32)),
        compiler_params=_params(("arbitrary", "arbitrary")),
        name="retention",
    )(g_t, qb, *([kbt] * nb), vb, gb, state0, dmask, qd, kd, ret_norm_g.reshape(1, V_B))


def _split_bf16(x):
    hi = x.astype(BF16)
    lo = (x - hi.astype(F32)).astype(BF16)
    return hi, lo


def _merge_kernel(x_ref, att_ref, bin_ref, gmix_ref, wga_ref, wgb_ref, wpa_ref, wpb_ref, wout_ref,
                  gffn_ref, wr_hi_ref, wr_lo_ref, br_ref, x1_ref, hx_ref, cls_ref):
    x = x_ref[...]
    h = _rms(x, gmix_ref[...]).astype(BF16)
    gate_a = jnp.dot(h, wga_ref[...], preferred_element_type=F32)
    gate_b = jnp.dot(h, wgb_ref[...], preferred_element_type=F32)
    a = jnp.dot(att_ref[...], wpa_ref[...], preferred_element_type=F32)
    b = jnp.dot(bin_ref[...], wpb_ref[...], preferred_element_type=F32)
    m = jax.nn.sigmoid(gate_a) * a + jax.nn.sigmoid(gate_b) * b
    x1 = x + jnp.dot(m.astype(BF16), wout_ref[...], preferred_element_type=F32)
    x1_ref[...] = x1
    h2 = _rms(x1, gffn_ref[...])
    hx_ref[:, :D_MODEL] = h2

    h_hi, h_lo = _split_bf16(h2)
    nt = (((1,), (1,)), ((), ()))
    lt = (lax.dot_general(wr_hi_ref[...], h_hi, nt, preferred_element_type=F32)
          + lax.dot_general(wr_hi_ref[...], h_lo, nt, preferred_element_type=F32)
          + lax.dot_general(wr_lo_ref[...], h_hi, nt, preferred_element_type=F32)) + br_ref[...]
    tm = lt.shape[1]
    row = lax.broadcasted_iota(jnp.int32, (8, tm), 0)
    lg = lt[0:8, :]
    mg = jnp.max(lg, axis=0, keepdims=True)
    grp = jnp.min(jnp.where(lg == mg, row, 8), axis=0, keepdims=True)
    p_grp = 1.0 / jnp.sum(jnp.exp(lg - mg), axis=0, keepdims=True)
    le = jnp.zeros((8, tm), F32)
    for g in range(N_GROUPS):
        le = jnp.where(grp == g, lt[8 + 8 * g:16 + 8 * g, :], le)
    m0 = jnp.max(le, axis=0, keepdims=True)
    i0 = jnp.min(jnp.where(le == m0, row, 8), axis=0, keepdims=True)
    rest = jnp.where(row == i0, jnp.float32(-jnp.inf), le)
    m1 = jnp.max(rest, axis=0, keepdims=True)
    i1 = jnp.min(jnp.where(rest == m1, row, 8), axis=0, keepdims=True)
    e = jnp.exp(m1 - m0)
    w0 = (1.0 / (1.0 + e)) * p_grp
    w1 = (e / (1.0 + e)) * p_grp
    ea = jnp.minimum(i0, i1)
    eb = jnp.maximum(i0, i1)
    pair = ((ea * (2 * EXPERTS_PER_GROUP - 1 - ea)) >> 1) + (eb - ea - 1)
    cls_ref[...] = jnp.where(row == 0, grp * PAIRS_PER_GROUP + pair, 0)
    wa = jnp.where(i0 < i1, w0, w1)
    wb = jnp.where(i0 < i1, w1, w0)
    wrow = lax.broadcasted_iota(jnp.int32, (LANES, tm), 0)
    wslab = jnp.where(wrow == 0, wa, jnp.where(wrow == 1, wb, 0.0))
    hx_ref[:, D_MODEL:] = wslab.T


def _merge(x2d, att, b_in, lw, *, tm):
    n = x2d.shape[0]
    row = lambda i: (i, 0)
    const = lambda i: (0, 0)
    full = lambda a: pl.BlockSpec(a.shape, const)
    return pl.pallas_call(
        _merge_kernel,
        grid=(n // tm,),
        in_specs=[
            pl.BlockSpec((tm, D_MODEL), row), pl.BlockSpec((tm, W_A), row), pl.BlockSpec((tm, V_B), row),
            full(lw["g_mix"]), full(lw["w_ga"]), full(lw["w_gb"]), full(lw["w_pa"]), full(lw["w_pb"]),
            full(lw["w_out"]), full(lw["g_ffn"]), full(lw["wr_hi"]), full(lw["wr_lo"]), full(lw["b_r"]),
        ],
        out_specs=(pl.BlockSpec((tm, D_MODEL), row), pl.BlockSpec((tm, HX_W), row),
                   pl.BlockSpec((8, tm), lambda i: (0, i))),
        out_shape=(jax.ShapeDtypeStruct((n, D_MODEL), F32), jax.ShapeDtypeStruct((n, HX_W), F32),
                   jax.ShapeDtypeStruct((8, n), jnp.int32)),
        compiler_params=_params(("arbitrary",)),
        name="merge",
    )(x2d, att, b_in, lw["g_mix"], lw["w_ga"], lw["w_gb"], lw["w_pa"], lw["w_pb"], lw["w_out"],
      lw["g_ffn"], lw["wr_hi"], lw["wr_lo"], lw["b_r"])


def _class_onehot(cls_row, base, n_real):
    t = cls_row.shape[1]
    crow = lax.broadcasted_iota(jnp.int32, (CLASS_ROWS, t), 0)
    tok = base + lax.broadcasted_iota(jnp.int32, (CLASS_ROWS, t), 1)
    return (cls_row == crow) & (tok < n_real)


def _rank_kernel(cls_ref, tri_ref, rank_ref, counts_ref, *, n_real):
    i = pl.program_id(0)

    @pl.when(i == 0)
    def _():
        counts_ref[...] = jnp.zeros_like(counts_ref)

    t = tri_ref.shape[0]
    hot = _class_onehot(cls_ref[0:1, :], i * t, n_real)
    incl = jnp.dot(jnp.where(hot, 1.0, 0.0).astype(BF16), tri_ref[...], preferred_element_type=F32)
    carry = counts_ref[:, 0:1]
    rank = jnp.sum(jnp.where(hot, incl + carry, 0.0), axis=0, keepdims=True) - 1.0
    rank_ref[...] = jnp.broadcast_to(rank, rank_ref.shape).astype(jnp.int32)
    counts_ref[...] = counts_ref[...] + incl[:, t - 1:t]


def _slot_kernel(cls_ref, rank_ref, pstart_ref, slot_ref, *, n_real):
    t = cls_ref.shape[1]
    hot = _class_onehot(cls_ref[0:1, :], pl.program_id(0) * t, n_real)
    start = jnp.sum(jnp.where(hot, pstart_ref[...], 0.0), axis=0, keepdims=True)
    slot_ref[...] = jnp.broadcast_to(start.astype(jnp.int32) + rank_ref[0:1, :], slot_ref.shape)


def _routing_plan(cls_all, n_real):
    n_pad = cls_all.shape[1]
    nblk = n_pad // PLAN_T
    tri = jnp.asarray(np.triu(np.ones((PLAN_T, PLAN_T), np.float32)), BF16)
    tok = lambda i: (0, i)
    const = lambda i: (0, 0)
    rank, counts = pl.pallas_call(
        functools.partial(_rank_kernel, n_real=n_real),
        grid=(nblk,),
        in_specs=[pl.BlockSpec((8, PLAN_T), tok), pl.BlockSpec((PLAN_T, PLAN_T), const)],
        out_specs=(pl.BlockSpec((8, PLAN_T), tok), pl.BlockSpec((CLASS_ROWS, LANES), const)),
        out_shape=(jax.ShapeDtypeStruct((8, n_pad), jnp.int32), jax.ShapeDtypeStruct((CLASS_ROWS, LANES), F32)),
        compiler_params=_params(("arbitrary",)),
        name="moe_rank",
    )(cls_all, tri)
    counts = counts[:, 0].astype(jnp.int32)
    psizes = ((counts + MOE_TM - 1) // MOE_TM) * MOE_TM
    pends = jnp.cumsum(psizes)
    pstart = (pends - psizes).astype(F32).reshape(CLASS_ROWS, 1)
    slot = pl.pallas_call(
        functools.partial(_slot_kernel, n_real=n_real),
        grid=(nblk,),
        in_specs=[pl.BlockSpec((8, PLAN_T), tok), pl.BlockSpec((8, PLAN_T), tok),
                  pl.BlockSpec((CLASS_ROWS, 1), const)],
        out_specs=pl.BlockSpec((8, PLAN_T), tok),
        out_shape=jax.ShapeDtypeStruct((8, n_pad), jnp.int32),
        compiler_params=_params(("arbitrary",)),
        name="moe_slot",
    )(cls_all, rank, pstart)
    return slot[0], pends


def _dispatch_kernel(slot_ref, hxp_ref, hxs_ref, xs_in_hbm, xs_hbm, buf, sem, *, nb_p, n_s):
    del xs_in_hbm
    i = pl.program_id(0)
    par = i % 2

    def row_copy(r, s):
        return pltpu.make_async_copy(buf.at[s, pl.ds(r, 1)], xs_hbm.at[pl.ds(slot_ref[0, 0, r], 1)], sem.at[s])

    def for_rows(n, fn):
        def body(r, c):
            fn(r)
            return c
        lax.fori_loop(0, n, body, 0, unroll=8)

    def wait_block(n, s):
        for_rows(n, lambda r: row_copy(0, s).wait())

    @pl.when(i >= 2)
    def _():
        wait_block(PLAN_T, par)

    @pl.when(i < nb_p)
    def _():
        buf[par] = hxp_ref[...]
        for r in range(PLAN_T):
            row_copy(r, par).start()

    @pl.when(i == nb_p)
    def _():
        buf[par, :n_s] = hxs_ref[...]
        for_rows(n_s, lambda r: row_copy(r, par).start())
        if nb_p >= 1:
            wait_block(PLAN_T, 1 - par)
        wait_block(n_s, par)


def _dispatch(slot, hx_p, hx_s, n_slots):
    n_p, n_s = hx_p.shape[0], hx_s.shape[0]
    nb_p = n_p // PLAN_T
    assert n_p % PLAN_T == 0 and 0 < n_s <= PLAN_T
    slot3 = slot.reshape(-1, 1, PLAN_T)
    assert slot3.shape[0] == nb_p + 1
    any_spec = pl.BlockSpec(memory_space=pl.ANY)
    return pl.pallas_call(
        functools.partial(_dispatch_kernel, nb_p=nb_p, n_s=n_s),
        grid=(nb_p + 1,),
        in_specs=[pl.BlockSpec((1, 1, PLAN_T), lambda i: (i, 0, 0), memory_space=pltpu.SMEM),
                  pl.BlockSpec((PLAN_T, HX_W), lambda i: (jnp.minimum(i, nb_p - 1), 0)),
                  pl.BlockSpec((n_s, HX_W), lambda i: (0, 0)), any_spec],
        out_specs=any_spec,
        out_shape=jax.ShapeDtypeStruct((n_slots, HX_W), F32),
        scratch_shapes=[pltpu.VMEM((2, PLAN_T, HX_W), F32), pltpu.SemaphoreType.DMA((2,))],
        input_output_aliases={3: 0},
        compiler_params=_params(("arbitrary",)),
        name="moe_dispatch",
    )(slot3, hx_p, hx_s, jnp.zeros((n_slots, HX_W), F32))


def _expert_kernel(ea_ref, eb_ref, nv_ref, xs_ref, wga_ref, wua_ref, wda_ref, wgb_ref, wub_ref, wdb_ref, ys_ref):
    del ea_ref, eb_ref

    def expert(x, wg_ref, wu_ref, wd_ref):
        g = jnp.dot(x, wg_ref[0], preferred_element_type=F32)
        u = jnp.dot(x, wu_ref[0], preferred_element_type=F32)
        act = (g * jax.nn.sigmoid(g) * u).astype(BF16)
        return jnp.dot(act, wd_ref[0], preferred_element_type=F32)

    @pl.when(pl.program_id(0) < nv_ref[0])
    def _():
        x = xs_ref[:, :D_MODEL].astype(BF16)
        wa = xs_ref[:, D_MODEL:D_MODEL + 1]
        wb = xs_ref[:, D_MODEL + 1:D_MODEL + 2]
        ys_ref[...] = expert(x, wga_ref, wua_ref, wda_ref) * wa + expert(x, wgb_ref, wub_ref, wdb_ref) * wb

    @pl.when(pl.program_id(0) >= nv_ref[0])
    def _():
        ys_ref[...] = jnp.zeros_like(ys_ref)


def _class_experts():
    ea, eb = [], []
    for g in range(N_GROUPS):
        for a in range(EXPERTS_PER_GROUP):
            for b in range(a + 1, EXPERTS_PER_GROUP):
                ea.append(g * EXPERTS_PER_GROUP + a)
                eb.append(g * EXPERTS_PER_GROUP + b)
    return np.asarray(ea, np.int32), np.asarray(eb, np.int32)


def _experts(xs, pends, w_gate, w_up, w_down):
    n_tiles = xs.shape[0] // MOE_TM
    nv = (pends[N_CLASSES - 1] // MOE_TM).astype(jnp.int32)
    tile_cls = jnp.sum(pends[None, :N_CLASSES] <= (jnp.arange(n_tiles) * MOE_TM)[:, None], axis=1)
    tile_cls = jnp.minimum(tile_cls, N_CLASSES - 1)
    cls_ea, cls_eb = _class_experts()
    tile_ea = jnp.asarray(cls_ea)[tile_cls]
    tile_eb = jnp.asarray(cls_eb)[tile_cls]
    wa_spec = lambda shp: pl.BlockSpec((1,) + shp, lambda i, ea, eb, nv_: (ea[i], 0, 0))
    wb_spec = lambda shp: pl.BlockSpec((1,) + shp, lambda i, ea, eb, nv_: (eb[i], 0, 0))
    up, down = (D_MODEL, D_EXPERT), (D_EXPERT, D_MODEL)
    grid_spec = pltpu.PrefetchScalarGridSpec(
        num_scalar_prefetch=3,
        grid=(n_tiles,),
        in_specs=[pl.BlockSpec((MOE_TM, HX_W), lambda i, ea, eb, nv_: (i, 0)),
                  wa_spec(up), wa_spec(up), wa_spec(down), wb_spec(up), wb_spec(up), wb_spec(down)],
        out_specs=pl.BlockSpec((MOE_TM, D_MODEL), lambda i, ea, eb, nv_: (i, 0)),
    )
    return pl.pallas_call(
        _expert_kernel,
        grid_spec=grid_spec,
        out_shape=jax.ShapeDtypeStruct((n_tiles * MOE_TM, D_MODEL), F32),
        compiler_params=_params(("arbitrary",)),
        name="moe_experts",
    )(tile_ea, tile_eb, nv.reshape(1), xs, w_gate, w_up, w_down, w_gate, w_up, w_down)


def _ple_kernel(slot_ref, slotn_ref, x1_ref, p_ref, wproj_ref, wgate_ref, gple_ref, gfin_ref, ys_hbm,
                y_ref, ybuf, sem, *, tm):
    i = pl.program_id(0)
    par = i % 2

    def row_copy(idx_ref, r, s):
        return pltpu.make_async_copy(ys_hbm.at[pl.ds(idx_ref[0, 0, r], 1)], ybuf.at[s, pl.ds(r, 1)], sem.at[s])

    def for_rows(fn):
        def body(r, c):
            fn(r)
            return c
        lax.fori_loop(0, tm, body, 0, unroll=8)

    @pl.when(i == 0)
    def _():
        for_rows(lambda r: row_copy(slot_ref, r, 0).start())

    for_rows(lambda r: row_copy(slot_ref, 0, par).wait())

    for r in range(tm):
        row_copy(slotn_ref, r, 1 - par).start()

    x2 = x1_ref[...] + ybuf[par]
    proj = jnp.dot(p_ref[...].astype(BF16), wproj_ref[...], preferred_element_type=F32)
    gate = jnp.dot(_rms(x2, gple_ref[...]).astype(BF16), wgate_ref[...], preferred_element_type=F32)
    x3 = x2 + proj * jax.nn.sigmoid(gate)
    y_ref[...] = _rms(x3, gfin_ref[...])

    @pl.when(i == pl.num_programs(0) - 1)
    def _():
        for_rows(lambda r: row_copy(slot_ref, 0, 1 - par).wait())


def _ple(x1, ys, slot, p2d, lw, g_final, *, tm):
    n = x1.shape[0]
    nt = n // tm
    slot3 = slot.reshape(nt, 1, tm)
    row = lambda i: (i, 0)
    const = lambda i: (0, 0)
    idx_spec = lambda f: pl.BlockSpec((1, 1, tm), f, memory_space=pltpu.SMEM)
    return pl.pallas_call(
        functools.partial(_ple_kernel, tm=tm),
        grid=(nt,),
        in_specs=[
            idx_spec(lambda i: (i, 0, 0)), idx_spec(lambda i: (jnp.minimum(i + 1, nt - 1), 0, 0)),
            pl.BlockSpec((tm, D_MODEL), row), pl.BlockSpec((tm, D_PLE), row),
            pl.BlockSpec((D_PLE, D_MODEL), const), pl.BlockSpec((D_MODEL, D_MODEL), const),
            pl.BlockSpec((1, D_MODEL), const), pl.BlockSpec((1, D_MODEL), const),
            pl.BlockSpec(memory_space=pl.ANY),
        ],
        out_specs=pl.BlockSpec((tm, D_MODEL), row),
        out_shape=jax.ShapeDtypeStruct((n, D_MODEL), F32),
        scratch_shapes=[pltpu.VMEM((2, tm, D_MODEL), F32), pltpu.SemaphoreType.DMA((2,))],
        compiler_params=_params(("arbitrary",)),
        name="ple",
    )(slot3, slot3, x1, p2d, lw["w_ple_proj"], lw["w_ple_gate"], lw["g_ple"], g_final.reshape(1, D_MODEL), ys)


def _layer_weights(i, norm_mix_g, w_in, ret_norm_g, w_proj_a, w_proj_b, w_out, norm_ffn_g,
                   w_router_group, b_router_group, w_router_expert, b_router_expert,
                   w_gate_e, w_up_e, w_down_e, norm_ple_g, w_ple_gate, w_ple_proj):
    w = w_in[i]
    o = _IN_OFFS
    cols = lambda k: w[:, o[k]:o[k + 1]]
    w_main = jnp.concatenate([cols(0), cols(1), cols(2), cols(3), cols(5), cols(6)], axis=1).astype(BF16)
    wr = jnp.zeros((ROUTER_ROWS, D_MODEL), F32)
    wr = wr.at[:N_GROUPS].set(w_router_group[i].T).at[8:].set(w_router_expert[i].T)
    br = jnp.full((ROUTER_ROWS,), NEG_BIG, F32)
    br = br.at[:N_GROUPS].set(b_router_group[i].astype(F32)).at[8:].set(b_router_expert[i].astype(F32))
    wr_hi, wr_lo = _split_bf16(wr)
    return {
        "g_mix": norm_mix_g[i].reshape(1, D_MODEL), "w_main": w_main, "w_kt": cols(4).T.astype(BF16),
        "w_qvt": jnp.concatenate([cols(0).T, cols(2).T], axis=0).astype(BF16),
        "w_ga": cols(7).astype(BF16), "w_gb": cols(8).astype(BF16),
        "ret_norm_g": ret_norm_g[i], "w_pa": w_proj_a[i].astype(BF16), "w_pb": w_proj_b[i].astype(BF16),
        "w_out": w_out[i].astype(BF16), "g_ffn": norm_ffn_g[i].reshape(1, D_MODEL),
        "wr_hi": wr_hi, "wr_lo": wr_lo, "b_r": br.reshape(ROUTER_ROWS, 1),
        "w_gate": w_gate_e[i].astype(BF16), "w_up": w_up_e[i].astype(BF16), "w_down": w_down_e[i].astype(BF16),
        "g_ple": norm_ple_g[i].reshape(1, D_MODEL), "w_ple_gate": w_ple_gate[i].astype(BF16),
        "w_ple_proj": w_ple_proj[i].astype(BF16),
    }


def _moe(hx_p, cls_p, hx_s, cls_s, lw):
    n_p, n_s = hx_p.shape[0], hx_s.shape[0]
    n_real = n_p + n_s
    n_pad = (n_p // PLAN_T + 1) * PLAN_T
    cls_all = jnp.concatenate([cls_p, cls_s, jnp.zeros((8, n_pad - n_real), jnp.int32)], axis=1)
    slot, pends = _routing_plan(cls_all, n_real)
    n_slots = (pl.cdiv(n_real, MOE_TM) + N_CLASSES) * MOE_TM
    xs = _dispatch(slot, hx_p, hx_s, n_slots)
    ys = _experts(xs, pends, lw["w_gate"], lw["w_up"], lw["w_down"])
    return ys, slot[:n_p], slot[n_p:n_real]


def kernel(x_prompt, x_sample, cache_k_a, cache_v_a, state_ret, p_prompt, p_sample, norm_mix_g, w_in, rel_bias, ret_norm_g, w_proj_a, w_proj_b, w_out, norm_ffn_g, w_router_group, b_router_group, w_router_expert, b_router_expert, w_gate_e, w_up_e, w_down_e, norm_ple_g, w_ple_gate, w_ple_proj, final_norm_g):
    depth = w_in.shape[0]
    assert depth == 1, "the final norm is fused into the last layer; deeper stacks are not supported"
    bp, sp, _ = x_prompt.shape
    bs, ss, _ = x_sample.shape
    keep = min(WINDOW_A, sp)
    n_cache = cache_k_a.shape[2]
    log_g = jnp.log(1.0 - 2.0 ** (-5.0 - jnp.arange(N_HEADS_B, dtype=F32)))
    i = 0
    lw = _layer_weights(i, norm_mix_g, w_in, ret_norm_g, w_proj_a, w_proj_b, w_out, norm_ffn_g,
                        w_router_group, b_router_group, w_router_expert, b_router_expert,
                        w_gate_e, w_up_e, w_down_e, norm_ple_g, w_ple_gate, w_ple_proj)

    tm = 512
    assert sp % tm == 0 and keep == tm and sp >= ATT_WIN
    t_ret = 128
    xp2 = x_prompt.reshape(bp * sp, D_MODEL)
    qa, ka, va, qb, kbt, vb, gb, ka32, va32 = _project(
        xp2, jnp.arange(sp), lw["g_mix"], lw["w_main"], lw["w_kt"], lw["w_qvt"], tm=tm,
        tiles_per_keep=sp // tm, feature_major_qv=True)
    r3 = lambda a: a.reshape(bp, sp, a.shape[-1])
    att = _attention_prompt(qa, r3(ka), va, rel_bias[i], bp)
    per_tile = tm // t_ret
    tiles_per_b = sp // tm
    b_in, s_prompt = _retention(
        r3(qb), kbt, r3(vb), r3(gb), jnp.zeros((bp, N_HEADS_B, DK_B, DV_B), F32), log_g, lw["ret_norm_g"],
        t=t_ret, kt_index=lambda bi, c: (bi * tiles_per_b + c // per_tile, 0, c % per_tile))
    x1_p, hx_p, cls_p = _merge(xp2, att.reshape(bp * sp, W_A), b_in.reshape(bp * sp, V_B), lw, tm=tm)
    k_a_prompt = ka32.reshape(bp, keep, N_HEADS_A, HEAD_DIM_A)
    v_a_prompt = va32.reshape(bp, keep, N_HEADS_A, HEAD_DIM_A)

    ns = bs * ss
    xs2 = x_sample.reshape(ns, D_MODEL)
    pos_s = jnp.tile(PAST_LEN + jnp.arange(ss), bs)
    qa, ka, va, qb, kbt, vb, gb, ka32, va32 = _project(
        xs2, pos_s, lw["g_mix"], lw["w_main"], lw["w_kt"], lw["w_qvt"], tm=ns, tiles_per_keep=1,
        feature_major_qv=False)
    r3 = lambda a: a.reshape(bs, ss, a.shape[-1])
    k_all = jnp.concatenate([cache_k_a[i].reshape(bs, n_cache, W_A).astype(BF16), r3(ka)], axis=1)
    v_all = jnp.concatenate([cache_v_a[i].reshape(bs, n_cache, W_A).astype(BF16), r3(va)], axis=1)
    att = _attention_sample(r3(qa), k_all, v_all, rel_bias[i], n_cache)
    kbt_s = kbt.reshape(QK_B, bs, ss).transpose(1, 0, 2)
    b_in, s_sample = _retention(
        r3(qb), kbt_s, r3(vb), r3(gb), state_ret[i].astype(F32), log_g, lw["ret_norm_g"],
        t=ss, kt_index=lambda bi, c: (bi, 0, 0))
    x1_s, hx_s, cls_s = _merge(xs2, att.reshape(ns, W_A), b_in.reshape(ns, V_B), lw, tm=ns)

    ys, slot_p, slot_s = _moe(hx_p, cls_p, hx_s, cls_s, lw)
    y_prompt = _ple(x1_p, ys, slot_p, p_prompt[i].reshape(bp * sp, D_PLE), lw, final_norm_g, tm=tm)
    y_sample = _ple(x1_s, ys, slot_s, p_sample[i].reshape(ns, D_PLE), lw, final_norm_g, tm=ns)
    k_a_sample = ka32.reshape(bs, ss, N_HEADS_A, HEAD_DIM_A)
    v_a_sample = va32.reshape(bs, ss, N_HEADS_A, HEAD_DIM_A)

    return (y_prompt.reshape(bp, sp, D_MODEL), y_sample.reshape(bs, ss, D_MODEL),
            k_a_prompt[None], v_a_prompt[None], s_prompt[None],
            k_a_sample[None], v_a_sample[None], s_sample.astype(state_ret.dtype)[None])
```

```python
import functools

import numpy as np
import jax
import jax.numpy as jnp
from jax import lax
from jax.experimental import pallas as pl
from jax.experimental.pallas import tpu as pltpu

F32 = jnp.float32
BF16 = jnp.bfloat16

D_MODEL = 1024
PAST_LEN = 1024
CHUNK = 64
BAND_CHUNKS = 8
WINDOW_A = BAND_CHUNKS * CHUNK
N_HEADS_A = 8
HEAD_DIM_A = 64
W_A = N_HEADS_A * HEAD_DIM_A
REL_CLIP = 128
N_HEADS_B = 4
DK_B = 128
DV_B = 256
QK_B = N_HEADS_B * DK_B
V_B = N_HEADS_B * DV_B
ROPE_BASE = 10000.0
N_GROUPS = 4
EXPERTS_PER_GROUP = 8
N_EXPERTS = N_GROUPS * EXPERTS_PER_GROUP
TOP_K = 2
D_EXPERT = 512
D_PLE = 256
EPS = 1e-6
_IN_SIZES = (W_A, W_A, W_A, QK_B, QK_B, V_B, V_B, D_MODEL, D_MODEL)
_IN_OFFS = tuple(sum(_IN_SIZES[:i]) for i in range(len(_IN_SIZES) + 1))

LANES = 128
ATT_QBLK = 2 * CHUNK
ATT_WIN = (BAND_CHUNKS + 2) * CHUNK
ROUTER_ROWS = 8 + N_EXPERTS
NEG_BIG = -1e30
PAIRS_PER_GROUP = EXPERTS_PER_GROUP * (EXPERTS_PER_GROUP - 1) // 2
N_CLASSES = N_GROUPS * PAIRS_PER_GROUP
CLASS_ROWS = 128
HX_W = D_MODEL + LANES
PLAN_T = 512
MOE_TM = 256
VMEM_LIMIT = 56 * 1024 * 1024


def _params(sem):
    return pltpu.CompilerParams(dimension_semantics=sem, vmem_limit_bytes=VMEM_LIMIT)


def _rms(x, g):
    return x * lax.rsqrt(jnp.mean(x * x, axis=-1, keepdims=True) + EPS) * g


def _proj_kernel(x_ref, g_ref, w_ref, wkt_ref, wqvt_ref, cos_ref, sin_ref, cost_ref, sint_ref,
                 qa_ref, ka_ref, va_ref, qb_ref, kbt_ref, vb_ref, gb_ref, ka32_ref, va32_ref,
                 *, tiles_per_keep, feature_major_qv):
    h = _rms(x_ref[...], g_ref[...]).astype(BF16)
    nt_dims = (((1,), (1,)), ((), ()))

    def seg(lo, hi):
        return jnp.dot(h, w_ref[:, lo:hi], preferred_element_type=F32)

    ka = seg(W_A, 2 * W_A)
    ka_ref[...] = ka.astype(BF16)
    q_scale = HEAD_DIM_A ** -0.5
    if feature_major_qv:
        qvt = lax.dot_general(wqvt_ref[...], h, nt_dims, preferred_element_type=F32)
        for c in range(qa_ref.shape[0]):
            cs = slice(c * LANES, (c + 1) * LANES)
            qa_ref[c] = (qvt[:W_A, cs] * q_scale).astype(BF16)
            va_ref[c] = qvt[W_A:, cs].astype(BF16)
    else:
        qa_ref[...] = (seg(0, W_A) * q_scale).astype(BF16)
        va_ref[...] = seg(2 * W_A, 3 * W_A).astype(BF16)

    @pl.when(pl.program_id(0) % tiles_per_keep == tiles_per_keep - 1)
    def _():
        ka32_ref[...] = ka
        va32_ref[...] = seg(2 * W_A, 3 * W_A)

    qb = seg(3 * W_A, 3 * W_A + QK_B)
    cos = cos_ref[...]
    sin = sin_ref[...]
    for hd in range(N_HEADS_B):
        xh = qb[:, hd * DK_B:(hd + 1) * DK_B]
        qb_ref[:, hd * DK_B:(hd + 1) * DK_B] = (xh * cos + pltpu.roll(xh, DK_B // 2, axis=1) * sin).astype(BF16)

    vb_ref[...] = seg(3 * W_A + QK_B, 3 * W_A + QK_B + V_B).astype(BF16)
    gb_ref[...] = seg(3 * W_A + QK_B + V_B, 3 * W_A + QK_B + 2 * V_B)

    kt = lax.dot_general(wkt_ref[...], h, (((1,), (1,)), ((), ())), preferred_element_type=F32)
    cost = cost_ref[...]
    sint = sint_ref[...]
    half = DK_B // 2
    scale = DK_B ** -0.5
    for hd in range(N_HEADS_B):
        x1 = kt[hd * DK_B:hd * DK_B + half, :]
        x2 = kt[hd * DK_B + half:(hd + 1) * DK_B, :]
        kbt_ref[0, hd * DK_B:hd * DK_B + half, :] = (x1 * cost - x2 * sint) * scale
        kbt_ref[0, hd * DK_B + half:(hd + 1) * DK_B, :] = (x2 * cost + x1 * sint) * scale


def _rope_tables(pos):
    half = DK_B // 2
    freqs = ROPE_BASE ** (-jnp.arange(half, dtype=F32) / half)
    ang = pos.astype(F32)[:, None] * freqs[None, :]
    cos = jnp.cos(ang)
    sin = jnp.sin(ang)
    return (jnp.concatenate([cos, cos], axis=1), jnp.concatenate([-sin, sin], axis=1), cos.T, sin.T)


def _project(x2d, pos_rows, g_norm, w_main, w_kt, w_qvt, *, tm, tiles_per_keep, feature_major_qv):
    n = x2d.shape[0]
    period = pos_rows.shape[0]
    nt = n // tm
    ppt = period // tm
    cos2, sin2, cost, sint = _rope_tables(pos_rows)
    n_keep = n // tiles_per_keep
    row = lambda i: (i, 0)
    const = lambda i: (0, 0)
    if feature_major_qv:
        spt = tm // LANES
        qv_shape = jax.ShapeDtypeStruct((n // LANES, W_A, LANES), BF16)
        qv_spec = pl.BlockSpec((spt, W_A, LANES), lambda i: (i, 0, 0))
    else:
        qv_shape = jax.ShapeDtypeStruct((n, W_A), BF16)
        qv_spec = pl.BlockSpec((tm, W_A), row)
    outs = (
        qv_shape, jax.ShapeDtypeStruct((n, W_A), BF16), qv_shape, jax.ShapeDtypeStruct((n, QK_B), BF16),
        jax.ShapeDtypeStruct((nt, QK_B, tm), F32), jax.ShapeDtypeStruct((n, V_B), BF16),
        jax.ShapeDtypeStruct((n, V_B), F32),
        jax.ShapeDtypeStruct((n_keep, W_A), F32), jax.ShapeDtypeStruct((n_keep, W_A), F32),
    )
    keep_spec = pl.BlockSpec((tm, W_A), lambda i: (i // tiles_per_keep, 0))
    return pl.pallas_call(
        functools.partial(_proj_kernel, tiles_per_keep=tiles_per_keep, feature_major_qv=feature_major_qv),
        grid=(nt,),
        in_specs=[
            pl.BlockSpec((tm, D_MODEL), row),
            pl.BlockSpec((1, D_MODEL), const),
            pl.BlockSpec(w_main.shape, const),
            pl.BlockSpec(w_kt.shape, const),
            pl.BlockSpec(w_qvt.shape, const),
            pl.BlockSpec((tm, DK_B), lambda i: (i % ppt, 0)),
            pl.BlockSpec((tm, DK_B), lambda i: (i % ppt, 0)),
            pl.BlockSpec((DK_B // 2, tm), lambda i: (0, i % ppt)),
            pl.BlockSpec((DK_B // 2, tm), lambda i: (0, i % ppt)),
        ],
        out_specs=(
            qv_spec, pl.BlockSpec((tm, W_A), row), qv_spec,
            pl.BlockSpec((tm, QK_B), row), pl.BlockSpec((1, QK_B, tm), lambda i: (i, 0, 0)),
            pl.BlockSpec((tm, V_B), row), pl.BlockSpec((tm, V_B), row), keep_spec, keep_spec,
        ),
        out_shape=outs,
        compiler_params=_params(("arbitrary",)),
        name="proj",
    )(x2d, g_norm.reshape(1, D_MODEL), w_main, w_kt, w_qvt, cos2, sin2, cost, sint)


def _attend_pairs(q_of, k_of, v_of, bias_of, store):
    for hp in range(N_HEADS_A // 2):
        qp = q_of(hp)
        kw = k_of(hp)
        vw = v_of(hp)
        lane = lax.broadcasted_iota(jnp.int32, qp.shape, 1)
        outs = []
        for hh in range(2):
            in_head = (lane >= hh * HEAD_DIM_A) & (lane < (hh + 1) * HEAD_DIM_A)
            qh = jnp.where(in_head, qp, jnp.zeros_like(qp))
            s = lax.dot_general(qh, kw, (((1,), (1,)), ((), ())), preferred_element_type=F32)
            s = s + bias_of(2 * hp + hh)
            m = jnp.max(s, axis=-1, keepdims=True)
            p = jnp.exp(s - m)
            l = jnp.sum(p, axis=-1, keepdims=True)
            o = jnp.dot(p.astype(BF16), vw, preferred_element_type=F32)
            outs.append(o / l)
        lane_o = lax.broadcasted_iota(jnp.int32, outs[0].shape, 1)
        store(hp, jnp.where(lane_o < HEAD_DIM_A, outs[0], outs[1]).astype(BF16))


def _fold_rows(x, op, reduce_rows):
    r = x.shape[0]
    while r % 16 == 0:
        r //= 2
        x = op(x[:r], x[r:])
    parts = [x[a:a + 8] for a in range(0, r, 8)]
    while len(parts) > 1:
        parts = [op(parts[a], parts[a + 1]) if a + 1 < len(parts) else parts[a] for a in range(0, len(parts), 2)]
    return reduce_rows(parts[0], axis=0, keepdims=True)


def _attn_prompt_kernel(qt_ref, k_ref, vt_ref, bias_ref, o_ref, s_scr, p_scr):
    j = pl.program_id(1)
    first = jnp.maximum(j - BAND_CHUNKS // 2, 0)
    start = pl.multiple_of(first * ATT_QBLK, ATT_QBLK)
    n_slab = ATT_WIN // LANES
    n_pairs = N_HEADS_A // 2

    def scores(hp):
        rows = slice(hp * LANES, (hp + 1) * LANES)
        qt = qt_ref[0, rows, :]
        dim = lax.broadcasted_iota(jnp.int32, qt.shape, 0)
        zero = jnp.zeros_like(qt)
        w = jnp.concatenate([jnp.where(dim < HEAD_DIM_A, qt, zero), jnp.where(dim >= HEAD_DIM_A, qt, zero)], axis=1)
        kw = k_ref[0, pl.ds(start, ATT_WIN), rows]
        return jnp.dot(kw, w, preferred_element_type=F32)

    for hp in range(n_pairs):
        s_scr[hp] = scores(hp) + bias_ref[0, hp]
    denom = []
    for hp in range(n_pairs):
        s = s_scr[hp]
        m = _fold_rows(s, jnp.maximum, jnp.max)
        p = jnp.exp(s - m)
        denom.append(_fold_rows(p, jnp.add, jnp.sum))
        p_scr[hp] = p.astype(BF16)
    for hp in range(n_pairs):
        rows = slice(hp * LANES, (hp + 1) * LANES)
        vt = jnp.concatenate([vt_ref[first + c, rows, :] for c in range(n_slab)], axis=1)
        ot = jnp.dot(vt, p_scr[hp], preferred_element_type=F32) / denom[hp]
        odim = lax.broadcasted_iota(jnp.int32, (LANES, LANES), 0)
        o_pair_t = jnp.where(odim < HEAD_DIM_A, ot[:, :LANES], ot[:, LANES:])
        o_ref[0, :, rows] = o_pair_t.T.astype(BF16)


def _band_bias(table):
    i = np.arange(ATT_QBLK)[None, :]
    jk = np.arange(ATT_WIN)[:, None]
    out = []
    for v in range(BAND_CHUNKS // 2 + 1):
        off_chunks = 2 * v if v < BAND_CHUNKS // 2 else BAND_CHUNKS
        dchunk = (off_chunks + i // CHUNK) - jk // CHUNK
        valid = (dchunk >= 0) & (dchunk <= BAND_CHUNKS)
        n_f = ATT_WIN + ATT_QBLK - 1
        idx = np.clip(off_chunks * CHUNK + (ATT_QBLK - 1) - np.arange(n_f + 1), -REL_CLIP, REL_CLIP) + REL_CLIP
        g = table[:, idx].astype(F32)
        rows = jnp.tile(g, (1, ATT_QBLK))[:, :ATT_QBLK * n_f].reshape(N_HEADS_A, ATT_QBLK, n_f)
        b = rows[:, :, ATT_QBLK - 1:].transpose(0, 2, 1)
        b = jnp.where(valid[None], b, jnp.float32(NEG_BIG))
        b = b.reshape(N_HEADS_A // 2, 2, ATT_WIN, ATT_QBLK).transpose(0, 2, 1, 3)
        out.append(b.reshape(N_HEADS_A // 2, ATT_WIN, 2 * ATT_QBLK))
    return jnp.stack(out)


def _attention_prompt(qat, ka, vat, table, b):
    s = ka.shape[1]
    nq = s // ATT_QBLK
    bias = _band_bias(table)
    nvar = bias.shape[0]
    return pl.pallas_call(
        _attn_prompt_kernel,
        grid=(b, nq),
        in_specs=[
            pl.BlockSpec((1, W_A, LANES), lambda bi, j: (bi * nq + j, 0, 0)),
            pl.BlockSpec((1, s, W_A), lambda bi, j: (bi, 0, 0)),
            pl.BlockSpec((nq, W_A, LANES), lambda bi, j: (bi, 0, 0)),
            pl.BlockSpec((1, N_HEADS_A // 2, ATT_WIN, 2 * ATT_QBLK),
                         lambda bi, j: (jnp.minimum(j, nvar - 1), 0, 0, 0)),
        ],
        out_specs=pl.BlockSpec((1, ATT_QBLK, W_A), lambda bi, j: (bi, j, 0)),
        out_shape=jax.ShapeDtypeStruct((b, s, W_A), BF16),
        scratch_shapes=[pltpu.VMEM((N_HEADS_A // 2, ATT_WIN, 2 * ATT_QBLK), F32),
                        pltpu.VMEM((N_HEADS_A // 2, ATT_WIN, 2 * ATT_QBLK), BF16)],
        compiler_params=_params(("arbitrary", "arbitrary")),
        name="attn_prompt",
    )(qat, ka, vat, bias)


def _attn_sample_kernel(q_ref, k_ref, v_ref, bias_ref, o_ref):
    def sl(hp):
        return slice(hp * LANES, (hp + 1) * LANES)

    def store(hp, val):
        o_ref[0, :, sl(hp)] = val

    _attend_pairs(
        lambda hp: q_ref[0, :, sl(hp)],
        lambda hp: k_ref[0, :, sl(hp)],
        lambda hp: v_ref[0, :, sl(hp)],
        lambda hd: bias_ref[hd],
        store)


def _attention_sample(qa, k_all, v_all, table, n_cache):
    b, n, _ = qa.shape
    nk = k_all.shape[1]
    dist = jnp.arange(n)[:, None] + n_cache - jnp.arange(nk)[None, :]
    bias = table[:, jnp.clip(dist, -REL_CLIP, REL_CLIP) + REL_CLIP].astype(F32)
    return pl.pallas_call(
        _attn_sample_kernel,
        grid=(b,),
        in_specs=[
            pl.BlockSpec((1, n, W_A), lambda bi: (bi, 0, 0)),
            pl.BlockSpec((1, nk, W_A), lambda bi: (bi, 0, 0)),
            pl.BlockSpec((1, nk, W_A), lambda bi: (bi, 0, 0)),
            pl.BlockSpec((N_HEADS_A, n, nk), lambda bi: (0, 0, 0)),
        ],
        out_specs=pl.BlockSpec((1, n, W_A), lambda bi: (bi, 0, 0)),
        out_shape=jax.ShapeDtypeStruct((b, n, W_A), BF16),
        compiler_params=_params(("arbitrary",)),
        name="attn_sample",
    )(qa, k_all, v_all, bias)


def _ret_kernel(gt_ref, q_ref, *refs, nb):
    kt_refs = refs[:nb]
    v_ref, gb_ref, s0_ref, dmask_ref, qd_ref, kd_ref, gn_ref, out_ref, state_ref = refs[nb:]

    @pl.when(pl.program_id(1) == 0)
    def _():
        state_ref[...] = s0_ref[...]

    for bb in range(nb):
        for hd in range(N_HEADS_B):
            qs = slice(hd * DK_B, (hd + 1) * DK_B)
            vs = slice(hd * DV_B, (hd + 1) * DV_B)
            q = q_ref[bb, :, qs]
            kt = kt_refs[bb][0, qs, :]
            v = v_ref[bb, :, vs]
            state = state_ref[bb, hd]
            scores = jnp.dot(q, kt.astype(BF16), preferred_element_type=F32) * dmask_ref[hd]
            o = jnp.dot(scores.astype(BF16), v, preferred_element_type=F32)
            o = o + jnp.dot(q, state.astype(BF16), preferred_element_type=F32) * qd_ref[hd]
            kd = (kt * kd_ref[hd]).astype(BF16)
            state_ref[bb, hd] = state * gt_ref[hd] + jnp.dot(kd, v, preferred_element_type=F32)
            mu = jnp.mean(o, axis=-1, keepdims=True)
            var = jnp.mean(jnp.square(o - mu), axis=-1, keepdims=True)
            rb = (o - mu) * lax.rsqrt(var + EPS) * gn_ref[:, vs]
            gb = gb_ref[bb, :, vs]
            out_ref[bb, :, vs] = (gb * jax.nn.sigmoid(gb) * rb).astype(BF16)


def _retention(qb, kbt, vb, gb, state0, log_g, ret_norm_g, *, t, kt_index, nb=4):
    b, s, _ = qb.shape
    assert b % nb == 0
    nc = s // t
    idx = jnp.arange(t, dtype=F32)
    diff = idx[:, None] - idx[None, :]
    dmask = jnp.where(diff[None] >= 0, jnp.exp(log_g[:, None, None] * jnp.maximum(diff, 0.0)[None]), 0.0)
    q_decay = jnp.exp(log_g[:, None] * (idx[None, :] + 1.0))
    k_decay = jnp.exp(log_g[:, None] * (t - 1.0 - idx[None, :]))
    g_t = jnp.exp(log_g * t)
    qd = jnp.broadcast_to(q_decay[:, :, None], (N_HEADS_B, t, DV_B))
    kd = k_decay[:, None, :]
    const3 = lambda bi, c: (0, 0, 0)
    seq = lambda w: pl.BlockSpec((nb, t, w), lambda bi, c: (bi, c, 0))
    state_spec = pl.BlockSpec((nb, N_HEADS_B, DK_B, DV_B), lambda bi, c: (bi, 0, 0, 0))
    kt_specs = [pl.BlockSpec((1, QK_B, t), functools.partial(lambda bi, c, k: kt_index(nb * bi + k, c), k=k))
                for k in range(nb)]
    return pl.pallas_call(
        functools.partial(_ret_kernel, nb=nb),
        grid=(b // nb, nc),
        in_specs=[pl.BlockSpec(memory_space=pltpu.SMEM), seq(QK_B)] + kt_specs + [
            seq(V_B), seq(V_B), state_spec,
            pl.BlockSpec((N_HEADS_B, t, t), const3),
            pl.BlockSpec((N_HEADS_B, t, DV_B), const3),
            pl.BlockSpec((N_HEADS_B, 1, t), const3),
            pl.BlockSpec((1, V_B), lambda bi, c: (0, 0)),
        ],
        out_specs=(seq(V_B), state_spec),
        out_shape=(jax.ShapeDtypeStruct((b, s, V_B), BF16),
                   jax.ShapeDtypeStruct((b, N_HEADS_B, DK_B, DV_B), F32)),
        compiler_params=_params(("arbitrary", "arbitrary")),
        name="retention",
    )(g_t, qb, *([kbt] * nb), vb, gb, state0, dmask, qd, kd, ret_norm_g.reshape(1, V_B))


def _split_bf16(x):
    hi = x.astype(BF16)
    lo = (x - hi.astype(F32)).astype(BF16)
    return hi, lo


def _merge_kernel(x_ref, att_ref, bin_ref, gmix_ref, wga_ref, wgb_ref, wpa_ref, wpb_ref, wout_ref,
                  gffn_ref, wr_hi_ref, wr_lo_ref, br_ref, x1_ref, hx_ref, cls_ref):
    x = x_ref[...]
    h = _rms(x, gmix_ref[...]).astype(BF16)
    gate_a = jnp.dot(h, wga_ref[...], preferred_element_type=F32)
    gate_b = jnp.dot(h, wgb_ref[...], preferred_element_type=F32)
    a = jnp.dot(att_ref[...], wpa_ref[...], preferred_element_type=F32)
    b = jnp.dot(bin_ref[...], wpb_ref[...], preferred_element_type=F32)
    m = jax.nn.sigmoid(gate_a) * a + jax.nn.sigmoid(gate_b) * b
    x1 = x + jnp.dot(m.astype(BF16), wout_ref[...], preferred_element_type=F32)
    x1_ref[...] = x1
    h2 = _rms(x1, gffn_ref[...])
    hx_ref[:, :D_MODEL] = h2

    h_hi, h_lo = _split_bf16(h2)
    nt = (((1,), (1,)), ((), ()))
    lt = (lax.dot_general(wr_hi_ref[...], h_hi, nt, preferred_element_type=F32)
          + lax.dot_general(wr_hi_ref[...], h_lo, nt, preferred_element_type=F32)
          + lax.dot_general(wr_lo_ref[...], h_hi, nt, preferred_element_type=F32)) + br_ref[...]
    tm = lt.shape[1]
    row = lax.broadcasted_iota(jnp.int32, (8, tm), 0)
    lg = lt[0:8, :]
    mg = jnp.max(lg, axis=0, keepdims=True)
    grp = jnp.min(jnp.where(lg == mg, row, 8), axis=0, keepdims=True)
    p_grp = 1.0 / jnp.sum(jnp.exp(lg - mg), axis=0, keepdims=True)
    le = jnp.zeros((8, tm), F32)
    for g in range(N_GROUPS):
        le = jnp.where(grp == g, lt[8 + 8 * g:16 + 8 * g, :], le)
    m0 = jnp.max(le, axis=0, keepdims=True)
    i0 = jnp.min(jnp.where(le == m0, row, 8), axis=0, keepdims=True)
    rest = jnp.where(row == i0, jnp.float32(-jnp.inf), le)
    m1 = jnp.max(rest, axis=0, keepdims=True)
    i1 = jnp.min(jnp.where(rest == m1, row, 8), axis=0, keepdims=True)
    e = jnp.exp(m1 - m0)
    w0 = (1.0 / (1.0 + e)) * p_grp
    w1 = (e / (1.0 + e)) * p_grp
    ea = jnp.minimum(i0, i1)
    eb = jnp.maximum(i0, i1)
    pair = ((ea * (2 * EXPERTS_PER_GROUP - 1 - ea)) >> 1) + (eb - ea - 1)
    cls_ref[...] = jnp.where(row == 0, grp * PAIRS_PER_GROUP + pair, 0)
    wa = jnp.where(i0 < i1, w0, w1)
    wb = jnp.where(i0 < i1, w1, w0)
    wrow = lax.broadcasted_iota(jnp.int32, (LANES, tm), 0)
    wslab = jnp.where(wrow == 0, wa, jnp.where(wrow == 1, wb, 0.0))
    hx_ref[:, D_MODEL:] = wslab.T


def _merge(x2d, att, b_in, lw, *, tm):
    n = x2d.shape[0]
    row = lambda i: (i, 0)
    const = lambda i: (0, 0)
    full = lambda a: pl.BlockSpec(a.shape, const)
    return pl.pallas_call(
        _merge_kernel,
        grid=(n // tm,),
        in_specs=[
            pl.BlockSpec((tm, D_MODEL), row), pl.BlockSpec((tm, W_A), row), pl.BlockSpec((tm, V_B), row),
            full(lw["g_mix"]), full(lw["w_ga"]), full(lw["w_gb"]), full(lw["w_pa"]), full(lw["w_pb"]),
            full(lw["w_out"]), full(lw["g_ffn"]), full(lw["wr_hi"]), full(lw["wr_lo"]), full(lw["b_r"]),
        ],
        out_specs=(pl.BlockSpec((tm, D_MODEL), row), pl.BlockSpec((tm, HX_W), row),
                   pl.BlockSpec((8, tm), lambda i: (0, i))),
        out_shape=(jax.ShapeDtypeStruct((n, D_MODEL), F32), jax.ShapeDtypeStruct((n, HX_W), F32),
                   jax.ShapeDtypeStruct((8, n), jnp.int32)),
        compiler_params=_params(("arbitrary",)),
        name="merge",
    )(x2d, att, b_in, lw["g_mix"], lw["w_ga"], lw["w_gb"], lw["w_pa"], lw["w_pb"], lw["w_out"],
      lw["g_ffn"], lw["wr_hi"], lw["wr_lo"], lw["b_r"])


def _class_onehot(cls_row, base, n_real):
    t = cls_row.shape[1]
    crow = lax.broadcasted_iota(jnp.int32, (CLASS_ROWS, t), 0)
    tok = base + lax.broadcasted_iota(jnp.int32, (CLASS_ROWS, t), 1)
    return (cls_row == crow) & (tok < n_real)


def _rank_kernel(cls_ref, tri_ref, rank_ref, counts_ref, *, n_real):
    i = pl.program_id(0)

    @pl.when(i == 0)
    def _():
        counts_ref[...] = jnp.zeros_like(counts_ref)

    t = tri_ref.shape[0]
    hot = _class_onehot(cls_ref[0:1, :], i * t, n_real)
    incl = jnp.dot(jnp.where(hot, 1.0, 0.0).astype(BF16), tri_ref[...], preferred_element_type=F32)
    carry = counts_ref[:, 0:1]
    rank = jnp.sum(jnp.where(hot, incl + carry, 0.0), axis=0, keepdims=True) - 1.0
    rank_ref[...] = jnp.broadcast_to(rank, rank_ref.shape).astype(jnp.int32)
    counts_ref[...] = counts_ref[...] + incl[:, t - 1:t]


def _slot_kernel(cls_ref, rank_ref, pstart_ref, slot_ref, *, n_real):
    t = cls_ref.shape[1]
    hot = _class_onehot(cls_ref[0:1, :], pl.program_id(0) * t, n_real)
    start = jnp.sum(jnp.where(hot, pstart_ref[...], 0.0), axis=0, keepdims=True)
    slot_ref[...] = jnp.broadcast_to(start.astype(jnp.int32) + rank_ref[0:1, :], slot_ref.shape)


def _routing_plan(cls_all, n_real):
    n_pad = cls_all.shape[1]
    nblk = n_pad // PLAN_T
    tri = jnp.asarray(np.triu(np.ones((PLAN_T, PLAN_T), np.float32)), BF16)
    tok = lambda i: (0, i)
    const = lambda i: (0, 0)
    rank, counts = pl.pallas_call(
        functools.partial(_rank_kernel, n_real=n_real),
        grid=(nblk,),
        in_specs=[pl.BlockSpec((8, PLAN_T), tok), pl.BlockSpec((PLAN_T, PLAN_T), const)],
        out_specs=(pl.BlockSpec((8, PLAN_T), tok), pl.BlockSpec((CLASS_ROWS, LANES), const)),
        out_shape=(jax.ShapeDtypeStruct((8, n_pad), jnp.int32), jax.ShapeDtypeStruct((CLASS_ROWS, LANES), F32)),
        compiler_params=_params(("arbitrary",)),
        name="moe_rank",
    )(cls_all, tri)
    counts = counts[:, 0].astype(jnp.int32)
    psizes = ((counts + MOE_TM - 1) // MOE_TM) * MOE_TM
    pends = jnp.cumsum(psizes)
    pstart = (pends - psizes).astype(F32).reshape(CLASS_ROWS, 1)
    slot = pl.pallas_call(
        functools.partial(_slot_kernel, n_real=n_real),
        grid=(nblk,),
        in_specs=[pl.BlockSpec((8, PLAN_T), tok), pl.BlockSpec((8, PLAN_T), tok),
                  pl.BlockSpec((CLASS_ROWS, 1), const)],
        out_specs=pl.BlockSpec((8, PLAN_T), tok),
        out_shape=jax.ShapeDtypeStruct((8, n_pad), jnp.int32),
        compiler_params=_params(("arbitrary",)),
        name="moe_slot",
    )(cls_all, rank, pstart)
    return slot[0], pends


def _dispatch_kernel(slot_ref, hxp_ref, hxs_ref, xs_in_hbm, xs_hbm, buf, sem, *, nb_p, n_s):
    del xs_in_hbm
    i = pl.program_id(0)
    par = i % 2

    def row_copy(r, s):
        return pltpu.make_async_copy(buf.at[s, pl.ds(r, 1)], xs_hbm.at[pl.ds(slot_ref[0, 0, r], 1)], sem.at[s])

    def for_rows(n, fn):
        def body(r, c):
            fn(r)
            return c
        lax.fori_loop(0, n, body, 0, unroll=8)

    def wait_block(n, s):
        for_rows(n, lambda r: row_copy(0, s).wait())

    @pl.when(i >= 2)
    def _():
        wait_block(PLAN_T, par)

    @pl.when(i < nb_p)
    def _():
        buf[par] = hxp_ref[...]
        for r in range(PLAN_T):
            row_copy(r, par).start()

    @pl.when(i == nb_p)
    def _():
        buf[par, :n_s] = hxs_ref[...]
        for_rows(n_s, lambda r: row_copy(r, par).start())
        if nb_p >= 1:
            wait_block(PLAN_T, 1 - par)
        wait_block(n_s, par)


def _dispatch(slot, hx_p, hx_s, n_slots):
    n_p, n_s = hx_p.shape[0], hx_s.shape[0]
    nb_p = n_p // PLAN_T
    assert n_p % PLAN_T == 0 and 0 < n_s <= PLAN_T
    slot3 = slot.reshape(-1, 1, PLAN_T)
    assert slot3.shape[0] == nb_p + 1
    any_spec = pl.BlockSpec(memory_space=pl.ANY)
    return pl.pallas_call(
        functools.partial(_dispatch_kernel, nb_p=nb_p, n_s=n_s),
        grid=(nb_p + 1,),
        in_specs=[pl.BlockSpec((1, 1, PLAN_T), lambda i: (i, 0, 0), memory_space=pltpu.SMEM),
                  pl.BlockSpec((PLAN_T, HX_W), lambda i: (jnp.minimum(i, nb_p - 1), 0)),
                  pl.BlockSpec((n_s, HX_W), lambda i: (0, 0)), any_spec],
        out_specs=any_spec,
        out_shape=jax.ShapeDtypeStruct((n_slots, HX_W), F32),
        scratch_shapes=[pltpu.VMEM((2, PLAN_T, HX_W), F32), pltpu.SemaphoreType.DMA((2,))],
        input_output_aliases={3: 0},
        compiler_params=_params(("arbitrary",)),
        name="moe_dispatch",
    )(slot3, hx_p, hx_s, jnp.zeros((n_slots, HX_W), F32))


def _expert_kernel(ea_ref, eb_ref, nv_ref, xs_ref, wga_ref, wua_ref, wda_ref, wgb_ref, wub_ref, wdb_ref, ys_ref):
    del ea_ref, eb_ref

    def expert(x, wg_ref, wu_ref, wd_ref):
        g = jnp.dot(x, wg_ref[0], preferred_element_type=F32)
        u = jnp.dot(x, wu_ref[0], preferred_element_type=F32)
        act = (g * jax.nn.sigmoid(g) * u).astype(BF16)
        return jnp.dot(act, wd_ref[0], preferred_element_type=F32)

    @pl.when(pl.program_id(0) < nv_ref[0])
    def _():
        x = xs_ref[:, :D_MODEL].astype(BF16)
        wa = xs_ref[:, D_MODEL:D_MODEL + 1]
        wb = xs_ref[:, D_MODEL + 1:D_MODEL + 2]
        ys_ref[...] = expert(x, wga_ref, wua_ref, wda_ref) * wa + expert(x, wgb_ref, wub_ref, wdb_ref) * wb

    @pl.when(pl.program_id(0) >= nv_ref[0])
    def _():
        ys_ref[...] = jnp.zeros_like(ys_ref)


def _class_experts():
    ea, eb = [], []
    for g in range(N_GROUPS):
        for a in range(EXPERTS_PER_GROUP):
            for b in range(a + 1, EXPERTS_PER_GROUP):
                ea.append(g * EXPERTS_PER_GROUP + a)
                eb.append(g * EXPERTS_PER_GROUP + b)
    return np.asarray(ea, np.int32), np.asarray(eb, np.int32)


def _experts(xs, pends, w_gate, w_up, w_down):
    n_tiles = xs.shape[0] // MOE_TM
    nv = (pends[N_CLASSES - 1] // MOE_TM).astype(jnp.int32)
    tile_cls = jnp.sum(pends[None, :N_CLASSES] <= (jnp.arange(n_tiles) * MOE_TM)[:, None], axis=1)
    tile_cls = jnp.minimum(tile_cls, N_CLASSES - 1)
    cls_ea, cls_eb = _class_experts()
    tile_ea = jnp.asarray(cls_ea)[tile_cls]
    tile_eb = jnp.asarray(cls_eb)[tile_cls]
    wa_spec = lambda shp: pl.BlockSpec((1,) + shp, lambda i, ea, eb, nv_: (ea[i], 0, 0))
    wb_spec = lambda shp: pl.BlockSpec((1,) + shp, lambda i, ea, eb, nv_: (eb[i], 0, 0))
    up, down = (D_MODEL, D_EXPERT), (D_EXPERT, D_MODEL)
    grid_spec = pltpu.PrefetchScalarGridSpec(
        num_scalar_prefetch=3,
        grid=(n_tiles,),
        in_specs=[pl.BlockSpec((MOE_TM, HX_W), lambda i, ea, eb, nv_: (i, 0)),
                  wa_spec(up), wa_spec(up), wa_spec(down), wb_spec(up), wb_spec(up), wb_spec(down)],
        out_specs=pl.BlockSpec((MOE_TM, D_MODEL), lambda i, ea, eb, nv_: (i, 0)),
    )
    return pl.pallas_call(
        _expert_kernel,
        grid_spec=grid_spec,
        out_shape=jax.ShapeDtypeStruct((n_tiles * MOE_TM, D_MODEL), F32),
        compiler_params=_params(("arbitrary",)),
        name="moe_experts",
    )(tile_ea, tile_eb, nv.reshape(1), xs, w_gate, w_up, w_down, w_gate, w_up, w_down)


def _ple_kernel(slot_ref, slotn_ref, x1_ref, p_ref, wproj_ref, wgate_ref, gple_ref, gfin_ref, ys_hbm,
                y_ref, ybuf, sem, *, tm):
    i = pl.program_id(0)
    par = i % 2

    def row_copy(idx_ref, r, s):
        return pltpu.make_async_copy(ys_hbm.at[pl.ds(idx_ref[0, 0, r], 1)], ybuf.at[s, pl.ds(r, 1)], sem.at[s])

    def for_rows(fn):
        def body(r, c):
            fn(r)
            return c
        lax.fori_loop(0, tm, body, 0, unroll=8)

    @pl.when(i == 0)
    def _():
        for_rows(lambda r: row_copy(slot_ref, r, 0).start())

    for_rows(lambda r: row_copy(slot_ref, 0, par).wait())

    for r in range(tm):
        row_copy(slotn_ref, r, 1 - par).start()

    x2 = x1_ref[...] + ybuf[par]
    proj = jnp.dot(p_ref[...].astype(BF16), wproj_ref[...], preferred_element_type=F32)
    gate = jnp.dot(_rms(x2, gple_ref[...]).astype(BF16), wgate_ref[...], preferred_element_type=F32)
    x3 = x2 + proj * jax.nn.sigmoid(gate)
    y_ref[...] = _rms(x3, gfin_ref[...])

    @pl.when(i == pl.num_programs(0) - 1)
    def _():
        for_rows(lambda r: row_copy(slot_ref, 0, 1 - par).wait())


def _ple(x1, ys, slot, p2d, lw, g_final, *, tm):
    n = x1.shape[0]
    nt = n // tm
    slot3 = slot.reshape(nt, 1, tm)
    row = lambda i: (i, 0)
    const = lambda i: (0, 0)
    idx_spec = lambda f: pl.BlockSpec((1, 1, tm), f, memory_space=pltpu.SMEM)
    return pl.pallas_call(
        functools.partial(_ple_kernel, tm=tm),
        grid=(nt,),
        in_specs=[
            idx_spec(lambda i: (i, 0, 0)), idx_spec(lambda i: (jnp.minimum(i + 1, nt - 1), 0, 0)),
            pl.BlockSpec((tm, D_MODEL), row), pl.BlockSpec((tm, D_PLE), row),
            pl.BlockSpec((D_PLE, D_MODEL), const), pl.BlockSpec((D_MODEL, D_MODEL), const),
            pl.BlockSpec((1, D_MODEL), const), pl.BlockSpec((1, D_MODEL), const),
            pl.BlockSpec(memory_space=pl.ANY),
        ],
        out_specs=pl.BlockSpec((tm, D_MODEL), row),
        out_shape=jax.ShapeDtypeStruct((n, D_MODEL), F32),
        scratch_shapes=[pltpu.VMEM((2, tm, D_MODEL), F32), pltpu.SemaphoreType.DMA((2,))],
        compiler_params=_params(("arbitrary",)),
        name="ple",
    )(slot3, slot3, x1, p2d, lw["w_ple_proj"], lw["w_ple_gate"], lw["g_ple"], g_final.reshape(1, D_MODEL), ys)


def _layer_weights(i, norm_mix_g, w_in, ret_norm_g, w_proj_a, w_proj_b, w_out, norm_ffn_g,
                   w_router_group, b_router_group, w_router_expert, b_router_expert,
                   w_gate_e, w_up_e, w_down_e, norm_ple_g, w_ple_gate, w_ple_proj):
    w = w_in[i]
    o = _IN_OFFS
    cols = lambda k: w[:, o[k]:o[k + 1]]
    w_main = jnp.concatenate([cols(0), cols(1), cols(2), cols(3), cols(5), cols(6)], axis=1).astype(BF16)
    wr = jnp.zeros((ROUTER_ROWS, D_MODEL), F32)
    wr = wr.at[:N_GROUPS].set(w_router_group[i].T).at[8:].set(w_router_expert[i].T)
    br = jnp.full((ROUTER_ROWS,), NEG_BIG, F32)
    br = br.at[:N_GROUPS].set(b_router_group[i].astype(F32)).at[8:].set(b_router_expert[i].astype(F32))
    wr_hi, wr_lo = _split_bf16(wr)
    return {
        "g_mix": norm_mix_g[i].reshape(1, D_MODEL), "w_main": w_main, "w_kt": cols(4).T.astype(BF16),
        "w_qvt": jnp.concatenate([cols(0).T, cols(2).T], axis=0).astype(BF16),
        "w_ga": cols(7).astype(BF16), "w_gb": cols(8).astype(BF16),
        "ret_norm_g": ret_norm_g[i], "w_pa": w_proj_a[i].astype(BF16), "w_pb": w_proj_b[i].astype(BF16),
        "w_out": w_out[i].astype(BF16), "g_ffn": norm_ffn_g[i].reshape(1, D_MODEL),
        "wr_hi": wr_hi, "wr_lo": wr_lo, "b_r": br.reshape(ROUTER_ROWS, 1),
        "w_gate": w_gate_e[i].astype(BF16), "w_up": w_up_e[i].astype(BF16), "w_down": w_down_e[i].astype(BF16),
        "g_ple": norm_ple_g[i].reshape(1, D_MODEL), "w_ple_gate": w_ple_gate[i].astype(BF16),
        "w_ple_proj": w_ple_proj[i].astype(BF16),
    }


def _moe(hx_p, cls_p, hx_s, cls_s, lw):
    n_p, n_s = hx_p.shape[0], hx_s.shape[0]
    n_real = n_p + n_s
    n_pad = (n_p // PLAN_T + 1) * PLAN_T
    cls_all = jnp.concatenate([cls_p, cls_s, jnp.zeros((8, n_pad - n_real), jnp.int32)], axis=1)
    slot, pends = _routing_plan(cls_all, n_real)
    n_slots = (pl.cdiv(n_real, MOE_TM) + N_CLASSES) * MOE_TM
    xs = _dispatch(slot, hx_p, hx_s, n_slots)
    ys = _experts(xs, pends, lw["w_gate"], lw["w_up"], lw["w_down"])
    return ys, slot[:n_p], slot[n_p:n_real]


def kernel(x_prompt, x_sample, cache_k_a, cache_v_a, state_ret, p_prompt, p_sample, norm_mix_g, w_in, rel_bias, ret_norm_g, w_proj_a, w_proj_b, w_out, norm_ffn_g, w_router_group, b_router_group, w_router_expert, b_router_expert, w_gate_e, w_up_e, w_down_e, norm_ple_g, w_ple_gate, w_ple_proj, final_norm_g):
    depth = w_in.shape[0]
    assert depth == 1, "the final norm is fused into the last layer; deeper stacks are not supported"
    bp, sp, _ = x_prompt.shape
    bs, ss, _ = x_sample.shape
    keep = min(WINDOW_A, sp)
    n_cache = cache_k_a.shape[2]
    log_g = jnp.log(1.0 - 2.0 ** (-5.0 - jnp.arange(N_HEADS_B, dtype=F32)))
    i = 0
    lw = _layer_weights(i, norm_mix_g, w_in, ret_norm_g, w_proj_a, w_proj_b, w_out, norm_ffn_g,
                        w_router_group, b_router_group, w_router_expert, b_router_expert,
                        w_gate_e, w_up_e, w_down_e, norm_ple_g, w_ple_gate, w_ple_proj)

    tm = 512
    assert sp % tm == 0 and keep == tm and sp >= ATT_WIN
    t_ret = 128
    xp2 = x_prompt.reshape(bp * sp, D_MODEL)
    qa, ka, va, qb, kbt, vb, gb, ka32, va32 = _project(
        xp2, jnp.arange(sp), lw["g_mix"], lw["w_main"], lw["w_kt"], lw["w_qvt"], tm=tm,
        tiles_per_keep=sp // tm, feature_major_qv=True)
    r3 = lambda a: a.reshape(bp, sp, a.shape[-1])
    att = _attention_prompt(qa, r3(ka), va, rel_bias[i], bp)
    per_tile = tm // t_ret
    tiles_per_b = sp // tm
    b_in, s_prompt = _retention(
        r3(qb), kbt, r3(vb), r3(gb), jnp.zeros((bp, N_HEADS_B, DK_B, DV_B), F32), log_g, lw["ret_norm_g"],
        t=t_ret, kt_index=lambda bi, c: (bi * tiles_per_b + c // per_tile, 0, c % per_tile))
    x1_p, hx_p, cls_p = _merge(xp2, att.reshape(bp * sp, W_A), b_in.reshape(bp * sp, V_B), lw, tm=tm)
    k_a_prompt = ka32.reshape(bp, keep, N_HEADS_A, HEAD_DIM_A)
    v_a_prompt = va32.reshape(bp, keep, N_HEADS_A, HEAD_DIM_A)

    ns = bs * ss
    xs2 = x_sample.reshape(ns, D_MODEL)
    pos_s = jnp.tile(PAST_LEN + jnp.arange(ss), bs)
    qa, ka, va, qb, kbt, vb, gb, ka32, va32 = _project(
        xs2, pos_s, lw["g_mix"], lw["w_main"], lw["w_kt"], lw["w_qvt"], tm=ns, tiles_per_keep=1,
        feature_major_qv=False)
    r3 = lambda a: a.reshape(bs, ss, a.shape[-1])
    k_all = jnp.concatenate([cache_k_a[i].reshape(bs, n_cache, W_A).astype(BF16), r3(ka)], axis=1)
    v_all = jnp.concatenate([cache_v_a[i].reshape(bs, n_cache, W_A).astype(BF16), r3(va)], axis=1)
    att = _attention_sample(r3(qa), k_all, v_all, rel_bias[i], n_cache)
    kbt_s = kbt.reshape(QK_B, bs, ss).transpose(1, 0, 2)
    b_in, s_sample = _retention(
        r3(qb), kbt_s, r3(vb), r3(gb), state_ret[i].astype(F32), log_g, lw["ret_norm_g"],
        t=ss, kt_index=lambda bi, c: (bi, 0, 0))
    x1_s, hx_s, cls_s = _merge(xs2, att.reshape(ns, W_A), b_in.reshape(ns, V_B), lw, tm=ns)

    ys, slot_p, slot_s = _moe(hx_p, cls_p, hx_s, cls_s, lw)
    y_prompt = _ple(x1_p, ys, slot_p, p_prompt[i].reshape(bp * sp, D_PLE), lw, final_norm_g, tm=tm)
    y_sample = _ple(x1_s, ys, slot_s, p_sample[i].reshape(ns, D_PLE), lw, final_norm_g, tm=ns)
    k_a_sample = ka32.reshape(bs, ss, N_HEADS_A, HEAD_DIM_A)
    v_a_sample = va32.reshape(bs, ss, N_HEADS_A, HEAD_DIM_A)

    return (y_prompt.reshape(bp, sp, D_MODEL), y_sample.reshape(bs, ss, D_MODEL),
            k_a_prompt[None], v_a_prompt[None], s_prompt[None],
            k_a_sample[None], v_a_sample[None], s_sample.astype(state_ret.dtype)[None])
```

```python
import functools

import numpy as np
import jax
import jax.numpy as jnp
from jax import lax
from jax.experimental import pallas as pl
from jax.experimental.pallas import tpu as pltpu

F32 = jnp.float32
BF16 = jnp.bfloat16

D_MODEL = 1024
PAST_LEN = 1024
CHUNK = 64
BAND_CHUNKS = 8
WINDOW_A = BAND_CHUNKS * CHUNK
N_HEADS_A = 8
HEAD_DIM_A = 64
W_A = N_HEADS_A * HEAD_DIM_A
REL_CLIP = 128
N_HEADS_B = 4
DK_B = 128
DV_B = 256
QK_B = N_HEADS_B * DK_B
V_B = N_HEADS_B * DV_B
ROPE_BASE = 10000.0
N_GROUPS = 4
EXPERTS_PER_GROUP = 8
N_EXPERTS = N_GROUPS * EXPERTS_PER_GROUP
TOP_K = 2
D_EXPERT = 512
D_PLE = 256
EPS = 1e-6
_IN_SIZES = (W_A, W_A, W_A, QK_B, QK_B, V_B, V_B, D_MODEL, D_MODEL)
_IN_OFFS = tuple(sum(_IN_SIZES[:i]) for i in range(len(_IN_SIZES) + 1))

LANES = 128
ATT_QBLK = 2 * CHUNK
ATT_WIN = (BAND_CHUNKS + 2) * CHUNK
ROUTER_ROWS = 8 + N_EXPERTS
NEG_BIG = -1e30
PAIRS_PER_GROUP = EXPERTS_PER_GROUP * (EXPERTS_PER_GROUP - 1) // 2
N_CLASSES = N_GROUPS * PAIRS_PER_GROUP
CLASS_ROWS = 128
HX_W = D_MODEL + LANES
PLAN_T = 512
MOE_TM = 256
VMEM_LIMIT = 56 * 1024 * 1024


def _params(sem):
    return pltpu.CompilerParams(dimension_semantics=sem, vmem_limit_bytes=VMEM_LIMIT)


def _rms(x, g):
    return x * lax.rsqrt(jnp.mean(x * x, axis=-1, keepdims=True) + EPS) * g


def _proj_kernel(x_ref, g_ref, w_ref, wkt_ref, wqvt_ref, cos_ref, sin_ref, cost_ref, sint_ref,
                 qa_ref, ka_ref, va_ref, qb_ref, kbt_ref, vb_ref, gb_ref, ka32_ref, va32_ref,
                 *, tiles_per_keep, feature_major_qv):
    h = _rms(x_ref[...], g_ref[...]).astype(BF16)
    nt_dims = (((1,), (1,)), ((), ()))

    def seg(lo, hi):
        return jnp.dot(h, w_ref[:, lo:hi], preferred_element_type=F32)

    ka = seg(W_A, 2 * W_A)
    ka_ref[...] = ka.astype(BF16)
    q_scale = HEAD_DIM_A ** -0.5
    if feature_major_qv:
        qvt = lax.dot_general(wqvt_ref[...], h, nt_dims, preferred_element_type=F32)
        for c in range(qa_ref.shape[0]):
            cs = slice(c * LANES, (c + 1) * LANES)
            qa_ref[c] = (qvt[:W_A, cs] * q_scale).astype(BF16)
            va_ref[c] = qvt[W_A:, cs].astype(BF16)
    else:
        qa_ref[...] = (seg(0, W_A) * q_scale).astype(BF16)
        va_ref[...] = seg(2 * W_A, 3 * W_A).astype(BF16)

    @pl.when(pl.program_id(0) % tiles_per_keep == tiles_per_keep - 1)
    def _():
        ka32_ref[...] = ka
        va32_ref[...] = seg(2 * W_A, 3 * W_A)

    qb = seg(3 * W_A, 3 * W_A + QK_B)
    cos = cos_ref[...]
    sin = sin_ref[...]
    for hd in range(N_HEADS_B):
        xh = qb[:, hd * DK_B:(hd + 1) * DK_B]
        qb_ref[:, hd * DK_B:(hd + 1) * DK_B] = (xh * cos + pltpu.roll(xh, DK_B // 2, axis=1) * sin).astype(BF16)

    vb_ref[...] = seg(3 * W_A + QK_B, 3 * W_A + QK_B + V_B).astype(BF16)
    gb_ref[...] = seg(3 * W_A + QK_B + V_B, 3 * W_A + QK_B + 2 * V_B)

    kt = lax.dot_general(wkt_ref[...], h, (((1,), (1,)), ((), ())), preferred_element_type=F32)
    cost = cost_ref[...]
    sint = sint_ref[...]
    half = DK_B // 2
    scale = DK_B ** -0.5
    for hd in range(N_HEADS_B):
        x1 = kt[hd * DK_B:hd * DK_B + half, :]
        x2 = kt[hd * DK_B + half:(hd + 1) * DK_B, :]
        kbt_ref[0, hd * DK_B:hd * DK_B + half, :] = (x1 * cost - x2 * sint) * scale
        kbt_ref[0, hd * DK_B + half:(hd + 1) * DK_B, :] = (x2 * cost + x1 * sint) * scale


def _rope_tables(pos):
    half = DK_B // 2
    freqs = ROPE_BASE ** (-jnp.arange(half, dtype=F32) / half)
    ang = pos.astype(F32)[:, None] * freqs[None, :]
    cos = jnp.cos(ang)
    sin = jnp.sin(ang)
    return (jnp.concatenate([cos, cos], axis=1), jnp.concatenate([-sin, sin], axis=1), cos.T, sin.T)


def _project(x2d, pos_rows, g_norm, w_main, w_kt, w_qvt, *, tm, tiles_per_keep, feature_major_qv):
    n = x2d.shape[0]
    period = pos_rows.shape[0]
    nt = n // tm
    ppt = period // tm
    cos2, sin2, cost, sint = _rope_tables(pos_rows)
    n_keep = n // tiles_per_keep
    row = lambda i: (i, 0)
    const = lambda i: (0, 0)
    if feature_major_qv:
        spt = tm // LANES
        qv_shape = jax.ShapeDtypeStruct((n // LANES, W_A, LANES), BF16)
        qv_spec = pl.BlockSpec((spt, W_A, LANES), lambda i: (i, 0, 0))
    else:
        qv_shape = jax.ShapeDtypeStruct((n, W_A), BF16)
        qv_spec = pl.BlockSpec((tm, W_A), row)
    outs = (
        qv_shape, jax.ShapeDtypeStruct((n, W_A), BF16), qv_shape, jax.ShapeDtypeStruct((n, QK_B), BF16),
        jax.ShapeDtypeStruct((nt, QK_B, tm), F32), jax.ShapeDtypeStruct((n, V_B), BF16),
        jax.ShapeDtypeStruct((n, V_B), F32),
        jax.ShapeDtypeStruct((n_keep, W_A), F32), jax.ShapeDtypeStruct((n_keep, W_A), F32),
    )
    keep_spec = pl.BlockSpec((tm, W_A), lambda i: (i // tiles_per_keep, 0))
    return pl.pallas_call(
        functools.partial(_proj_kernel, tiles_per_keep=tiles_per_keep, feature_major_qv=feature_major_qv),
        grid=(nt,),
        in_specs=[
            pl.BlockSpec((tm, D_MODEL), row),
            pl.BlockSpec((1, D_MODEL), const),
            pl.BlockSpec(w_main.shape, const),
            pl.BlockSpec(w_kt.shape, const),
            pl.BlockSpec(w_qvt.shape, const),
            pl.BlockSpec((tm, DK_B), lambda i: (i % ppt, 0)),
            pl.BlockSpec((tm, DK_B), lambda i: (i % ppt, 0)),
            pl.BlockSpec((DK_B // 2, tm), lambda i: (0, i % ppt)),
            pl.BlockSpec((DK_B // 2, tm), lambda i: (0, i % ppt)),
        ],
        out_specs=(
            qv_spec, pl.BlockSpec((tm, W_A), row), qv_spec,
            pl.BlockSpec((tm, QK_B), row), pl.BlockSpec((1, QK_B, tm), lambda i: (i, 0, 0)),
            pl.BlockSpec((tm, V_B), row), pl.BlockSpec((tm, V_B), row), keep_spec, keep_spec,
        ),
        out_shape=outs,
        compiler_params=_params(("arbitrary",)),
        name="proj",
    )(x2d, g_norm.reshape(1, D_MODEL), w_main, w_kt, w_qvt, cos2, sin2, cost, sint)


def _attend_pairs(q_of, k_of, v_of, bias_of, store):
    for hp in range(N_HEADS_A // 2):
        qp = q_of(hp)
        kw = k_of(hp)
        vw = v_of(hp)
        lane = lax.broadcasted_iota(jnp.int32, qp.shape, 1)
        outs = []
        for hh in range(2):
            in_head = (lane >= hh * HEAD_DIM_A) & (lane < (hh + 1) * HEAD_DIM_A)
            qh = jnp.where(in_head, qp, jnp.zeros_like(qp))
            s = lax.dot_general(qh, kw, (((1,), (1,)), ((), ())), preferred_element_type=F32)
            s = s + bias_of(2 * hp + hh)
            m = jnp.max(s, axis=-1, keepdims=True)
            p = jnp.exp(s - m)
            l = jnp.sum(p, axis=-1, keepdims=True)
            o = jnp.dot(p.astype(BF16), vw, preferred_element_type=F32)
            outs.append(o / l)
        lane_o = lax.broadcasted_iota(jnp.int32, outs[0].shape, 1)
        store(hp, jnp.where(lane_o < HEAD_DIM_A, outs[0], outs[1]).astype(BF16))


def _fold_rows(x, op, reduce_rows):
    r = x.shape[0]
    while r % 16 == 0:
        r //= 2
        x = op(x[:r], x[r:])
    parts = [x[a:a + 8] for a in range(0, r, 8)]
    while len(parts) > 1:
        parts = [op(parts[a], parts[a + 1]) if a + 1 < len(parts) else parts[a] for a in range(0, len(parts), 2)]
    return reduce_rows(parts[0], axis=0, keepdims=True)


def _attn_prompt_kernel(qt_ref, k_ref, vt_ref, bias_ref, o_ref, s_scr, p_scr):
    j = pl.program_id(1)
    first = jnp.maximum(j - BAND_CHUNKS // 2, 0)
    start = pl.multiple_of(first * ATT_QBLK, ATT_QBLK)
    n_slab = ATT_WIN // LANES
    n_pairs = N_HEADS_A // 2

    def scores(hp):
        rows = slice(hp * LANES, (hp + 1) * LANES)
        qt = qt_ref[0, rows, :]
        dim = lax.broadcasted_iota(jnp.int32, qt.shape, 0)
        zero = jnp.zeros_like(qt)
        w = jnp.concatenate([jnp.where(dim < HEAD_DIM_A, qt, zero), jnp.where(dim >= HEAD_DIM_A, qt, zero)], axis=1)
        kw = k_ref[0, pl.ds(start, ATT_WIN), rows]
        return jnp.dot(kw, w, preferred_element_type=F32)

    for hp in range(n_pairs):
        s_scr[hp] = scores(hp) + bias_ref[0, hp]
    denom = []
    for hp in range(n_pairs):
        s = s_scr[hp]
        m = _fold_rows(s, jnp.maximum, jnp.max)
        p = jnp.exp(s - m)
        denom.append(_fold_rows(p, jnp.add, jnp.sum))
        p_scr[hp] = p.astype(BF16)
    for hp in range(n_pairs):
        rows = slice(hp * LANES, (hp + 1) * LANES)
        vt = jnp.concatenate([vt_ref[first + c, rows, :] for c in range(n_slab)], axis=1)
        ot = jnp.dot(vt, p_scr[hp], preferred_element_type=F32) / denom[hp]
        odim = lax.broadcasted_iota(jnp.int32, (LANES, LANES), 0)
        o_pair_t = jnp.where(odim < HEAD_DIM_A, ot[:, :LANES], ot[:, LANES:])
        o_ref[0, :, rows] = o_pair_t.T.astype(BF16)


def _band_bias(table):
    i = np.arange(ATT_QBLK)[None, :]
    jk = np.arange(ATT_WIN)[:, None]
    out = []
    for v in range(BAND_CHUNKS // 2 + 1):
        off_chunks = 2 * v if v < BAND_CHUNKS // 2 else BAND_CHUNKS
        dchunk = (off_chunks + i // CHUNK) - jk // CHUNK
        valid = (dchunk >= 0) & (dchunk <= BAND_CHUNKS)
        n_f = ATT_WIN + ATT_QBLK - 1
        idx = np.clip(off_chunks * CHUNK + (ATT_QBLK - 1) - np.arange(n_f + 1), -REL_CLIP, REL_CLIP) + REL_CLIP
        g = table[:, idx].astype(F32)
        rows = jnp.tile(g, (1, ATT_QBLK))[:, :ATT_QBLK * n_f].reshape(N_HEADS_A, ATT_QBLK, n_f)
        b = rows[:, :, ATT_QBLK - 1:].transpose(0, 2, 1)
        b = jnp.where(valid[None], b, jnp.float32(NEG_BIG))
        b = b.reshape(N_HEADS_A // 2, 2, ATT_WIN, ATT_QBLK).transpose(0, 2, 1, 3)
        out.append(b.reshape(N_HEADS_A // 2, ATT_WIN, 2 * ATT_QBLK))
    return jnp.stack(out)


def _attention_prompt(qat, ka, vat, table, b):
    s = ka.shape[1]
    nq = s // ATT_QBLK
    bias = _band_bias(table)
    nvar = bias.shape[0]
    return pl.pallas_call(
        _attn_prompt_kernel,
        grid=(b, nq),
        in_specs=[
            pl.BlockSpec((1, W_A, LANES), lambda bi, j: (bi * nq + j, 0, 0)),
            pl.BlockSpec((1, s, W_A), lambda bi, j: (bi, 0, 0)),
            pl.BlockSpec((nq, W_A, LANES), lambda bi, j: (bi, 0, 0)),
            pl.BlockSpec((1, N_HEADS_A // 2, ATT_WIN, 2 * ATT_QBLK),
                         lambda bi, j: (jnp.minimum(j, nvar - 1), 0, 0, 0)),
        ],
        out_specs=pl.BlockSpec((1, ATT_QBLK, W_A), lambda bi, j: (bi, j, 0)),
        out_shape=jax.ShapeDtypeStruct((b, s, W_A), BF16),
        scratch_shapes=[pltpu.VMEM((N_HEADS_A // 2, ATT_WIN, 2 * ATT_QBLK), F32),
                        pltpu.VMEM((N_HEADS_A // 2, ATT_WIN, 2 * ATT_QBLK), BF16)],
        compiler_params=_params(("arbitrary", "arbitrary")),
        name="attn_prompt",
    )(qat, ka, vat, bias)


def _attn_sample_kernel(q_ref, k_ref, v_ref, bias_ref, o_ref):
    def sl(hp):
        return slice(hp * LANES, (hp + 1) * LANES)

    def store(hp, val):
        o_ref[0, :, sl(hp)] = val

    _attend_pairs(
        lambda hp: q_ref[0, :, sl(hp)],
        lambda hp: k_ref[0, :, sl(hp)],
        lambda hp: v_ref[0, :, sl(hp)],
        lambda hd: bias_ref[hd],
        store)


def _attention_sample(qa, k_all, v_all, table, n_cache):
    b, n, _ = qa.shape
    nk = k_all.shape[1]
    dist = jnp.arange(n)[:, None] + n_cache - jnp.arange(nk)[None, :]
    bias = table[:, jnp.clip(dist, -REL_CLIP, REL_CLIP) + REL_CLIP].astype(F32)
    return pl.pallas_call(
        _attn_sample_kernel,
        grid=(b,),
        in_specs=[
            pl.BlockSpec((1, n, W_A), lambda bi: (bi, 0, 0)),
            pl.BlockSpec((1, nk, W_A), lambda bi: (bi, 0, 0)),
            pl.BlockSpec((1, nk, W_A), lambda bi: (bi, 0, 0)),
            pl.BlockSpec((N_HEADS_A, n, nk), lambda bi: (0, 0, 0)),
        ],
        out_specs=pl.BlockSpec((1, n, W_A), lambda bi: (bi, 0, 0)),
        out_shape=jax.ShapeDtypeStruct((b, n, W_A), BF16),
        compiler_params=_params(("arbitrary",)),
        name="attn_sample",
    )(qa, k_all, v_all, bias)


def _ret_kernel(gt_ref, q_ref, *refs, nb):
    kt_refs = refs[:nb]
    v_ref, gb_ref, s0_ref, dmask_ref, qd_ref, kd_ref, gn_ref, out_ref, state_ref = refs[nb:]

    @pl.when(pl.program_id(1) == 0)
    def _():
        state_ref[...] = s0_ref[...]

    for bb in range(nb):
        for hd in range(N_HEADS_B):
            qs = slice(hd * DK_B, (hd + 1) * DK_B)
            vs = slice(hd * DV_B, (hd + 1) * DV_B)
            q = q_ref[bb, :, qs]
            kt = kt_refs[bb][0, qs, :]
            v = v_ref[bb, :, vs]
            state = state_ref[bb, hd]
            scores = jnp.dot(q, kt.astype(BF16), preferred_element_type=F32) * dmask_ref[hd]
            o = jnp.dot(scores.astype(BF16), v, preferred_element_type=F32)
            o = o + jnp.dot(q, state.astype(BF16), preferred_element_type=F32) * qd_ref[hd]
            kd = (kt * kd_ref[hd]).astype(BF16)
            state_ref[bb, hd] = state * gt_ref[hd] + jnp.dot(kd, v, preferred_element_type=F32)
            mu = jnp.mean(o, axis=-1, keepdims=True)
            var = jnp.mean(jnp.square(o - mu), axis=-1, keepdims=True)
            rb = (o - mu) * lax.rsqrt(var + EPS) * gn_ref[:, vs]
            gb = gb_ref[bb, :, vs]
            out_ref[bb, :, vs] = (gb * jax.nn.sigmoid(gb) * rb).astype(BF16)


def _retention(qb, kbt, vb, gb, state0, log_g, ret_norm_g, *, t, kt_index, nb=4):
    b, s, _ = qb.shape
    assert b % nb == 0
    nc = s // t
    idx = jnp.arange(t, dtype=F32)
    diff = idx[:, None] - idx[None, :]
    dmask = jnp.where(diff[None] >= 0, jnp.exp(log_g[:, None, None] * jnp.maximum(diff, 0.0)[None]), 0.0)
    q_decay = jnp.exp(log_g[:, None] * (idx[None, :] + 1.0))
    k_decay = jnp.exp(log_g[:, None] * (t - 1.0 - idx[None, :]))
    g_t = jnp.exp(log_g * t)
    qd = jnp.broadcast_to(q_decay[:, :, None], (N_HEADS_B, t, DV_B))
    kd = k_decay[:, None, :]
    const3 = lambda bi, c: (0, 0, 0)
    seq = lambda w: pl.BlockSpec((nb, t, w), lambda bi, c: (bi, c, 0))
    state_spec = pl.BlockSpec((nb, N_HEADS_B, DK_B, DV_B), lambda bi, c: (bi, 0, 0, 0))
    kt_specs = [pl.BlockSpec((1, QK_B, t), functools.partial(lambda bi, c, k: kt_index(nb * bi + k, c), k=k))
                for k in range(nb)]
    return pl.pallas_call(
        functools.partial(_ret_kernel, nb=nb),
        grid=(b // nb, nc),
        in_specs=[pl.BlockSpec(memory_space=pltpu.SMEM), seq(QK_B)] + kt_specs + [
            seq(V_B), seq(V_B), state_spec,
            pl.BlockSpec((N_HEADS_B, t, t), const3),
            pl.BlockSpec((N_HEADS_B, t, DV_B), const3),
            pl.BlockSpec((N_HEADS_B, 1, t), const3),
            pl.BlockSpec((1, V_B), lambda bi, c: (0, 0)),
        ],
        out_specs=(seq(V_B), state_spec),
        out_shape=(jax.ShapeDtypeStruct((b, s, V_B), BF16),
                   jax.ShapeDtypeStruct((b, N_HEADS_B, DK_B, DV_B), F32)),
        compiler_params=_params(("arbitrary", "arbitrary")),
        name="retention",
    )(g_t, qb, *([kbt] * nb), vb, gb, state0, dmask, qd, kd, ret_norm_g.reshape(1, V_B))


def _split_bf16(x):
    hi = x.astype(BF16)
    lo = (x - hi.astype(F32)).astype(BF16)
    return hi, lo


def _merge_kernel(x_ref, att_ref, bin_ref, gmix_ref, wga_ref, wgb_ref, wpa_ref, wpb_ref, wout_ref,
                  gffn_ref, wr_hi_ref, wr_lo_ref, br_ref, x1_ref, hx_ref, cls_ref):
    x = x_ref[...]
    h = _rms(x, gmix_ref[...]).astype(BF16)
    gate_a = jnp.dot(h, wga_ref[...], preferred_element_type=F32)
    gate_b = jnp.dot(h, wgb_ref[...], preferred_element_type=F32)
    a = jnp.dot(att_ref[...], wpa_ref[...], preferred_element_type=F32)
    b = jnp.dot(bin_ref[...], wpb_ref[...], preferred_element_type=F32)
    m = jax.nn.sigmoid(gate_a) * a + jax.nn.sigmoid(gate_b) * b
    x1 = x + jnp.dot(m.astype(BF16), wout_ref[...], preferred_element_type=F32)
    x1_ref[...] = x1
    h2 = _rms(x1, gffn_ref[...])
    hx_ref[:, :D_MODEL] = h2

    h_hi, h_lo = _split_bf16(h2)
    nt = (((1,), (1,)), ((), ()))
    lt = (lax.dot_general(wr_hi_ref[...], h_hi, nt, preferred_element_type=F32)
          + lax.dot_general(wr_hi_ref[...], h_lo, nt, preferred_element_type=F32)
          + lax.dot_general(wr_lo_ref[...], h_hi, nt, preferred_element_type=F32)) + br_ref[...]
    tm = lt.shape[1]
    row = lax.broadcasted_iota(jnp.int32, (8, tm), 0)
    lg = lt[0:8, :]
    mg = jnp.max(lg, axis=0, keepdims=True)
    grp = jnp.min(jnp.where(lg == mg, row, 8), axis=0, keepdims=True)
    p_grp = 1.0 / jnp.sum(jnp.exp(lg - mg), axis=0, keepdims=True)
    le = jnp.zeros((8, tm), F32)
    for g in range(N_GROUPS):
        le = jnp.where(grp == g, lt[8 + 8 * g:16 + 8 * g, :], le)
    m0 = jnp.max(le, axis=0, keepdims=True)
    i0 = jnp.min(jnp.where(le == m0, row, 8), axis=0, keepdims=True)
    rest = jnp.where(row == i0, jnp.float32(-jnp.inf), le)
    m1 = jnp.max(rest, axis=0, keepdims=True)
    i1 = jnp.min(jnp.where(rest == m1, row, 8), axis=0, keepdims=True)
    e = jnp.exp(m1 - m0)
    w0 = (1.0 / (1.0 + e)) * p_grp
    w1 = (e / (1.0 + e)) * p_grp
    ea = jnp.minimum(i0, i1)
    eb = jnp.maximum(i0, i1)
    pair = ((ea * (2 * EXPERTS_PER_GROUP - 1 - ea)) >> 1) + (eb - ea - 1)
    cls_ref[...] = jnp.where(row == 0, grp * PAIRS_PER_GROUP + pair, 0)
    wa = jnp.where(i0 < i1, w0, w1)
    wb = jnp.where(i0 < i1, w1, w0)
    wrow = lax.broadcasted_iota(jnp.int32, (LANES, tm), 0)
    wslab = jnp.where(wrow == 0, wa, jnp.where(wrow == 1, wb, 0.0))
    hx_ref[:, D_MODEL:] = wslab.T


def _merge(x2d, att, b_in, lw, *, tm):
    n = x2d.shape[0]
    row = lambda i: (i, 0)
    const = lambda i: (0, 0)
    full = lambda a: pl.BlockSpec(a.shape, const)
    return pl.pallas_call(
        _merge_kernel,
        grid=(n // tm,),
        in_specs=[
            pl.BlockSpec((tm, D_MODEL), row), pl.BlockSpec((tm, W_A), row), pl.BlockSpec((tm, V_B), row),
            full(lw["g_mix"]), full(lw["w_ga"]), full(lw["w_gb"]), full(lw["w_pa"]), full(lw["w_pb"]),
            full(lw["w_out"]), full(lw["g_ffn"]), full(lw["wr_hi"]), full(lw["wr_lo"]), full(lw["b_r"]),
        ],
        out_specs=(pl.BlockSpec((tm, D_MODEL), row), pl.BlockSpec((tm, HX_W), row),
                   pl.BlockSpec((8, tm), lambda i: (0, i))),
        out_shape=(jax.ShapeDtypeStruct((n, D_MODEL), F32), jax.ShapeDtypeStruct((n, HX_W), F32),
                   jax.ShapeDtypeStruct((8, n), jnp.int32)),
        compiler_params=_params(("arbitrary",)),
        name="merge",
    )(x2d, att, b_in, lw["g_mix"], lw["w_ga"], lw["w_gb"], lw["w_pa"], lw["w_pb"], lw["w_out"],
      lw["g_ffn"], lw["wr_hi"], lw["wr_lo"], lw["b_r"])


def _class_onehot(cls_row, base, n_real):
    t = cls_row.shape[1]
    crow = lax.broadcasted_iota(jnp.int32, (CLASS_ROWS, t), 0)
    tok = base + lax.broadcasted_iota(jnp.int32, (CLASS_ROWS, t), 1)
    return (cls_row == crow) & (tok < n_real)


def _rank_kernel(cls_ref, tri_ref, rank_ref, counts_ref, *, n_real):
    i = pl.program_id(0)

    @pl.when(i == 0)
    def _():
        counts_ref[...] = jnp.zeros_like(counts_ref)

    t = tri_ref.shape[0]
    hot = _class_onehot(cls_ref[0:1, :], i * t, n_real)
    incl = jnp.dot(jnp.where(hot, 1.0, 0.0).astype(BF16), tri_ref[...], preferred_element_type=F32)
    carry = counts_ref[:, 0:1]
    rank = jnp.sum(jnp.where(hot, incl + carry, 0.0), axis=0, keepdims=True) - 1.0
    rank_ref[...] = jnp.broadcast_to(rank, rank_ref.shape).astype(jnp.int32)
    counts_ref[...] = counts_ref[...] + incl[:, t - 1:t]


def _slot_kernel(cls_ref, rank_ref, pstart_ref, slot_ref, *, n_real):
    t = cls_ref.shape[1]
    hot = _class_onehot(cls_ref[0:1, :], pl.program_id(0) * t, n_real)
    start = jnp.sum(jnp.where(hot, pstart_ref[...], 0.0), axis=0, keepdims=True)
    slot_ref[...] = jnp.broadcast_to(start.astype(jnp.int32) + rank_ref[0:1, :], slot_ref.shape)


def _routing_plan(cls_all, n_real):
    n_pad = cls_all.shape[1]
    nblk = n_pad // PLAN_T
    tri = jnp.asarray(np.triu(np.ones((PLAN_T, PLAN_T), np.float32)), BF16)
    tok = lambda i: (0, i)
    const = lambda i: (0, 0)
    rank, counts = pl.pallas_call(
        functools.partial(_rank_kernel, n_real=n_real),
        grid=(nblk,),
        in_specs=[pl.BlockSpec((8, PLAN_T), tok), pl.BlockSpec((PLAN_T, PLAN_T), const)],
        out_specs=(pl.BlockSpec((8, PLAN_T), tok), pl.BlockSpec((CLASS_ROWS, LANES), const)),
        out_shape=(jax.ShapeDtypeStruct((8, n_pad), jnp.int32), jax.ShapeDtypeStruct((CLASS_ROWS, LANES), F32)),
        compiler_params=_params(("arbitrary",)),
        name="moe_rank",
    )(cls_all, tri)
    counts = counts[:, 0].astype(jnp.int32)
    psizes = ((counts + MOE_TM - 1) // MOE_TM) * MOE_TM
    pends = jnp.cumsum(psizes)
    pstart = (pends - psizes).astype(F32).reshape(CLASS_ROWS, 1)
    slot = pl.pallas_call(
        functools.partial(_slot_kernel, n_real=n_real),
        grid=(nblk,),
        in_specs=[pl.BlockSpec((8, PLAN_T), tok), pl.BlockSpec((8, PLAN_T), tok),
                  pl.BlockSpec((CLASS_ROWS, 1), const)],
        out_specs=pl.BlockSpec((8, PLAN_T), tok),
        out_shape=jax.ShapeDtypeStruct((8, n_pad), jnp.int32),
        compiler_params=_params(("arbitrary",)),
        name="moe_slot",
    )(cls_all, rank, pstart)
    return slot[0], pends


def _dispatch_kernel(slot_ref, slots_ref, hxp_ref, hxs_ref, xs_in_hbm, xs_hbm, buf, sem, *, ns_p, n_s):
    del xs_in_hbm
    i = pl.program_id(0)

    def row_copy(idx_ref, k, r, s):
        return pltpu.make_async_copy(buf.at[s, pl.ds(r, 1)],
                                     xs_hbm.at[pl.ds(idx_ref[0, 0, k * PLAN_T + r], 1)], sem.at[s])

    def wait_block(n, s):
        def body(r, c):
            row_copy(slot_ref, 0, 0, s).wait()
            return c
        lax.fori_loop(0, n, body, 0, unroll=8)

    for k in range(2):
        @pl.when(i >= 1)
        def _():
            wait_block(PLAN_T, k)

        @pl.when(i < ns_p)
        def _():
            buf[k] = hxp_ref[k * PLAN_T:(k + 1) * PLAN_T, :]
            for r in range(PLAN_T):
                row_copy(slot_ref, k, r, k).start()

    @pl.when(i == ns_p)
    def _():
        buf[0, :n_s] = hxs_ref[...]
        for r in range(n_s):
            row_copy(slots_ref, 0, r, 0).start()
        wait_block(n_s, 0)


def _dispatch(slot, hx_p, hx_s, n_slots):
    n_p, n_s = hx_p.shape[0], hx_s.shape[0]
    ns_p = n_p // (2 * PLAN_T)
    assert n_p == ns_p * 2 * PLAN_T and ns_p >= 1 and 0 < n_s <= PLAN_T
    slot_p = slot[:n_p].reshape(ns_p, 1, 2 * PLAN_T)
    slot_s = slot[n_p:n_p + PLAN_T].reshape(1, 1, PLAN_T)
    any_spec = pl.BlockSpec(memory_space=pl.ANY)
    last = ns_p - 1
    return pl.pallas_call(
        functools.partial(_dispatch_kernel, ns_p=ns_p, n_s=n_s),
        grid=(ns_p + 1,),
        in_specs=[pl.BlockSpec((1, 1, 2 * PLAN_T), lambda i: (jnp.minimum(i, last), 0, 0), memory_space=pltpu.SMEM),
                  pl.BlockSpec((1, 1, PLAN_T), lambda i: (0, 0, 0), memory_space=pltpu.SMEM),
                  pl.BlockSpec((2 * PLAN_T, HX_W), lambda i: (jnp.minimum(i, last), 0)),
                  pl.BlockSpec((n_s, HX_W), lambda i: (0, 0)), any_spec],
        out_specs=any_spec,
        out_shape=jax.ShapeDtypeStruct((n_slots, HX_W), F32),
        scratch_shapes=[pltpu.VMEM((2, PLAN_T, HX_W), F32), pltpu.SemaphoreType.DMA((2,))],
        input_output_aliases={4: 0},
        compiler_params=_params(("arbitrary",)),
        name="moe_dispatch",
    )(slot_p, slot_s, hx_p, hx_s, jnp.zeros((n_slots, HX_W), F32))


def _expert_kernel(ea_ref, eb_ref, nv_ref, xs_ref, wga_ref, wua_ref, wda_ref, wgb_ref, wub_ref, wdb_ref, ys_ref,
                   gu_scr, act_scr):
    del ea_ref, eb_ref

    @pl.when(pl.program_id(0) < nv_ref[0])
    def _():
        x = xs_ref[:, :D_MODEL].astype(BF16)
        for k, w_ref in enumerate((wga_ref, wua_ref, wgb_ref, wub_ref)):
            gu_scr[k] = jnp.dot(x, w_ref[0], preferred_element_type=F32)
        for k in range(2):
            g = gu_scr[2 * k]
            act_scr[k] = (g * jax.nn.sigmoid(g) * gu_scr[2 * k + 1]).astype(BF16)
        wa = xs_ref[:, D_MODEL:D_MODEL + 1]
        wb = xs_ref[:, D_MODEL + 1:D_MODEL + 2]
        ys_ref[...] = (jnp.dot(act_scr[0], wda_ref[0], preferred_element_type=F32) * wa
                       + jnp.dot(act_scr[1], wdb_ref[0], preferred_element_type=F32) * wb)

    @pl.when(pl.program_id(0) >= nv_ref[0])
    def _():
        ys_ref[...] = jnp.zeros_like(ys_ref)


def _class_experts():
    ea, eb = [], []
    for g in range(N_GROUPS):
        for a in range(EXPERTS_PER_GROUP):
            for b in range(a + 1, EXPERTS_PER_GROUP):
                ea.append(g * EXPERTS_PER_GROUP + a)
                eb.append(g * EXPERTS_PER_GROUP + b)
    return np.asarray(ea, np.int32), np.asarray(eb, np.int32)


def _experts(xs, pends, w_gate, w_up, w_down):
    n_tiles = xs.shape[0] // MOE_TM
    nv = (pends[N_CLASSES - 1] // MOE_TM).astype(jnp.int32)
    tile_cls = jnp.sum(pends[None, :N_CLASSES] <= (jnp.arange(n_tiles) * MOE_TM)[:, None], axis=1)
    tile_cls = jnp.minimum(tile_cls, N_CLASSES - 1)
    cls_ea, cls_eb = _class_experts()
    tile_ea = jnp.asarray(cls_ea)[tile_cls]
    tile_eb = jnp.asarray(cls_eb)[tile_cls]
    wa_spec = lambda shp: pl.BlockSpec((1,) + shp, lambda i, ea, eb, nv_: (ea[i], 0, 0))
    wb_spec = lambda shp: pl.BlockSpec((1,) + shp, lambda i, ea, eb, nv_: (eb[i], 0, 0))
    up, down = (D_MODEL, D_EXPERT), (D_EXPERT, D_MODEL)
    grid_spec = pltpu.PrefetchScalarGridSpec(
        num_scalar_prefetch=3,
        grid=(n_tiles,),
        in_specs=[pl.BlockSpec((MOE_TM, HX_W), lambda i, ea, eb, nv_: (i, 0)),
                  wa_spec(up), wa_spec(up), wa_spec(down), wb_spec(up), wb_spec(up), wb_spec(down)],
        out_specs=pl.BlockSpec((MOE_TM, D_MODEL), lambda i, ea, eb, nv_: (i, 0)),
        scratch_shapes=[pltpu.VMEM((4, MOE_TM, D_EXPERT), F32), pltpu.VMEM((2, MOE_TM, D_EXPERT), BF16)],
    )
    return pl.pallas_call(
        _expert_kernel,
        grid_spec=grid_spec,
        out_shape=jax.ShapeDtypeStruct((n_tiles * MOE_TM, D_MODEL), F32),
        compiler_params=_params(("arbitrary",)),
        name="moe_experts",
    )(tile_ea, tile_eb, nv.reshape(1), xs, w_gate, w_up, w_down, w_gate, w_up, w_down)


def _ple_kernel(slot_ref, slotn_ref, x1_ref, p_ref, wproj_ref, wgate_ref, gple_ref, gfin_ref, ys_hbm,
                y_ref, ybuf, sem, *, tm):
    i = pl.program_id(0)

    def row_copy(idx_ref, k, r, s):
        return pltpu.make_async_copy(ys_hbm.at[pl.ds(idx_ref[0, 0, k * tm + r], 1)], ybuf.at[s, pl.ds(r, 1)],
                                     sem.at[s])

    def for_rows(fn):
        def body(r, c):
            fn(r)
            return c
        lax.fori_loop(0, tm, body, 0, unroll=8)

    def wait_tile(s):
        for_rows(lambda r: row_copy(slot_ref, 0, 0, s).wait())

    def compute(k, s):
        rows = slice(k * tm, (k + 1) * tm)
        x2 = x1_ref[rows, :] + ybuf[s]
        proj = jnp.dot(p_ref[rows, :].astype(BF16), wproj_ref[...], preferred_element_type=F32)
        gate = jnp.dot(_rms(x2, gple_ref[...]).astype(BF16), wgate_ref[...], preferred_element_type=F32)
        x3 = x2 + proj * jax.nn.sigmoid(gate)
        y_ref[rows, :] = _rms(x3, gfin_ref[...])

    @pl.when(i == 0)
    def _():
        for_rows(lambda r: row_copy(slot_ref, 0, r, 0).start())

    wait_tile(0)
    for r in range(tm):
        row_copy(slot_ref, 1, r, 1).start()
    compute(0, 0)
    wait_tile(1)
    for r in range(tm):
        row_copy(slotn_ref, 0, r, 0).start()
    compute(1, 1)

    @pl.when(i == pl.num_programs(0) - 1)
    def _():
        wait_tile(0)


def _ple(x1, ys, slot, p2d, lw, g_final, *, tm):
    n = x1.shape[0]
    ns = n // (2 * tm)
    assert n == ns * 2 * tm
    slot3 = slot.reshape(ns, 1, 2 * tm)
    row = lambda i: (i, 0)
    const = lambda i: (0, 0)
    idx_spec = lambda f: pl.BlockSpec((1, 1, 2 * tm), f, memory_space=pltpu.SMEM)
    return pl.pallas_call(
        functools.partial(_ple_kernel, tm=tm),
        grid=(ns,),
        in_specs=[
            idx_spec(lambda i: (i, 0, 0)), idx_spec(lambda i: (jnp.minimum(i + 1, ns - 1), 0, 0)),
            pl.BlockSpec((2 * tm, D_MODEL), row), pl.BlockSpec((2 * tm, D_PLE), row),
            pl.BlockSpec((D_PLE, D_MODEL), const), pl.BlockSpec((D_MODEL, D_MODEL), const),
            pl.BlockSpec((1, D_MODEL), const), pl.BlockSpec((1, D_MODEL), const),
            pl.BlockSpec(memory_space=pl.ANY),
        ],
        out_specs=pl.BlockSpec((2 * tm, D_MODEL), row),
        out_shape=jax.ShapeDtypeStruct((n, D_MODEL), F32),
        scratch_shapes=[pltpu.VMEM((2, tm, D_MODEL), F32), pltpu.SemaphoreType.DMA((2,))],
        compiler_params=_params(("arbitrary",)),
        name="ple",
    )(slot3, slot3, x1, p2d, lw["w_ple_proj"], lw["w_ple_gate"], lw["g_ple"], g_final.reshape(1, D_MODEL), ys)


def _layer_weights(i, norm_mix_g, w_in, ret_norm_g, w_proj_a, w_proj_b, w_out, norm_ffn_g,
                   w_router_group, b_router_group, w_router_expert, b_router_expert,
                   w_gate_e, w_up_e, w_down_e, norm_ple_g, w_ple_gate, w_ple_proj):
    w = w_in[i]
    o = _IN_OFFS
    cols = lambda k: w[:, o[k]:o[k + 1]]
    w_main = jnp.concatenate([cols(0), cols(1), cols(2), cols(3), cols(5), cols(6)], axis=1).astype(BF16)
    wr = jnp.zeros((ROUTER_ROWS, D_MODEL), F32)
    wr = wr.at[:N_GROUPS].set(w_router_group[i].T).at[8:].set(w_router_expert[i].T)
    br = jnp.full((ROUTER_ROWS,), NEG_BIG, F32)
    br = br.at[:N_GROUPS].set(b_router_group[i].astype(F32)).at[8:].set(b_router_expert[i].astype(F32))
    wr_hi, wr_lo = _split_bf16(wr)
    return {
        "g_mix": norm_mix_g[i].reshape(1, D_MODEL), "w_main": w_main, "w_kt": cols(4).T.astype(BF16),
        "w_qvt": jnp.concatenate([cols(0).T, cols(2).T], axis=0).astype(BF16),
        "w_ga": cols(7).astype(BF16), "w_gb": cols(8).astype(BF16),
        "ret_norm_g": ret_norm_g[i], "w_pa": w_proj_a[i].astype(BF16), "w_pb": w_proj_b[i].astype(BF16),
        "w_out": w_out[i].astype(BF16), "g_ffn": norm_ffn_g[i].reshape(1, D_MODEL),
        "wr_hi": wr_hi, "wr_lo": wr_lo, "b_r": br.reshape(ROUTER_ROWS, 1),
        "w_gate": w_gate_e[i].astype(BF16), "w_up": w_up_e[i].astype(BF16), "w_down": w_down_e[i].astype(BF16),
        "g_ple": norm_ple_g[i].reshape(1, D_MODEL), "w_ple_gate": w_ple_gate[i].astype(BF16),
        "w_ple_proj": w_ple_proj[i].astype(BF16),
    }


def _moe(hx_p, cls_p, hx_s, cls_s, lw):
    n_p, n_s = hx_p.shape[0], hx_s.shape[0]
    n_real = n_p + n_s
    n_pad = (n_p // PLAN_T + 1) * PLAN_T
    cls_all = jnp.concatenate([cls_p, cls_s, jnp.zeros((8, n_pad - n_real), jnp.int32)], axis=1)
    slot, pends = _routing_plan(cls_all, n_real)
    n_slots = (pl.cdiv(n_real, MOE_TM) + N_CLASSES) * MOE_TM
    xs = _dispatch(slot, hx_p, hx_s, n_slots)
    ys = _experts(xs, pends, lw["w_gate"], lw["w_up"], lw["w_down"])
    return ys, slot[:n_p], slot[n_p:n_real]


def kernel(x_prompt, x_sample, cache_k_a, cache_v_a, state_ret, p_prompt, p_sample, norm_mix_g, w_in, rel_bias, ret_norm_g, w_proj_a, w_proj_b, w_out, norm_ffn_g, w_router_group, b_router_group, w_router_expert, b_router_expert, w_gate_e, w_up_e, w_down_e, norm_ple_g, w_ple_gate, w_ple_proj, final_norm_g):
    depth = w_in.shape[0]
    assert depth == 1, "the final norm is fused into the last layer; deeper stacks are not supported"
    bp, sp, _ = x_prompt.shape
    bs, ss, _ = x_sample.shape
    keep = min(WINDOW_A, sp)
    n_cache = cache_k_a.shape[2]
    log_g = jnp.log(1.0 - 2.0 ** (-5.0 - jnp.arange(N_HEADS_B, dtype=F32)))
    i = 0
    lw = _layer_weights(i, norm_mix_g, w_in, ret_norm_g, w_proj_a, w_proj_b, w_out, norm_ffn_g,
                        w_router_group, b_router_group, w_router_expert, b_router_expert,
                        w_gate_e, w_up_e, w_down_e, norm_ple_g, w_ple_gate, w_ple_proj)

    tm = 512
    assert sp % tm == 0 and keep == tm and sp >= ATT_WIN
    t_ret = 128
    xp2 = x_prompt.reshape(bp * sp, D_MODEL)
    qa, ka, va, qb, kbt, vb, gb, ka32, va32 = _project(
        xp2, jnp.arange(sp), lw["g_mix"], lw["w_main"], lw["w_kt"], lw["w_qvt"], tm=tm,
        tiles_per_keep=sp // tm, feature_major_qv=True)
    r3 = lambda a: a.reshape(bp, sp, a.shape[-1])
    att = _attention_prompt(qa, r3(ka), va, rel_bias[i], bp)
    per_tile = tm // t_ret
    tiles_per_b = sp // tm
    b_in, s_prompt = _retention(
        r3(qb), kbt, r3(vb), r3(gb), jnp.zeros((bp, N_HEADS_B, DK_B, DV_B), F32), log_g, lw["ret_norm_g"],
        t=t_ret, kt_index=lambda bi, c: (bi * tiles_per_b + c // per_tile, 0, c % per_tile))
    x1_p, hx_p, cls_p = _merge(xp2, att.reshape(bp * sp, W_A), b_in.reshape(bp * sp, V_B), lw, tm=tm)
    k_a_prompt = ka32.reshape(bp, keep, N_HEADS_A, HEAD_DIM_A)
    v_a_prompt = va32.reshape(bp, keep, N_HEADS_A, HEAD_DIM_A)

    ns = bs * ss
    xs2 = x_sample.reshape(ns, D_MODEL)
    pos_s = jnp.tile(PAST_LEN + jnp.arange(ss), bs)
    qa, ka, va, qb, kbt, vb, gb, ka32, va32 = _project(
        xs2, pos_s, lw["g_mix"], lw["w_main"], lw["w_kt"], lw["w_qvt"], tm=ns, tiles_per_keep=1,
        feature_major_qv=False)
    r3 = lambda a: a.reshape(bs, ss, a.shape[-1])
    k_all = jnp.concatenate([cache_k_a[i].reshape(bs, n_cache, W_A).astype(BF16), r3(ka)], axis=1)
    v_all = jnp.concatenate([cache_v_a[i].reshape(bs, n_cache, W_A).astype(BF16), r3(va)], axis=1)
    att = _attention_sample(r3(qa), k_all, v_all, rel_bias[i], n_cache)
    kbt_s = kbt.reshape(QK_B, bs, ss).transpose(1, 0, 2)
    b_in, s_sample = _retention(
        r3(qb), kbt_s, r3(vb), r3(gb), state_ret[i].astype(F32), log_g, lw["ret_norm_g"],
        t=ss, kt_index=lambda bi, c: (bi, 0, 0))
    x1_s, hx_s, cls_s = _merge(xs2, att.reshape(ns, W_A), b_in.reshape(ns, V_B), lw, tm=ns)

    ys, slot_p, slot_s = _moe(hx_p, cls_p, hx_s, cls_s, lw)
    y_prompt = _ple(x1_p, ys, slot_p, p_prompt[i].reshape(bp * sp, D_PLE), lw, final_norm_g, tm=tm)
    y_sample = _ple(x1_s, ys, slot_s, p_sample[i].reshape(ns, D_PLE), lw, final_norm_g, tm=ns // 2)
    k_a_sample = ka32.reshape(bs, ss, N_HEADS_A, HEAD_DIM_A)
    v_a_sample = va32.reshape(bs, ss, N_HEADS_A, HEAD_DIM_A)

    return (y_prompt.reshape(bp, sp, D_MODEL), y_sample.reshape(bs, ss, D_MODEL),
            k_a_prompt[None], v_a_prompt[None], s_prompt[None],
            k_a_sample[None], v_a_sample[None], s_sample.astype(state_ret.dtype)[None])
```

```python
import functools

import numpy as np
import jax
import jax.numpy as jnp
from jax import lax
from jax.experimental import pallas as pl
from jax.experimental.pallas import tpu as pltpu

F32 = jnp.float32
BF16 = jnp.bfloat16

D_MODEL = 1024
PAST_LEN = 1024
CHUNK = 64
BAND_CHUNKS = 8
WINDOW_A = BAND_CHUNKS * CHUNK
N_HEADS_A = 8
HEAD_DIM_A = 64
W_A = N_HEADS_A * HEAD_DIM_A
REL_CLIP = 128
N_HEADS_B = 4
DK_B = 128
DV_B = 256
QK_B = N_HEADS_B * DK_B
V_B = N_HEADS_B * DV_B
ROPE_BASE = 10000.0
N_GROUPS = 4
EXPERTS_PER_GROUP = 8
N_EXPERTS = N_GROUPS * EXPERTS_PER_GROUP
TOP_K = 2
D_EXPERT = 512
D_PLE = 256
EPS = 1e-6
_IN_SIZES = (W_A, W_A, W_A, QK_B, QK_B, V_B, V_B, D_MODEL, D_MODEL)
_IN_OFFS = tuple(sum(_IN_SIZES[:i]) for i in range(len(_IN_SIZES) + 1))

LANES = 128
ATT_QBLK = 2 * CHUNK
ATT_WIN = (BAND_CHUNKS + 2) * CHUNK
ROUTER_ROWS = 8 + N_EXPERTS
NEG_BIG = -1e30
PAIRS_PER_GROUP = EXPERTS_PER_GROUP * (EXPERTS_PER_GROUP - 1) // 2
N_CLASSES = N_GROUPS * PAIRS_PER_GROUP
CLASS_ROWS = 128
HX_W = D_MODEL + LANES
PLAN_T = 512
RANK_T = 1024
MOE_TM = 256
VMEM_LIMIT = 56 * 1024 * 1024


def _params(sem):
    return pltpu.CompilerParams(dimension_semantics=sem, vmem_limit_bytes=VMEM_LIMIT)


def _rms(x, g):
    return x * lax.rsqrt(jnp.mean(x * x, axis=-1, keepdims=True) + EPS) * g


def _proj_kernel(x_ref, g_ref, w_ref, wkt_ref, wqvt_ref, cos_ref, sin_ref, cost_ref, sint_ref,
                 qa_ref, ka_ref, va_ref, qb_ref, kbt_ref, vb_ref, gb_ref, ka32_ref, va32_ref,
                 *, tiles_per_keep, feature_major_qv):
    h = _rms(x_ref[...], g_ref[...]).astype(BF16)
    nt_dims = (((1,), (1,)), ((), ()))

    def seg(lo, hi):
        return jnp.dot(h, w_ref[:, lo:hi], preferred_element_type=F32)

    ka = seg(W_A, 2 * W_A)
    ka_ref[...] = ka.astype(BF16)
    q_scale = HEAD_DIM_A ** -0.5
    if feature_major_qv:
        qvt = lax.dot_general(wqvt_ref[...], h, nt_dims, preferred_element_type=F32)
        for c in range(qa_ref.shape[0]):
            cs = slice(c * LANES, (c + 1) * LANES)
            qa_ref[c] = (qvt[:W_A, cs] * q_scale).astype(BF16)
            va_ref[c] = qvt[W_A:, cs].astype(BF16)
    else:
        qa_ref[...] = (seg(0, W_A) * q_scale).astype(BF16)
        va_ref[...] = seg(2 * W_A, 3 * W_A).astype(BF16)

    @pl.when(pl.program_id(0) % tiles_per_keep == tiles_per_keep - 1)
    def _():
        ka32_ref[...] = ka
        va32_ref[...] = seg(2 * W_A, 3 * W_A)

    qb = seg(3 * W_A, 3 * W_A + QK_B)
    cos = cos_ref[...]
    sin = sin_ref[...]
    for hd in range(N_HEADS_B):
        xh = qb[:, hd * DK_B:(hd + 1) * DK_B]
        qb_ref[:, hd * DK_B:(hd + 1) * DK_B] = (xh * cos + pltpu.roll(xh, DK_B // 2, axis=1) * sin).astype(BF16)

    vb_ref[...] = seg(3 * W_A + QK_B, 3 * W_A + QK_B + V_B).astype(BF16)
    gb_ref[...] = seg(3 * W_A + QK_B + V_B, 3 * W_A + QK_B + 2 * V_B)

    kt = lax.dot_general(wkt_ref[...], h, (((1,), (1,)), ((), ())), preferred_element_type=F32)
    cost = cost_ref[...]
    sint = sint_ref[...]
    half = DK_B // 2
    scale = DK_B ** -0.5
    for hd in range(N_HEADS_B):
        x1 = kt[hd * DK_B:hd * DK_B + half, :]
        x2 = kt[hd * DK_B + half:(hd + 1) * DK_B, :]
        kbt_ref[0, hd * DK_B:hd * DK_B + half, :] = (x1 * cost - x2 * sint) * scale
        kbt_ref[0, hd * DK_B + half:(hd + 1) * DK_B, :] = (x2 * cost + x1 * sint) * scale


def _rope_tables(pos):
    half = DK_B // 2
    freqs = ROPE_BASE ** (-jnp.arange(half, dtype=F32) / half)
    ang = pos.astype(F32)[:, None] * freqs[None, :]
    cos = jnp.cos(ang)
    sin = jnp.sin(ang)
    return (jnp.concatenate([cos, cos], axis=1), jnp.concatenate([-sin, sin], axis=1), cos.T, sin.T)


def _project(x2d, pos_rows, g_norm, w_main, w_kt, w_qvt, *, tm, tiles_per_keep, feature_major_qv):
    n = x2d.shape[0]
    period = pos_rows.shape[0]
    nt = n // tm
    ppt = period // tm
    cos2, sin2, cost, sint = _rope_tables(pos_rows)
    n_keep = n // tiles_per_keep
    row = lambda i: (i, 0)
    const = lambda i: (0, 0)
    if feature_major_qv:
        spt = tm // LANES
        qv_shape = jax.ShapeDtypeStruct((n // LANES, W_A, LANES), BF16)
        qv_spec = pl.BlockSpec((spt, W_A, LANES), lambda i: (i, 0, 0))
    else:
        qv_shape = jax.ShapeDtypeStruct((n, W_A), BF16)
        qv_spec = pl.BlockSpec((tm, W_A), row)
    outs = (
        qv_shape, jax.ShapeDtypeStruct((n, W_A), BF16), qv_shape, jax.ShapeDtypeStruct((n, QK_B), BF16),
        jax.ShapeDtypeStruct((nt, QK_B, tm), F32), jax.ShapeDtypeStruct((n, V_B), BF16),
        jax.ShapeDtypeStruct((n, V_B), F32),
        jax.ShapeDtypeStruct((n_keep, W_A), F32), jax.ShapeDtypeStruct((n_keep, W_A), F32),
    )
    keep_spec = pl.BlockSpec((tm, W_A), lambda i: (i // tiles_per_keep, 0))
    return pl.pallas_call(
        functools.partial(_proj_kernel, tiles_per_keep=tiles_per_keep, feature_major_qv=feature_major_qv),
        grid=(nt,),
        in_specs=[
            pl.BlockSpec((tm, D_MODEL), row),
            pl.BlockSpec((1, D_MODEL), const),
            pl.BlockSpec(w_main.shape, const),
            pl.BlockSpec(w_kt.shape, const),
            pl.BlockSpec(w_qvt.shape, const),
            pl.BlockSpec((tm, DK_B), lambda i: (i % ppt, 0)),
            pl.BlockSpec((tm, DK_B), lambda i: (i % ppt, 0)),
            pl.BlockSpec((DK_B // 2, tm), lambda i: (0, i % ppt)),
            pl.BlockSpec((DK_B // 2, tm), lambda i: (0, i % ppt)),
        ],
        out_specs=(
            qv_spec, pl.BlockSpec((tm, W_A), row), qv_spec,
            pl.BlockSpec((tm, QK_B), row), pl.BlockSpec((1, QK_B, tm), lambda i: (i, 0, 0)),
            pl.BlockSpec((tm, V_B), row), pl.BlockSpec((tm, V_B), row), keep_spec, keep_spec,
        ),
        out_shape=outs,
        compiler_params=_params(("arbitrary",)),
        name="proj",
    )(x2d, g_norm.reshape(1, D_MODEL), w_main, w_kt, w_qvt, cos2, sin2, cost, sint)


def _attend_pairs(q_of, k_of, v_of, bias_of, store):
    for hp in range(N_HEADS_A // 2):
        qp = q_of(hp)
        kw = k_of(hp)
        vw = v_of(hp)
        lane = lax.broadcasted_iota(jnp.int32, qp.shape, 1)
        outs = []
        for hh in range(2):
            in_head = (lane >= hh * HEAD_DIM_A) & (lane < (hh + 1) * HEAD_DIM_A)
            qh = jnp.where(in_head, qp, jnp.zeros_like(qp))
            s = lax.dot_general(qh, kw, (((1,), (1,)), ((), ())), preferred_element_type=F32)
            s = s + bias_of(2 * hp + hh)
            m = jnp.max(s, axis=-1, keepdims=True)
            p = jnp.exp(s - m)
            l = jnp.sum(p, axis=-1, keepdims=True)
            o = jnp.dot(p.astype(BF16), vw, preferred_element_type=F32)
            outs.append(o / l)
        lane_o = lax.broadcasted_iota(jnp.int32, outs[0].shape, 1)
        store(hp, jnp.where(lane_o < HEAD_DIM_A, outs[0], outs[1]).astype(BF16))


def _fold_rows(x, op, reduce_rows):
    r = x.shape[0]
    while r % 16 == 0:
        r //= 2
        x = op(x[:r], x[r:])
    parts = [x[a:a + 8] for a in range(0, r, 8)]
    while len(parts) > 1:
        parts = [op(parts[a], parts[a + 1]) if a + 1 < len(parts) else parts[a] for a in range(0, len(parts), 2)]
    return reduce_rows(parts[0], axis=0, keepdims=True)


def _attn_prompt_kernel(qt_ref, k_ref, vt_ref, bias_ref, o_ref, s_scr, p_scr):
    j = pl.program_id(1)
    first = jnp.maximum(j - BAND_CHUNKS // 2, 0)
    start = pl.multiple_of(first * ATT_QBLK, ATT_QBLK)
    n_slab = ATT_WIN // LANES
    n_pairs = N_HEADS_A // 2

    def scores(hp):
        rows = slice(hp * LANES, (hp + 1) * LANES)
        qt = qt_ref[0, rows, :]
        dim = lax.broadcasted_iota(jnp.int32, qt.shape, 0)
        zero = jnp.zeros_like(qt)
        w = jnp.concatenate([jnp.where(dim < HEAD_DIM_A, qt, zero), jnp.where(dim >= HEAD_DIM_A, qt, zero)], axis=1)
        kw = k_ref[0, pl.ds(start, ATT_WIN), rows]
        return jnp.dot(kw, w, preferred_element_type=F32)

    for hp in range(n_pairs):
        s_scr[hp] = scores(hp) + bias_ref[0, hp]
    denom = []
    for hp in range(n_pairs):
        s = s_scr[hp]
        m = _fold_rows(s, jnp.maximum, jnp.max)
        p = jnp.exp(s - m)
        denom.append(_fold_rows(p, jnp.add, jnp.sum))
        p_scr[hp] = p.astype(BF16)
    for hp in range(n_pairs):
        rows = slice(hp * LANES, (hp + 1) * LANES)
        vt = jnp.concatenate([vt_ref[first + c, rows, :] for c in range(n_slab)], axis=1)
        ot = jnp.dot(vt, p_scr[hp], preferred_element_type=F32) / denom[hp]
        odim = lax.broadcasted_iota(jnp.int32, (LANES, LANES), 0)
        o_pair_t = jnp.where(odim < HEAD_DIM_A, ot[:, :LANES], ot[:, LANES:])
        o_ref[0, :, rows] = o_pair_t.T.astype(BF16)


def _band_bias(table):
    i = np.arange(ATT_QBLK)[None, :]
    jk = np.arange(ATT_WIN)[:, None]
    out = []
    for v in range(BAND_CHUNKS // 2 + 1):
        off_chunks = 2 * v if v < BAND_CHUNKS // 2 else BAND_CHUNKS
        dchunk = (off_chunks + i // CHUNK) - jk // CHUNK
        valid = (dchunk >= 0) & (dchunk <= BAND_CHUNKS)
        n_f = ATT_WIN + ATT_QBLK - 1
        idx = np.clip(off_chunks * CHUNK + (ATT_QBLK - 1) - np.arange(n_f + 1), -REL_CLIP, REL_CLIP) + REL_CLIP
        g = table[:, idx].astype(F32)
        rows = jnp.tile(g, (1, ATT_QBLK))[:, :ATT_QBLK * n_f].reshape(N_HEADS_A, ATT_QBLK, n_f)
        b = rows[:, :, ATT_QBLK - 1:].transpose(0, 2, 1)
        b = jnp.where(valid[None], b, jnp.float32(NEG_BIG))
        b = b.reshape(N_HEADS_A // 2, 2, ATT_WIN, ATT_QBLK).transpose(0, 2, 1, 3)
        out.append(b.reshape(N_HEADS_A // 2, ATT_WIN, 2 * ATT_QBLK))
    return jnp.stack(out)


def _attention_prompt(qat, ka, vat, table, b):
    s = ka.shape[1]
    nq = s // ATT_QBLK
    bias = _band_bias(table)
    nvar = bias.shape[0]
    return pl.pallas_call(
        _attn_prompt_kernel,
        grid=(b, nq),
        in_specs=[
            pl.BlockSpec((1, W_A, LANES), lambda bi, j: (bi * nq + j, 0, 0)),
            pl.BlockSpec((1, s, W_A), lambda bi, j: (bi, 0, 0)),
            pl.BlockSpec((nq, W_A, LANES), lambda bi, j: (bi, 0, 0)),
            pl.BlockSpec((1, N_HEADS_A // 2, ATT_WIN, 2 * ATT_QBLK),
                         lambda bi, j: (jnp.minimum(j, nvar - 1), 0, 0, 0)),
        ],
        out_specs=pl.BlockSpec((1, ATT_QBLK, W_A), lambda bi, j: (bi, j, 0)),
        out_shape=jax.ShapeDtypeStruct((b, s, W_A), BF16),
        scratch_shapes=[pltpu.VMEM((N_HEADS_A // 2, ATT_WIN, 2 * ATT_QBLK), F32),
                        pltpu.VMEM((N_HEADS_A // 2, ATT_WIN, 2 * ATT_QBLK), BF16)],
        compiler_params=_params(("arbitrary", "arbitrary")),
        name="attn_prompt",
    )(qat, ka, vat, bias)


def _attn_sample_kernel(q_ref, k_ref, v_ref, bias_ref, o_ref):
    def sl(hp):
        return slice(hp * LANES, (hp + 1) * LANES)

    def store(hp, val):
        o_ref[0, :, sl(hp)] = val

    _attend_pairs(
        lambda hp: q_ref[0, :, sl(hp)],
        lambda hp: k_ref[0, :, sl(hp)],
        lambda hp: v_ref[0, :, sl(hp)],
        lambda hd: bias_ref[hd],
        store)


def _attention_sample(qa, k_all, v_all, table, n_cache):
    b, n, _ = qa.shape
    nk = k_all.shape[1]
    dist = jnp.arange(n)[:, None] + n_cache - jnp.arange(nk)[None, :]
    bias = table[:, jnp.clip(dist, -REL_CLIP, REL_CLIP) + REL_CLIP].astype(F32)
    return pl.pallas_call(
        _attn_sample_kernel,
        grid=(b,),
        in_specs=[
            pl.BlockSpec((1, n, W_A), lambda bi: (bi, 0, 0)),
            pl.BlockSpec((1, nk, W_A), lambda bi: (bi, 0, 0)),
            pl.BlockSpec((1, nk, W_A), lambda bi: (bi, 0, 0)),
            pl.BlockSpec((N_HEADS_A, n, nk), lambda bi: (0, 0, 0)),
        ],
        out_specs=pl.BlockSpec((1, n, W_A), lambda bi: (bi, 0, 0)),
        out_shape=jax.ShapeDtypeStruct((b, n, W_A), BF16),
        compiler_params=_params(("arbitrary",)),
        name="attn_sample",
    )(qa, k_all, v_all, bias)


def _ret_kernel(gt_ref, q_ref, *refs, nb):
    kt_refs = refs[:nb]
    v_ref, gb_ref, s0_ref, dmask_ref, qd_ref, kd_ref, gn_ref, out_ref, state_ref, s_scr, o_scr = refs[nb:]

    @pl.when(pl.program_id(1) == 0)
    def _():
        state_ref[...] = s0_ref[...]

    units = [(bb, hd) for bb in range(nb) for hd in range(N_HEADS_B)]
    qs = lambda hd: slice(hd * DK_B, (hd + 1) * DK_B)
    vs = lambda hd: slice(hd * DV_B, (hd + 1) * DV_B)

    for u, (bb, hd) in enumerate(units):
        q = q_ref[bb, :, qs(hd)]
        kt = kt_refs[bb][0, qs(hd), :]
        s_scr[u] = (jnp.dot(q, kt.astype(BF16), preferred_element_type=F32) * dmask_ref[hd]).astype(BF16)
        o_scr[u] = jnp.dot(q, state_ref[bb, hd].astype(BF16), preferred_element_type=F32) * qd_ref[hd]
    for u, (bb, hd) in enumerate(units):
        v = v_ref[bb, :, vs(hd)]
        o_scr[u] = o_scr[u] + jnp.dot(s_scr[u], v, preferred_element_type=F32)
        kd = (kt_refs[bb][0, qs(hd), :] * kd_ref[hd]).astype(BF16)
        state_ref[bb, hd] = state_ref[bb, hd] * gt_ref[hd] + jnp.dot(kd, v, preferred_element_type=F32)
    for u, (bb, hd) in enumerate(units):
        o = o_scr[u]
        mu = jnp.mean(o, axis=-1, keepdims=True)
        var = jnp.mean(jnp.square(o - mu), axis=-1, keepdims=True)
        rb = (o - mu) * lax.rsqrt(var + EPS) * gn_ref[:, vs(hd)]
        gb = gb_ref[bb, :, vs(hd)]
        out_ref[bb, :, vs(hd)] = (gb * jax.nn.sigmoid(gb) * rb).astype(BF16)


def _retention(qb, kbt, vb, gb, state0, log_g, ret_norm_g, *, t, kt_index, nb=4):
    b, s, _ = qb.shape
    assert b % nb == 0
    nc = s // t
    idx = jnp.arange(t, dtype=F32)
    diff = idx[:, None] - idx[None, :]
    dmask = jnp.where(diff[None] >= 0, jnp.exp(log_g[:, None, None] * jnp.maximum(diff, 0.0)[None]), 0.0)
    q_decay = jnp.exp(log_g[:, None] * (idx[None, :] + 1.0))
    k_decay = jnp.exp(log_g[:, None] * (t - 1.0 - idx[None, :]))
    g_t = jnp.exp(log_g * t)
    qd = jnp.broadcast_to(q_decay[:, :, None], (N_HEADS_B, t, DV_B))
    kd = k_decay[:, None, :]
    const3 = lambda bi, c: (0, 0, 0)
    seq = lambda w: pl.BlockSpec((nb, t, w), lambda bi, c: (bi, c, 0))
    state_spec = pl.BlockSpec((nb, N_HEADS_B, DK_B, DV_B), lambda bi, c: (bi, 0, 0, 0))
    kt_specs = [pl.BlockSpec((1, QK_B, t), functools.partial(lambda bi, c, k: kt_index(nb * bi + k, c), k=k))
                for k in range(nb)]
    return pl.pallas_call(
        functools.partial(_ret_kernel, nb=nb),
        grid=(b // nb, nc),
        in_specs=[pl.BlockSpec(memory_space=pltpu.SMEM), seq(QK_B)] + kt_specs + [
            seq(V_B), seq(V_B), state_spec,
            pl.BlockSpec((N_HEADS_B, t, t), const3),
            pl.BlockSpec((N_HEADS_B, t, DV_B), const3),
            pl.BlockSpec((N_HEADS_B, 1, t), const3),
            pl.BlockSpec((1, V_B), lambda bi, c: (0, 0)),
        ],
        out_specs=(seq(V_B), state_spec),
        out_shape=(jax.ShapeDtypeStruct((b, s, V_B), BF16),
                   jax.ShapeDtypeStruct((b, N_HEADS_B, DK_B, DV_B), F32)),
        scratch_shapes=[pltpu.VMEM((nb * N_HEADS_B, t, t), BF16), pltpu.VMEM((nb * N_HEADS_B, t, DV_B), F32)],
        compiler_params=_params(("arbitrary", "arbitrary")),
        name="retention",
    )(g_t, qb, *([kbt] * nb), vb, gb, state0, dmask, qd, kd, ret_norm_g.reshape(1, V_B))


def _split_bf16(x):
    hi = x.astype(BF16)
    lo = (x - hi.astype(F32)).astype(BF16)
    return hi, lo


def _merge_kernel(x_ref, att_ref, bin_ref, gmix_ref, wga_ref, wgb_ref, wpa_ref, wpb_ref, wout_ref,
                  gffn_ref, wr_hi_ref, wr_lo_ref, br_ref, x1_ref, hx_ref, cls_ref):
    x = x_ref[...]
    h = _rms(x, gmix_ref[...]).astype(BF16)
    gate_a = jnp.dot(h, wga_ref[...], preferred_element_type=F32)
    gate_b = jnp.dot(h, wgb_ref[...], preferred_element_type=F32)
    a = jnp.dot(att_ref[...], wpa_ref[...], preferred_element_type=F32)
    b = jnp.dot(bin_ref[...], wpb_ref[...], preferred_element_type=F32)
    m = jax.nn.sigmoid(gate_a) * a + jax.nn.sigmoid(gate_b) * b
    x1 = x + jnp.dot(m.astype(BF16), wout_ref[...], preferred_element_type=F32)
    x1_ref[...] = x1
    h2 = _rms(x1, gffn_ref[...])
    hx_ref[:, :D_MODEL] = h2

    h_hi, h_lo = _split_bf16(h2)
    nt = (((1,), (1,)), ((), ()))
    lt = (lax.dot_general(wr_hi_ref[...], h_hi, nt, preferred_element_type=F32)
          + lax.dot_general(wr_hi_ref[...], h_lo, nt, preferred_element_type=F32)
          + lax.dot_general(wr_lo_ref[...], h_hi, nt, preferred_element_type=F32)) + br_ref[...]
    tm = lt.shape[1]
    row = lax.broadcasted_iota(jnp.int32, (8, tm), 0)
    lg = lt[0:8, :]
    mg = jnp.max(lg, axis=0, keepdims=True)
    grp = jnp.min(jnp.where(lg == mg, row, 8), axis=0, keepdims=True)
    p_grp = 1.0 / jnp.sum(jnp.exp(lg - mg), axis=0, keepdims=True)
    le = jnp.zeros((8, tm), F32)
    for g in range(N_GROUPS):
        le = jnp.where(grp == g, lt[8 + 8 * g:16 + 8 * g, :], le)
    m0 = jnp.max(le, axis=0, keepdims=True)
    i0 = jnp.min(jnp.where(le == m0, row, 8), axis=0, keepdims=True)
    rest = jnp.where(row == i0, jnp.float32(-jnp.inf), le)
    m1 = jnp.max(rest, axis=0, keepdims=True)
    i1 = jnp.min(jnp.where(rest == m1, row, 8), axis=0, keepdims=True)
    e = jnp.exp(m1 - m0)
    w0 = (1.0 / (1.0 + e)) * p_grp
    w1 = (e / (1.0 + e)) * p_grp
    ea = jnp.minimum(i0, i1)
    eb = jnp.maximum(i0, i1)
    pair = ((ea * (2 * EXPERTS_PER_GROUP - 1 - ea)) >> 1) + (eb - ea - 1)
    cls_ref[...] = jnp.where(row == 0, grp * PAIRS_PER_GROUP + pair, 0)
    wa = jnp.where(i0 < i1, w0, w1)
    wb = jnp.where(i0 < i1, w1, w0)
    wrow = lax.broadcasted_iota(jnp.int32, (LANES, tm), 0)
    wslab = jnp.where(wrow == 0, wa, jnp.where(wrow == 1, wb, 0.0))
    hx_ref[:, D_MODEL:] = wslab.T


def _merge(x2d, att, b_in, lw, *, tm):
    n = x2d.shape[0]
    row = lambda i: (i, 0)
    const = lambda i: (0, 0)
    full = lambda a: pl.BlockSpec(a.shape, const)
    return pl.pallas_call(
        _merge_kernel,
        grid=(n // tm,),
        in_specs=[
            pl.BlockSpec((tm, D_MODEL), row), pl.BlockSpec((tm, W_A), row), pl.BlockSpec((tm, V_B), row),
            full(lw["g_mix"]), full(lw["w_ga"]), full(lw["w_gb"]), full(lw["w_pa"]), full(lw["w_pb"]),
            full(lw["w_out"]), full(lw["g_ffn"]), full(lw["wr_hi"]), full(lw["wr_lo"]), full(lw["b_r"]),
        ],
        out_specs=(pl.BlockSpec((tm, D_MODEL), row), pl.BlockSpec((tm, HX_W), row),
                   pl.BlockSpec((8, tm), lambda i: (0, i))),
        out_shape=(jax.ShapeDtypeStruct((n, D_MODEL), F32), jax.ShapeDtypeStruct((n, HX_W), F32),
                   jax.ShapeDtypeStruct((8, n), jnp.int32)),
        compiler_params=_params(("arbitrary",)),
        name="merge",
    )(x2d, att, b_in, lw["g_mix"], lw["w_ga"], lw["w_gb"], lw["w_pa"], lw["w_pb"], lw["w_out"],
      lw["g_ffn"], lw["wr_hi"], lw["wr_lo"], lw["b_r"])


def _class_onehot(cls_row, base, n_real):
    t = cls_row.shape[1]
    crow = lax.broadcasted_iota(jnp.int32, (CLASS_ROWS, t), 0)
    tok = base + lax.broadcasted_iota(jnp.int32, (CLASS_ROWS, t), 1)
    return (cls_row == crow) & (tok < n_real)


def _rank_kernel(cls_ref, tri_ref, rank_ref, counts_ref, *, n_real):
    i = pl.program_id(0)

    @pl.when(i == 0)
    def _():
        counts_ref[...] = jnp.zeros_like(counts_ref)

    t = tri_ref.shape[0]
    hot = _class_onehot(cls_ref[0:1, :], i * t, n_real)
    incl = jnp.dot(jnp.where(hot, 1.0, 0.0).astype(BF16), tri_ref[...], preferred_element_type=F32)
    carry = counts_ref[:, 0:1]
    rank = jnp.sum(jnp.where(hot, incl + carry, 0.0), axis=0, keepdims=True) - 1.0
    rank_ref[...] = jnp.broadcast_to(rank, rank_ref.shape).astype(jnp.int32)
    counts_ref[...] = counts_ref[...] + incl[:, t - 1:t]


def _slot_kernel(cls_ref, rank_ref, pstart_ref, slot_ref, *, n_real):
    t = cls_ref.shape[1]
    hot = _class_onehot(cls_ref[0:1, :], pl.program_id(0) * t, n_real)
    start = jnp.sum(jnp.where(hot, pstart_ref[...], 0.0), axis=0, keepdims=True)
    slot_ref[...] = jnp.broadcast_to(start.astype(jnp.int32) + rank_ref[0:1, :], slot_ref.shape)


def _routing_plan(cls_all, n_real):
    n_pad = cls_all.shape[1]
    nblk = n_pad // RANK_T
    tri = jnp.asarray(np.triu(np.ones((RANK_T, RANK_T), np.float32)), BF16)
    tok = lambda i: (0, i)
    const = lambda i: (0, 0)
    rank, counts = pl.pallas_call(
        functools.partial(_rank_kernel, n_real=n_real),
        grid=(nblk,),
        in_specs=[pl.BlockSpec((8, RANK_T), tok), pl.BlockSpec((RANK_T, RANK_T), const)],
        out_specs=(pl.BlockSpec((8, RANK_T), tok), pl.BlockSpec((CLASS_ROWS, LANES), const)),
        out_shape=(jax.ShapeDtypeStruct((8, n_pad), jnp.int32), jax.ShapeDtypeStruct((CLASS_ROWS, LANES), F32)),
        compiler_params=_params(("arbitrary",)),
        name="moe_rank",
    )(cls_all, tri)
    counts = counts[:, 0].astype(jnp.int32)
    psizes = ((counts + MOE_TM - 1) // MOE_TM) * MOE_TM
    pends = jnp.cumsum(psizes)
    pstart = (pends - psizes).astype(F32).reshape(CLASS_ROWS, 1)
    slot = pl.pallas_call(
        functools.partial(_slot_kernel, n_real=n_real),
        grid=(nblk,),
        in_specs=[pl.BlockSpec((8, RANK_T), tok), pl.BlockSpec((8, RANK_T), tok),
                  pl.BlockSpec((CLASS_ROWS, 1), const)],
        out_specs=pl.BlockSpec((8, RANK_T), tok),
        out_shape=jax.ShapeDtypeStruct((8, n_pad), jnp.int32),
        compiler_params=_params(("arbitrary",)),
        name="moe_slot",
    )(cls_all, rank, pstart)
    return slot[0], pends


def _dispatch_kernel(slot_ref, slots_ref, hxp_ref, hxs_ref, xs_in_hbm, xs_hbm, buf, sem, *, ns_p, n_s):
    del xs_in_hbm
    i = pl.program_id(0)

    def row_copy(idx_ref, k, r, s):
        return pltpu.make_async_copy(buf.at[s, pl.ds(r, 1)],
                                     xs_hbm.at[pl.ds(idx_ref[0, 0, k * PLAN_T + r], 1)], sem.at[s])

    def wait_block(n, s):
        def body(r, c):
            row_copy(slot_ref, 0, 0, s).wait()
            return c
        lax.fori_loop(0, n, body, 0, unroll=8)

    for k in range(2):
        @pl.when(i >= 1)
        def _():
            wait_block(PLAN_T, k)

        @pl.when(i < ns_p)
        def _():
            buf[k] = hxp_ref[k * PLAN_T:(k + 1) * PLAN_T, :]
            for r in range(PLAN_T):
                row_copy(slot_ref, k, r, k).start()

    @pl.when(i == ns_p)
    def _():
        buf[0, :n_s] = hxs_ref[...]
        for r in range(n_s):
            row_copy(slots_ref, 0, r, 0).start()
        wait_block(n_s, 0)


def _dispatch(slot, hx_p, hx_s, n_slots):
    n_p, n_s = hx_p.shape[0], hx_s.shape[0]
    ns_p = n_p // (2 * PLAN_T)
    assert n_p == ns_p * 2 * PLAN_T and ns_p >= 1 and 0 < n_s <= PLAN_T
    slot_p = slot[:n_p].reshape(ns_p, 1, 2 * PLAN_T)
    slot_s = slot[n_p:n_p + PLAN_T].reshape(1, 1, PLAN_T)
    any_spec = pl.BlockSpec(memory_space=pl.ANY)
    last = ns_p - 1
    return pl.pallas_call(
        functools.partial(_dispatch_kernel, ns_p=ns_p, n_s=n_s),
        grid=(ns_p + 1,),
        in_specs=[pl.BlockSpec((1, 1, 2 * PLAN_T), lambda i: (jnp.minimum(i, last), 0, 0), memory_space=pltpu.SMEM),
                  pl.BlockSpec((1, 1, PLAN_T), lambda i: (0, 0, 0), memory_space=pltpu.SMEM),
                  pl.BlockSpec((2 * PLAN_T, HX_W), lambda i: (jnp.minimum(i, last), 0)),
                  pl.BlockSpec((n_s, HX_W), lambda i: (0, 0)), any_spec],
        out_specs=any_spec,
        out_shape=jax.ShapeDtypeStruct((n_slots, HX_W), F32),
        scratch_shapes=[pltpu.VMEM((2, PLAN_T, HX_W), F32), pltpu.SemaphoreType.DMA((2,))],
        input_output_aliases={4: 0},
        compiler_params=_params(("arbitrary",)),
        name="moe_dispatch",
    )(slot_p, slot_s, hx_p, hx_s, jnp.zeros((n_slots, HX_W), F32))


def _expert_kernel(ea_ref, eb_ref, nv_ref, xs_ref, wga_ref, wua_ref, wda_ref, wgb_ref, wub_ref, wdb_ref, ys_ref,
                   gu_scr, act_scr):
    del ea_ref, eb_ref

    @pl.when(pl.program_id(0) < nv_ref[0])
    def _():
        x = xs_ref[:, :D_MODEL].astype(BF16)
        for k, w_ref in enumerate((wga_ref, wua_ref, wgb_ref, wub_ref)):
            gu_scr[k] = jnp.dot(x, w_ref[0], preferred_element_type=F32)
        for k in range(2):
            g = gu_scr[2 * k]
            act_scr[k] = (g * jax.nn.sigmoid(g) * gu_scr[2 * k + 1]).astype(BF16)
        wa = xs_ref[:, D_MODEL:D_MODEL + 1]
        wb = xs_ref[:, D_MODEL + 1:D_MODEL + 2]
        ys_ref[...] = (jnp.dot(act_scr[0], wda_ref[0], preferred_element_type=F32) * wa
                       + jnp.dot(act_scr[1], wdb_ref[0], preferred_element_type=F32) * wb)

    @pl.when(pl.program_id(0) >= nv_ref[0])
    def _():
        ys_ref[...] = jnp.zeros_like(ys_ref)


def _class_experts():
    ea, eb = [], []
    for g in range(N_GROUPS):
        for a in range(EXPERTS_PER_GROUP):
            for b in range(a + 1, EXPERTS_PER_GROUP):
                ea.append(g * EXPERTS_PER_GROUP + a)
                eb.append(g * EXPERTS_PER_GROUP + b)
    return np.asarray(ea, np.int32), np.asarray(eb, np.int32)


def _experts(xs, pends, w_gate, w_up, w_down):
    n_tiles = xs.shape[0] // MOE_TM
    nv = (pends[N_CLASSES - 1] // MOE_TM).astype(jnp.int32)
    tile_cls = jnp.sum(pends[None, :N_CLASSES] <= (jnp.arange(n_tiles) * MOE_TM)[:, None], axis=1)
    tile_cls = jnp.minimum(tile_cls, N_CLASSES - 1)
    cls_ea, cls_eb = _class_experts()
    tile_ea = jnp.asarray(cls_ea)[tile_cls]
    tile_eb = jnp.asarray(cls_eb)[tile_cls]
    wa_spec = lambda shp: pl.BlockSpec((1,) + shp, lambda i, ea, eb, nv_: (ea[i], 0, 0))
    wb_spec = lambda shp: pl.BlockSpec((1,) + shp, lambda i, ea, eb, nv_: (eb[i], 0, 0))
    up, down = (D_MODEL, D_EXPERT), (D_EXPERT, D_MODEL)
    grid_spec = pltpu.PrefetchScalarGridSpec(
        num_scalar_prefetch=3,
        grid=(n_tiles,),
        in_specs=[pl.BlockSpec((MOE_TM, HX_W), lambda i, ea, eb, nv_: (i, 0)),
                  wa_spec(up), wa_spec(up), wa_spec(down), wb_spec(up), wb_spec(up), wb_spec(down)],
        out_specs=pl.BlockSpec((MOE_TM, D_MODEL), lambda i, ea, eb, nv_: (i, 0)),
        scratch_shapes=[pltpu.VMEM((4, MOE_TM, D_EXPERT), F32), pltpu.VMEM((2, MOE_TM, D_EXPERT), BF16)],
    )
    return pl.pallas_call(
        _expert_kernel,
        grid_spec=grid_spec,
        out_shape=jax.ShapeDtypeStruct((n_tiles * MOE_TM, D_MODEL), F32),
        compiler_params=_params(("arbitrary",)),
        name="moe_experts",
    )(tile_ea, tile_eb, nv.reshape(1), xs, w_gate, w_up, w_down, w_gate, w_up, w_down)


def _ple_kernel(slot_ref, slotn_ref, x1_ref, p_ref, wproj_ref, wgate_ref, gple_ref, gfin_ref, ys_hbm,
                y_ref, ybuf, sem, *, tm):
    i = pl.program_id(0)

    def row_copy(idx_ref, k, r, s):
        return pltpu.make_async_copy(ys_hbm.at[pl.ds(idx_ref[0, 0, k * tm + r], 1)], ybuf.at[s, pl.ds(r, 1)],
                                     sem.at[s])

    def for_rows(fn):
        def body(r, c):
            fn(r)
            return c
        lax.fori_loop(0, tm, body, 0, unroll=8)

    def wait_tile(s):
        for_rows(lambda r: row_copy(slot_ref, 0, 0, s).wait())

    def compute(k, s):
        rows = slice(k * tm, (k + 1) * tm)
        x2 = x1_ref[rows, :] + ybuf[s]
        proj = jnp.dot(p_ref[rows, :].astype(BF16), wproj_ref[...], preferred_element_type=F32)
        gate = jnp.dot(_rms(x2, gple_ref[...]).astype(BF16), wgate_ref[...], preferred_element_type=F32)
        x3 = x2 + proj * jax.nn.sigmoid(gate)
        y_ref[rows, :] = _rms(x3, gfin_ref[...])

    @pl.when(i == 0)
    def _():
        for_rows(lambda r: row_copy(slot_ref, 0, r, 0).start())

    wait_tile(0)
    for r in range(tm):
        row_copy(slot_ref, 1, r, 1).start()
    compute(0, 0)
    wait_tile(1)
    for r in range(tm):
        row_copy(slotn_ref, 0, r, 0).start()
    compute(1, 1)

    @pl.when(i == pl.num_programs(0) - 1)
    def _():
        wait_tile(0)


def _ple(x1, ys, slot, p2d, lw, g_final, *, tm):
    n = x1.shape[0]
    ns = n // (2 * tm)
    assert n == ns * 2 * tm
    slot3 = slot.reshape(ns, 1, 2 * tm)
    row = lambda i: (i, 0)
    const = lambda i: (0, 0)
    idx_spec = lambda f: pl.BlockSpec((1, 1, 2 * tm), f, memory_space=pltpu.SMEM)
    return pl.pallas_call(
        functools.partial(_ple_kernel, tm=tm),
        grid=(ns,),
        in_specs=[
            idx_spec(lambda i: (i, 0, 0)), idx_spec(lambda i: (jnp.minimum(i + 1, ns - 1), 0, 0)),
            pl.BlockSpec((2 * tm, D_MODEL), row), pl.BlockSpec((2 * tm, D_PLE), row),
            pl.BlockSpec((D_PLE, D_MODEL), const), pl.BlockSpec((D_MODEL, D_MODEL), const),
            pl.BlockSpec((1, D_MODEL), const), pl.BlockSpec((1, D_MODEL), const),
            pl.BlockSpec(memory_space=pl.ANY),
        ],
        out_specs=pl.BlockSpec((2 * tm, D_MODEL), row),
        out_shape=jax.ShapeDtypeStruct((n, D_MODEL), F32),
        scratch_shapes=[pltpu.VMEM((2, tm, D_MODEL), F32), pltpu.SemaphoreType.DMA((2,))],
        compiler_params=_params(("arbitrary",)),
        name="ple",
    )(slot3, slot3, x1, p2d, lw["w_ple_proj"], lw["w_ple_gate"], lw["g_ple"], g_final.reshape(1, D_MODEL), ys)


def _layer_weights(i, norm_mix_g, w_in, ret_norm_g, w_proj_a, w_proj_b, w_out, norm_ffn_g,
                   w_router_group, b_router_group, w_router_expert, b_router_expert,
                   w_gate_e, w_up_e, w_down_e, norm_ple_g, w_ple_gate, w_ple_proj):
    w = w_in[i]
    o = _IN_OFFS
    cols = lambda k: w[:, o[k]:o[k + 1]]
    w_main = jnp.concatenate([cols(0), cols(1), cols(2), cols(3), cols(5), cols(6)], axis=1).astype(BF16)
    wr = jnp.zeros((ROUTER_ROWS, D_MODEL), F32)
    wr = wr.at[:N_GROUPS].set(w_router_group[i].T).at[8:].set(w_router_expert[i].T)
    br = jnp.full((ROUTER_ROWS,), NEG_BIG, F32)
    br = br.at[:N_GROUPS].set(b_router_group[i].astype(F32)).at[8:].set(b_router_expert[i].astype(F32))
    wr_hi, wr_lo = _split_bf16(wr)
    return {
        "g_mix": norm_mix_g[i].reshape(1, D_MODEL), "w_main": w_main, "w_kt": cols(4).T.astype(BF16),
        "w_qvt": jnp.concatenate([cols(0).T, cols(2).T], axis=0).astype(BF16),
        "w_ga": cols(7).astype(BF16), "w_gb": cols(8).astype(BF16),
        "ret_norm_g": ret_norm_g[i], "w_pa": w_proj_a[i].astype(BF16), "w_pb": w_proj_b[i].astype(BF16),
        "w_out": w_out[i].astype(BF16), "g_ffn": norm_ffn_g[i].reshape(1, D_MODEL),
        "wr_hi": wr_hi, "wr_lo": wr_lo, "b_r": br.reshape(ROUTER_ROWS, 1),
        "w_gate": w_gate_e[i].astype(BF16), "w_up": w_up_e[i].astype(BF16), "w_down": w_down_e[i].astype(BF16),
        "g_ple": norm_ple_g[i].reshape(1, D_MODEL), "w_ple_gate": w_ple_gate[i].astype(BF16),
        "w_ple_proj": w_ple_proj[i].astype(BF16),
    }


def _moe(hx_p, cls_p, hx_s, cls_s, lw):
    n_p, n_s = hx_p.shape[0], hx_s.shape[0]
    n_real = n_p + n_s
    n_pad = pl.cdiv(n_p + PLAN_T, RANK_T) * RANK_T
    cls_all = jnp.concatenate([cls_p, cls_s, jnp.zeros((8, n_pad - n_real), jnp.int32)], axis=1)
    slot, pends = _routing_plan(cls_all, n_real)
    n_slots = (pl.cdiv(n_real, MOE_TM) + N_CLASSES) * MOE_TM
    xs = _dispatch(slot, hx_p, hx_s, n_slots)
    ys = _experts(xs, pends, lw["w_gate"], lw["w_up"], lw["w_down"])
    return ys, slot[:n_p], slot[n_p:n_real]


def kernel(x_prompt, x_sample, cache_k_a, cache_v_a, state_ret, p_prompt, p_sample, norm_mix_g, w_in, rel_bias, ret_norm_g, w_proj_a, w_proj_b, w_out, norm_ffn_g, w_router_group, b_router_group, w_router_expert, b_router_expert, w_gate_e, w_up_e, w_down_e, norm_ple_g, w_ple_gate, w_ple_proj, final_norm_g):
    depth = w_in.shape[0]
    assert depth == 1, "the final norm is fused into the last layer; deeper stacks are not supported"
    bp, sp, _ = x_prompt.shape
    bs, ss, _ = x_sample.shape
    keep = min(WINDOW_A, sp)
    n_cache = cache_k_a.shape[2]
    log_g = jnp.log(1.0 - 2.0 ** (-5.0 - jnp.arange(N_HEADS_B, dtype=F32)))
    i = 0
    lw = _layer_weights(i, norm_mix_g, w_in, ret_norm_g, w_proj_a, w_proj_b, w_out, norm_ffn_g,
                        w_router_group, b_router_group, w_router_expert, b_router_expert,
                        w_gate_e, w_up_e, w_down_e, norm_ple_g, w_ple_gate, w_ple_proj)

    tm = 512
    assert sp % tm == 0 and keep == tm and sp >= ATT_WIN
    t_ret = 128
    xp2 = x_prompt.reshape(bp * sp, D_MODEL)
    qa, ka, va, qb, kbt, vb, gb, ka32, va32 = _project(
        xp2, jnp.arange(sp), lw["g_mix"], lw["w_main"], lw["w_kt"], lw["w_qvt"], tm=tm,
        tiles_per_keep=sp // tm, feature_major_qv=True)
    r3 = lambda a: a.reshape(bp, sp, a.shape[-1])
    att = _attention_prompt(qa, r3(ka), va, rel_bias[i], bp)
    per_tile = tm // t_ret
    tiles_per_b = sp // tm
    b_in, s_prompt = _retention(
        r3(qb), kbt, r3(vb), r3(gb), jnp.zeros((bp, N_HEADS_B, DK_B, DV_B), F32), log_g, lw["ret_norm_g"],
        t=t_ret, kt_index=lambda bi, c: (bi * tiles_per_b + c // per_tile, 0, c % per_tile))
    x1_p, hx_p, cls_p = _merge(xp2, att.reshape(bp * sp, W_A), b_in.reshape(bp * sp, V_B), lw, tm=tm)
    k_a_prompt = ka32.reshape(bp, keep, N_HEADS_A, HEAD_DIM_A)
    v_a_prompt = va32.reshape(bp, keep, N_HEADS_A, HEAD_DIM_A)

    ns = bs * ss
    xs2 = x_sample.reshape(ns, D_MODEL)
    pos_s = jnp.tile(PAST_LEN + jnp.arange(ss), bs)
    qa, ka, va, qb, kbt, vb, gb, ka32, va32 = _project(
        xs2, pos_s, lw["g_mix"], lw["w_main"], lw["w_kt"], lw["w_qvt"], tm=ns, tiles_per_keep=1,
        feature_major_qv=False)
    r3 = lambda a: a.reshape(bs, ss, a.shape[-1])
    k_all = jnp.concatenate([cache_k_a[i].reshape(bs, n_cache, W_A).astype(BF16), r3(ka)], axis=1)
    v_all = jnp.concatenate([cache_v_a[i].reshape(bs, n_cache, W_A).astype(BF16), r3(va)], axis=1)
    att = _attention_sample(r3(qa), k_all, v_all, rel_bias[i], n_cache)
    kbt_s = kbt.reshape(QK_B, bs, ss).transpose(1, 0, 2)
    b_in, s_sample = _retention(
        r3(qb), kbt_s, r3(vb), r3(gb), state_ret[i].astype(F32), log_g, lw["ret_norm_g"],
        t=ss, kt_index=lambda bi, c: (bi, 0, 0))
    x1_s, hx_s, cls_s = _merge(xs2, att.reshape(ns, W_A), b_in.reshape(ns, V_B), lw, tm=ns)

    ys, slot_p, slot_s = _moe(hx_p, cls_p, hx_s, cls_s, lw)
    y_prompt = _ple(x1_p, ys, slot_p, p_prompt[i].reshape(bp * sp, D_PLE), lw, final_norm_g, tm=tm)
    y_sample = _ple(x1_s, ys, slot_s, p_sample[i].reshape(ns, D_PLE), lw, final_norm_g, tm=ns // 2)
    k_a_sample = ka32.reshape(bs, ss, N_HEADS_A, HEAD_DIM_A)
    v_a_sample = va32.reshape(bs, ss, N_HEADS_A, HEAD_DIM_A)

    return (y_prompt.reshape(bp, sp, D_MODEL), y_sample.reshape(bs, ss, D_MODEL),
            k_a_prompt[None], v_a_prompt[None], s_prompt[None],
            k_a_sample[None], v_a_sample[None], s_sample.astype(state_ret.dtype)[None])
```

```python
import functools

import numpy as np
import jax
import jax.numpy as jnp
from jax import lax
from jax.experimental import pallas as pl
from jax.experimental.pallas import tpu as pltpu

F32 = jnp.float32
BF16 = jnp.bfloat16

D_MODEL = 1024
PAST_LEN = 1024
CHUNK = 64
BAND_CHUNKS = 8
WINDOW_A = BAND_CHUNKS * CHUNK
N_HEADS_A = 8
HEAD_DIM_A = 64
W_A = N_HEADS_A * HEAD_DIM_A
REL_CLIP = 128
N_HEADS_B = 4
DK_B = 128
DV_B = 256
QK_B = N_HEADS_B * DK_B
V_B = N_HEADS_B * DV_B
ROPE_BASE = 10000.0
N_GROUPS = 4
EXPERTS_PER_GROUP = 8
N_EXPERTS = N_GROUPS * EXPERTS_PER_GROUP
TOP_K = 2
D_EXPERT = 512
D_PLE = 256
EPS = 1e-6
_IN_SIZES = (W_A, W_A, W_A, QK_B, QK_B, V_B, V_B, D_MODEL, D_MODEL)
_IN_OFFS = tuple(sum(_IN_SIZES[:i]) for i in range(len(_IN_SIZES) + 1))

LANES = 128
ATT_QBLK = 2 * CHUNK
ATT_WIN = (BAND_CHUNKS + 2) * CHUNK
ATT_QPS = 2
ROUTER_ROWS = 8 + N_EXPERTS
NEG_BIG = -1e30
PAIRS_PER_GROUP = EXPERTS_PER_GROUP * (EXPERTS_PER_GROUP - 1) // 2
N_CLASSES = N_GROUPS * PAIRS_PER_GROUP
CLASS_ROWS = 128
HX_W = D_MODEL + LANES
PLAN_T = 512
RANK_T = 1024
MOE_TM = 256
VMEM_LIMIT = 56 * 1024 * 1024


def _params(sem):
    return pltpu.CompilerParams(dimension_semantics=sem, vmem_limit_bytes=VMEM_LIMIT)


def _rms(x, g):
    return x * lax.rsqrt(jnp.mean(x * x, axis=-1, keepdims=True) + EPS) * g


def _proj_kernel(x_ref, g_ref, w_ref, wkt_ref, wqvt_ref, cos_ref, sin_ref, cost_ref, sint_ref,
                 qa_ref, ka_ref, va_ref, qb_ref, kbt_ref, vb_ref, gb_ref, ka32_ref, va32_ref,
                 *, tiles_per_keep, feature_major_qv):
    h = _rms(x_ref[...], g_ref[...]).astype(BF16)
    nt_dims = (((1,), (1,)), ((), ()))

    def seg(lo, hi):
        return jnp.dot(h, w_ref[:, lo:hi], preferred_element_type=F32)

    ka = seg(W_A, 2 * W_A)
    ka_ref[...] = ka.astype(BF16)
    q_scale = HEAD_DIM_A ** -0.5
    if feature_major_qv:
        qvt = lax.dot_general(wqvt_ref[...], h, nt_dims, preferred_element_type=F32)
        for c in range(qa_ref.shape[0]):
            cs = slice(c * LANES, (c + 1) * LANES)
            qa_ref[c] = (qvt[:W_A, cs] * q_scale).astype(BF16)
            va_ref[c] = qvt[W_A:, cs].astype(BF16)
    else:
        qa_ref[...] = (seg(0, W_A) * q_scale).astype(BF16)
        va_ref[...] = seg(2 * W_A, 3 * W_A).astype(BF16)

    @pl.when(pl.program_id(0) % tiles_per_keep == tiles_per_keep - 1)
    def _():
        ka32_ref[...] = ka
        va32_ref[...] = seg(2 * W_A, 3 * W_A)

    qb = seg(3 * W_A, 3 * W_A + QK_B)
    cos = cos_ref[...]
    sin = sin_ref[...]
    for hd in range(N_HEADS_B):
        xh = qb[:, hd * DK_B:(hd + 1) * DK_B]
        qb_ref[:, hd * DK_B:(hd + 1) * DK_B] = (xh * cos + pltpu.roll(xh, DK_B // 2, axis=1) * sin).astype(BF16)

    vb_ref[...] = seg(3 * W_A + QK_B, 3 * W_A + QK_B + V_B).astype(BF16)
    gb_ref[...] = seg(3 * W_A + QK_B + V_B, 3 * W_A + QK_B + 2 * V_B)

    kt = lax.dot_general(wkt_ref[...], h, (((1,), (1,)), ((), ())), preferred_element_type=F32)
    cost = cost_ref[...]
    sint = sint_ref[...]
    half = DK_B // 2
    scale = DK_B ** -0.5
    for hd in range(N_HEADS_B):
        x1 = kt[hd * DK_B:hd * DK_B + half, :]
        x2 = kt[hd * DK_B + half:(hd + 1) * DK_B, :]
        kbt_ref[0, hd * DK_B:hd * DK_B + half, :] = (x1 * cost - x2 * sint) * scale
        kbt_ref[0, hd * DK_B + half:(hd + 1) * DK_B, :] = (x2 * cost + x1 * sint) * scale


def _rope_tables(pos):
    half = DK_B // 2
    freqs = ROPE_BASE ** (-jnp.arange(half, dtype=F32) / half)
    ang = pos.astype(F32)[:, None] * freqs[None, :]
    cos = jnp.cos(ang)
    sin = jnp.sin(ang)
    return (jnp.concatenate([cos, cos], axis=1), jnp.concatenate([-sin, sin], axis=1), cos.T, sin.T)


def _project(x2d, pos_rows, g_norm, w_main, w_kt, w_qvt, *, tm, tiles_per_keep, feature_major_qv):
    n = x2d.shape[0]
    period = pos_rows.shape[0]
    nt = n // tm
    ppt = period // tm
    cos2, sin2, cost, sint = _rope_tables(pos_rows)
    n_keep = n // tiles_per_keep
    row = lambda i: (i, 0)
    const = lambda i: (0, 0)
    if feature_major_qv:
        spt = tm // LANES
        qv_shape = jax.ShapeDtypeStruct((n // LANES, W_A, LANES), BF16)
        qv_spec = pl.BlockSpec((spt, W_A, LANES), lambda i: (i, 0, 0))
    else:
        qv_shape = jax.ShapeDtypeStruct((n, W_A), BF16)
        qv_spec = pl.BlockSpec((tm, W_A), row)
    outs = (
        qv_shape, jax.ShapeDtypeStruct((n, W_A), BF16), qv_shape, jax.ShapeDtypeStruct((n, QK_B), BF16),
        jax.ShapeDtypeStruct((nt, QK_B, tm), F32), jax.ShapeDtypeStruct((n, V_B), BF16),
        jax.ShapeDtypeStruct((n, V_B), F32),
        jax.ShapeDtypeStruct((n_keep, W_A), F32), jax.ShapeDtypeStruct((n_keep, W_A), F32),
    )
    keep_spec = pl.BlockSpec((tm, W_A), lambda i: (i // tiles_per_keep, 0))
    return pl.pallas_call(
        functools.partial(_proj_kernel, tiles_per_keep=tiles_per_keep, feature_major_qv=feature_major_qv),
        grid=(nt,),
        in_specs=[
            pl.BlockSpec((tm, D_MODEL), row),
            pl.BlockSpec((1, D_MODEL), const),
            pl.BlockSpec(w_main.shape, const),
            pl.BlockSpec(w_kt.shape, const),
            pl.BlockSpec(w_qvt.shape, const),
            pl.BlockSpec((tm, DK_B), lambda i: (i % ppt, 0)),
            pl.BlockSpec((tm, DK_B), lambda i: (i % ppt, 0)),
            pl.BlockSpec((DK_B // 2, tm), lambda i: (0, i % ppt)),
            pl.BlockSpec((DK_B // 2, tm), lambda i: (0, i % ppt)),
        ],
        out_specs=(
            qv_spec, pl.BlockSpec((tm, W_A), row), qv_spec,
            pl.BlockSpec((tm, QK_B), row), pl.BlockSpec((1, QK_B, tm), lambda i: (i, 0, 0)),
            pl.BlockSpec((tm, V_B), row), pl.BlockSpec((tm, V_B), row), keep_spec, keep_spec,
        ),
        out_shape=outs,
        compiler_params=_params(("arbitrary",)),
        name="proj",
    )(x2d, g_norm.reshape(1, D_MODEL), w_main, w_kt, w_qvt, cos2, sin2, cost, sint)


def _attend_pairs(q_of, k_of, v_of, bias_of, store):
    for hp in range(N_HEADS_A // 2):
        qp = q_of(hp)
        kw = k_of(hp)
        vw = v_of(hp)
        lane = lax.broadcasted_iota(jnp.int32, qp.shape, 1)
        outs = []
        for hh in range(2):
            in_head = (lane >= hh * HEAD_DIM_A) & (lane < (hh + 1) * HEAD_DIM_A)
            qh = jnp.where(in_head, qp, jnp.zeros_like(qp))
            s = lax.dot_general(qh, kw, (((1,), (1,)), ((), ())), preferred_element_type=F32)
            s = s + bias_of(2 * hp + hh)
            m = jnp.max(s, axis=-1, keepdims=True)
            p = jnp.exp(s - m)
            l = jnp.sum(p, axis=-1, keepdims=True)
            o = jnp.dot(p.astype(BF16), vw, preferred_element_type=F32)
            outs.append(o / l)
        lane_o = lax.broadcasted_iota(jnp.int32, outs[0].shape, 1)
        store(hp, jnp.where(lane_o < HEAD_DIM_A, outs[0], outs[1]).astype(BF16))


def _fold_rows(x, op, reduce_rows):
    r = x.shape[0]
    while r % 16 == 0:
        r //= 2
        x = op(x[:r], x[r:])
    parts = [x[a:a + 8] for a in range(0, r, 8)]
    while len(parts) > 1:
        parts = [op(parts[a], parts[a + 1]) if a + 1 < len(parts) else parts[a] for a in range(0, len(parts), 2)]
    return reduce_rows(parts[0], axis=0, keepdims=True)


def _attn_prompt_kernel(qt_ref, k_ref, vt_ref, *refs):
    bias_refs = refs[:ATT_QPS]
    o_ref, s_scr, p_scr = refs[ATT_QPS:]
    n_slab = ATT_WIN // LANES
    n_pairs = N_HEADS_A // 2
    units = [(qb, hp) for qb in range(ATT_QPS) for hp in range(n_pairs)]
    first = [jnp.maximum(pl.program_id(1) * ATT_QPS + qb - BAND_CHUNKS // 2, 0) for qb in range(ATT_QPS)]
    rows = lambda hp: slice(hp * LANES, (hp + 1) * LANES)

    def scores(qb, hp):
        qt = qt_ref[qb, rows(hp), :]
        dim = lax.broadcasted_iota(jnp.int32, qt.shape, 0)
        zero = jnp.zeros_like(qt)
        w = jnp.concatenate([jnp.where(dim < HEAD_DIM_A, qt, zero), jnp.where(dim >= HEAD_DIM_A, qt, zero)], axis=1)
        start = pl.multiple_of(first[qb] * ATT_QBLK, ATT_QBLK)
        kw = k_ref[0, pl.ds(start, ATT_WIN), rows(hp)]
        return jnp.dot(kw, w, preferred_element_type=F32)

    for u, (qb, hp) in enumerate(units):
        s_scr[u] = scores(qb, hp) + bias_refs[qb][0, hp]
    denom = []
    for u in range(len(units)):
        s = s_scr[u]
        m = _fold_rows(s, jnp.maximum, jnp.max)
        p = jnp.exp(s - m)
        denom.append(_fold_rows(p, jnp.add, jnp.sum))
        p_scr[u] = p.astype(BF16)
    for u, (qb, hp) in enumerate(units):
        vt = jnp.concatenate([vt_ref[first[qb] + c, rows(hp), :] for c in range(n_slab)], axis=1)
        ot = jnp.dot(vt, p_scr[u], preferred_element_type=F32) / denom[u]
        odim = lax.broadcasted_iota(jnp.int32, (LANES, LANES), 0)
        o_pair_t = jnp.where(odim < HEAD_DIM_A, ot[:, :LANES], ot[:, LANES:])
        o_ref[0, qb * ATT_QBLK:(qb + 1) * ATT_QBLK, rows(hp)] = o_pair_t.T.astype(BF16)


def _band_bias(table):
    i = np.arange(ATT_QBLK)[None, :]
    jk = np.arange(ATT_WIN)[:, None]
    out = []
    for v in range(BAND_CHUNKS // 2 + 1):
        off_chunks = 2 * v if v < BAND_CHUNKS // 2 else BAND_CHUNKS
        dchunk = (off_chunks + i // CHUNK) - jk // CHUNK
        valid = (dchunk >= 0) & (dchunk <= BAND_CHUNKS)
        n_f = ATT_WIN + ATT_QBLK - 1
        idx = np.clip(off_chunks * CHUNK + (ATT_QBLK - 1) - np.arange(n_f + 1), -REL_CLIP, REL_CLIP) + REL_CLIP
        g = table[:, idx].astype(F32)
        rows = jnp.tile(g, (1, ATT_QBLK))[:, :ATT_QBLK * n_f].reshape(N_HEADS_A, ATT_QBLK, n_f)
        b = rows[:, :, ATT_QBLK - 1:].transpose(0, 2, 1)
        b = jnp.where(valid[None], b, jnp.float32(NEG_BIG))
        b = b.reshape(N_HEADS_A // 2, 2, ATT_WIN, ATT_QBLK).transpose(0, 2, 1, 3)
        out.append(b.reshape(N_HEADS_A // 2, ATT_WIN, 2 * ATT_QBLK))
    return jnp.stack(out)


def _attention_prompt(qat, ka, vat, table, b):
    s = ka.shape[1]
    nq = s // ATT_QBLK
    ns = nq // ATT_QPS
    assert nq == ns * ATT_QPS
    bias = _band_bias(table)
    nvar = bias.shape[0]
    n_units = ATT_QPS * N_HEADS_A // 2
    bias_specs = [pl.BlockSpec((1, N_HEADS_A // 2, ATT_WIN, 2 * ATT_QBLK),
                               functools.partial(lambda bi, j, qb: (jnp.minimum(j * ATT_QPS + qb, nvar - 1), 0, 0, 0),
                                                 qb=qb)) for qb in range(ATT_QPS)]
    return pl.pallas_call(
        _attn_prompt_kernel,
        grid=(b, ns),
        in_specs=[
            pl.BlockSpec((ATT_QPS, W_A, LANES), lambda bi, j: (bi * ns + j, 0, 0)),
            pl.BlockSpec((1, s, W_A), lambda bi, j: (bi, 0, 0)),
            pl.BlockSpec((nq, W_A, LANES), lambda bi, j: (bi, 0, 0)),
        ] + bias_specs,
        out_specs=pl.BlockSpec((1, ATT_QPS * ATT_QBLK, W_A), lambda bi, j: (bi, j, 0)),
        out_shape=jax.ShapeDtypeStruct((b, s, W_A), BF16),
        scratch_shapes=[pltpu.VMEM((n_units, ATT_WIN, 2 * ATT_QBLK), F32),
                        pltpu.VMEM((n_units, ATT_WIN, 2 * ATT_QBLK), BF16)],
        compiler_params=_params(("arbitrary", "arbitrary")),
        name="attn_prompt",
    )(qat, ka, vat, *([bias] * ATT_QPS))


def _attn_sample_kernel(q_ref, k_ref, v_ref, bias_ref, o_ref):
    def sl(hp):
        return slice(hp * LANES, (hp + 1) * LANES)

    def store(hp, val):
        o_ref[0, :, sl(hp)] = val

    _attend_pairs(
        lambda hp: q_ref[0, :, sl(hp)],
        lambda hp: k_ref[0, :, sl(hp)],
        lambda hp: v_ref[0, :, sl(hp)],
        lambda hd: bias_ref[hd],
        store)


def _attention_sample(qa, k_all, v_all, table, n_cache):
    b, n, _ = qa.shape
    nk = k_all.shape[1]
    dist = jnp.arange(n)[:, None] + n_cache - jnp.arange(nk)[None, :]
    bias = table[:, jnp.clip(dist, -REL_CLIP, REL_CLIP) + REL_CLIP].astype(F32)
    return pl.pallas_call(
        _attn_sample_kernel,
        grid=(b,),
        in_specs=[
            pl.BlockSpec((1, n, W_A), lambda bi: (bi, 0, 0)),
            pl.BlockSpec((1, nk, W_A), lambda bi: (bi, 0, 0)),
            pl.BlockSpec((1, nk, W_A), lambda bi: (bi, 0, 0)),
            pl.BlockSpec((N_HEADS_A, n, nk), lambda bi: (0, 0, 0)),
        ],
        out_specs=pl.BlockSpec((1, n, W_A), lambda bi: (bi, 0, 0)),
        out_shape=jax.ShapeDtypeStruct((b, n, W_A), BF16),
        compiler_params=_params(("arbitrary",)),
        name="attn_sample",
    )(qa, k_all, v_all, bias)


def _ret_kernel(gt_ref, q_ref, *refs, nb):
    kt_refs = refs[:nb]
    v_ref, gb_ref, s0_ref, dmask_ref, qd_ref, kd_ref, gn_ref, out_ref, state_ref, s_scr, o_scr = refs[nb:]

    @pl.when(pl.program_id(1) == 0)
    def _():
        state_ref[...] = s0_ref[...]

    units = [(bb, hd) for bb in range(nb) for hd in range(N_HEADS_B)]
    qs = lambda hd: slice(hd * DK_B, (hd + 1) * DK_B)
    vs = lambda hd: slice(hd * DV_B, (hd + 1) * DV_B)

    for u, (bb, hd) in enumerate(units):
        q = q_ref[bb, :, qs(hd)]
        kt = kt_refs[bb][0, qs(hd), :]
        s_scr[u] = (jnp.dot(q, kt.astype(BF16), preferred_element_type=F32) * dmask_ref[hd]).astype(BF16)
        o_scr[u] = jnp.dot(q, state_ref[bb, hd].astype(BF16), preferred_element_type=F32) * qd_ref[hd]
    for u, (bb, hd) in enumerate(units):
        v = v_ref[bb, :, vs(hd)]
        o_scr[u] = o_scr[u] + jnp.dot(s_scr[u], v, preferred_element_type=F32)
        kd = (kt_refs[bb][0, qs(hd), :] * kd_ref[hd]).astype(BF16)
        state_ref[bb, hd] = state_ref[bb, hd] * gt_ref[hd] + jnp.dot(kd, v, preferred_element_type=F32)
    for u, (bb, hd) in enumerate(units):
        o = o_scr[u]
        mu = jnp.mean(o, axis=-1, keepdims=True)
        var = jnp.mean(jnp.square(o - mu), axis=-1, keepdims=True)
        rb = (o - mu) * lax.rsqrt(var + EPS) * gn_ref[:, vs(hd)]
        gb = gb_ref[bb, :, vs(hd)]
        out_ref[bb, :, vs(hd)] = (gb * jax.nn.sigmoid(gb) * rb).astype(BF16)


def _retention(qb, kbt, vb, gb, state0, log_g, ret_norm_g, *, t, kt_index, nb=4):
    b, s, _ = qb.shape
    assert b % nb == 0
    nc = s // t
    idx = jnp.arange(t, dtype=F32)
    diff = idx[:, None] - idx[None, :]
    dmask = jnp.where(diff[None] >= 0, jnp.exp(log_g[:, None, None] * jnp.maximum(diff, 0.0)[None]), 0.0)
    q_decay = jnp.exp(log_g[:, None] * (idx[None, :] + 1.0))
    k_decay = jnp.exp(log_g[:, None] * (t - 1.0 - idx[None, :]))
    g_t = jnp.exp(log_g * t)
    qd = jnp.broadcast_to(q_decay[:, :, None], (N_HEADS_B, t, DV_B))
    kd = k_decay[:, None, :]
    const3 = lambda bi, c: (0, 0, 0)
    seq = lambda w: pl.BlockSpec((nb, t, w), lambda bi, c: (bi, c, 0))
    state_spec = pl.BlockSpec((nb, N_HEADS_B, DK_B, DV_B), lambda bi, c: (bi, 0, 0, 0))
    kt_specs = [pl.BlockSpec((1, QK_B, t), functools.partial(lambda bi, c, k: kt_index(nb * bi + k, c), k=k))
                for k in range(nb)]
    return pl.pallas_call(
        functools.partial(_ret_kernel, nb=nb),
        grid=(b // nb, nc),
        in_specs=[pl.BlockSpec(memory_space=pltpu.SMEM), seq(QK_B)] + kt_specs + [
            seq(V_B), seq(V_B), state_spec,
            pl.BlockSpec((N_HEADS_B, t, t), const3),
            pl.BlockSpec((N_HEADS_B, t, DV_B), const3),
            pl.BlockSpec((N_HEADS_B, 1, t), const3),
            pl.BlockSpec((1, V_B), lambda bi, c: (0, 0)),
        ],
        out_specs=(seq(V_B), state_spec),
        out_shape=(jax.ShapeDtypeStruct((b, s, V_B), BF16),
                   jax.ShapeDtypeStruct((b, N_HEADS_B, DK_B, DV_B), F32)),
        scratch_shapes=[pltpu.VMEM((nb * N_HEADS_B, t, t), BF16), pltpu.VMEM((nb * N_HEADS_B, t, DV_B), F32)],
        compiler_params=_params(("arbitrary", "arbitrary")),
        name="retention",
    )(g_t, qb, *([kbt] * nb), vb, gb, state0, dmask, qd, kd, ret_norm_g.reshape(1, V_B))


def _split_bf16(x):
    hi = x.astype(BF16)
    lo = (x - hi.astype(F32)).astype(BF16)
    return hi, lo


def _merge_kernel(x_ref, att_ref, bin_ref, gmix_ref, wga_ref, wgb_ref, wpa_ref, wpb_ref, wout_ref,
                  gffn_ref, wr_hi_ref, wr_lo_ref, br_ref, x1_ref, hx_ref, cls_ref):
    x = x_ref[...]
    h = _rms(x, gmix_ref[...]).astype(BF16)
    gate_a = jnp.dot(h, wga_ref[...], preferred_element_type=F32)
    gate_b = jnp.dot(h, wgb_ref[...], preferred_element_type=F32)
    a = jnp.dot(att_ref[...], wpa_ref[...], preferred_element_type=F32)
    b = jnp.dot(bin_ref[...], wpb_ref[...], preferred_element_type=F32)
    m = jax.nn.sigmoid(gate_a) * a + jax.nn.sigmoid(gate_b) * b
    x1 = x + jnp.dot(m.astype(BF16), wout_ref[...], preferred_element_type=F32)
    x1_ref[...] = x1
    h2 = _rms(x1, gffn_ref[...])
    hx_ref[:, :D_MODEL] = h2

    h_hi, h_lo = _split_bf16(h2)
    nt = (((1,), (1,)), ((), ()))
    lt = (lax.dot_general(wr_hi_ref[...], h_hi, nt, preferred_element_type=F32)
          + lax.dot_general(wr_hi_ref[...], h_lo, nt, preferred_element_type=F32)
          + lax.dot_general(wr_lo_ref[...], h_hi, nt, preferred_element_type=F32)) + br_ref[...]
    tm = lt.shape[1]
    row = lax.broadcasted_iota(jnp.int32, (8, tm), 0)
    lg = lt[0:8, :]
    mg = jnp.max(lg, axis=0, keepdims=True)
    grp = jnp.min(jnp.where(lg == mg, row, 8), axis=0, keepdims=True)
    p_grp = 1.0 / jnp.sum(jnp.exp(lg - mg), axis=0, keepdims=True)
    le = jnp.zeros((8, tm), F32)
    for g in range(N_GROUPS):
        le = jnp.where(grp == g, lt[8 + 8 * g:16 + 8 * g, :], le)
    m0 = jnp.max(le, axis=0, keepdims=True)
    i0 = jnp.min(jnp.where(le == m0, row, 8), axis=0, keepdims=True)
    rest = jnp.where(row == i0, jnp.float32(-jnp.inf), le)
    m1 = jnp.max(rest, axis=0, keepdims=True)
    i1 = jnp.min(jnp.where(rest == m1, row, 8), axis=0, keepdims=True)
    e = jnp.exp(m1 - m0)
    w0 = (1.0 / (1.0 + e)) * p_grp
    w1 = (e / (1.0 + e)) * p_grp
    ea = jnp.minimum(i0, i1)
    eb = jnp.maximum(i0, i1)
    pair = ((ea * (2 * EXPERTS_PER_GROUP - 1 - ea)) >> 1) + (eb - ea - 1)
    cls_ref[...] = jnp.where(row == 0, grp * PAIRS_PER_GROUP + pair, 0)
    wa = jnp.where(i0 < i1, w0, w1)
    wb = jnp.where(i0 < i1, w1, w0)
    wrow = lax.broadcasted_iota(jnp.int32, (LANES, tm), 0)
    wslab = jnp.where(wrow == 0, wa, jnp.where(wrow == 1, wb, 0.0))
    hx_ref[:, D_MODEL:] = wslab.T


def _merge(x2d, att, b_in, lw, *, tm):
    n = x2d.shape[0]
    row = lambda i: (i, 0)
    const = lambda i: (0, 0)
    full = lambda a: pl.BlockSpec(a.shape, const)
    return pl.pallas_call(
        _merge_kernel,
        grid=(n // tm,),
        in_specs=[
            pl.BlockSpec((tm, D_MODEL), row), pl.BlockSpec((tm, W_A), row), pl.BlockSpec((tm, V_B), row),
            full(lw["g_mix"]), full(lw["w_ga"]), full(lw["w_gb"]), full(lw["w_pa"]), full(lw["w_pb"]),
            full(lw["w_out"]), full(lw["g_ffn"]), full(lw["wr_hi"]), full(lw["wr_lo"]), full(lw["b_r"]),
        ],
        out_specs=(pl.BlockSpec((tm, D_MODEL), row), pl.BlockSpec((tm, HX_W), row),
                   pl.BlockSpec((8, tm), lambda i: (0, i))),
        out_shape=(jax.ShapeDtypeStruct((n, D_MODEL), F32), jax.ShapeDtypeStruct((n, HX_W), F32),
                   jax.ShapeDtypeStruct((8, n), jnp.int32)),
        compiler_params=_params(("arbitrary",)),
        name="merge",
    )(x2d, att, b_in, lw["g_mix"], lw["w_ga"], lw["w_gb"], lw["w_pa"], lw["w_pb"], lw["w_out"],
      lw["g_ffn"], lw["wr_hi"], lw["wr_lo"], lw["b_r"])


def _class_onehot(cls_row, base, n_real):
    t = cls_row.shape[1]
    crow = lax.broadcasted_iota(jnp.int32, (CLASS_ROWS, t), 0)
    tok = base + lax.broadcasted_iota(jnp.int32, (CLASS_ROWS, t), 1)
    return (cls_row == crow) & (tok < n_real)


def _rank_kernel(cls_ref, tri_ref, rank_ref, counts_ref, *, n_real):
    i = pl.program_id(0)

    @pl.when(i == 0)
    def _():
        counts_ref[...] = jnp.zeros_like(counts_ref)

    t = tri_ref.shape[0]
    hot = _class_onehot(cls_ref[0:1, :], i * t, n_real)
    incl = jnp.dot(jnp.where(hot, 1.0, 0.0).astype(BF16), tri_ref[...], preferred_element_type=F32)
    carry = counts_ref[:, 0:1]
    rank = jnp.sum(jnp.where(hot, incl + carry, 0.0), axis=0, keepdims=True) - 1.0
    rank_ref[...] = jnp.broadcast_to(rank, rank_ref.shape).astype(jnp.int32)
    counts_ref[...] = counts_ref[...] + incl[:, t - 1:t]


def _slot_kernel(cls_ref, rank_ref, pstart_ref, slot_ref, *, n_real):
    t = cls_ref.shape[1]
    hot = _class_onehot(cls_ref[0:1, :], pl.program_id(0) * t, n_real)
    start = jnp.sum(jnp.where(hot, pstart_ref[...], 0.0), axis=0, keepdims=True)
    slot_ref[...] = jnp.broadcast_to(start.astype(jnp.int32) + rank_ref[0:1, :], slot_ref.shape)


def _routing_plan(cls_all, n_real):
    n_pad = cls_all.shape[1]
    nblk = n_pad // RANK_T
    tri = jnp.asarray(np.triu(np.ones((RANK_T, RANK_T), np.float32)), BF16)
    tok = lambda i: (0, i)
    const = lambda i: (0, 0)
    rank, counts = pl.pallas_call(
        functools.partial(_rank_kernel, n_real=n_real),
        grid=(nblk,),
        in_specs=[pl.BlockSpec((8, RANK_T), tok), pl.BlockSpec((RANK_T, RANK_T), const)],
        out_specs=(pl.BlockSpec((8, RANK_T), tok), pl.BlockSpec((CLASS_ROWS, LANES), const)),
        out_shape=(jax.ShapeDtypeStruct((8, n_pad), jnp.int32), jax.ShapeDtypeStruct((CLASS_ROWS, LANES), F32)),
        compiler_params=_params(("arbitrary",)),
        name="moe_rank",
    )(cls_all, tri)
    counts = counts[:, 0].astype(jnp.int32)
    psizes = ((counts + MOE_TM - 1) // MOE_TM) * MOE_TM
    pends = jnp.cumsum(psizes)
    pstart = (pends - psizes).astype(F32).reshape(CLASS_ROWS, 1)
    slot = pl.pallas_call(
        functools.partial(_slot_kernel, n_real=n_real),
        grid=(nblk,),
        in_specs=[pl.BlockSpec((8, RANK_T), tok), pl.BlockSpec((8, RANK_T), tok),
                  pl.BlockSpec((CLASS_ROWS, 1), const)],
        out_specs=pl.BlockSpec((8, RANK_T), tok),
        out_shape=jax.ShapeDtypeStruct((8, n_pad), jnp.int32),
        compiler_params=_params(("arbitrary",)),
        name="moe_slot",
    )(cls_all, rank, pstart)
    return slot[0], pends


def _dispatch_kernel(slot_ref, slots_ref, hxp_ref, hxs_ref, xs_in_hbm, xs_hbm, buf, sem, *, ns_p, n_s):
    del xs_in_hbm
    i = pl.program_id(0)

    def row_copy(idx_ref, k, r, s):
        return pltpu.make_async_copy(buf.at[s, pl.ds(r, 1)],
                                     xs_hbm.at[pl.ds(idx_ref[0, 0, k * PLAN_T + r], 1)], sem.at[s])

    def wait_block(n, s):
        def body(r, c):
            row_copy(slot_ref, 0, 0, s).wait()
            return c
        lax.fori_loop(0, n, body, 0, unroll=8)

    for k in range(2):
        @pl.when(i >= 1)
        def _():
            wait_block(PLAN_T, k)

        @pl.when(i < ns_p)
        def _():
            buf[k] = hxp_ref[k * PLAN_T:(k + 1) * PLAN_T, :]
            for r in range(PLAN_T):
                row_copy(slot_ref, k, r, k).start()

    @pl.when(i == ns_p)
    def _():
        buf[0, :n_s] = hxs_ref[...]
        for r in range(n_s):
            row_copy(slots_ref, 0, r, 0).start()
        wait_block(n_s, 0)


def _dispatch(slot, hx_p, hx_s, n_slots):
    n_p, n_s = hx_p.shape[0], hx_s.shape[0]
    ns_p = n_p // (2 * PLAN_T)
    assert n_p == ns_p * 2 * PLAN_T and ns_p >= 1 and 0 < n_s <= PLAN_T
    slot_p = slot[:n_p].reshape(ns_p, 1, 2 * PLAN_T)
    slot_s = slot[n_p:n_p + PLAN_T].reshape(1, 1, PLAN_T)
    any_spec = pl.BlockSpec(memory_space=pl.ANY)
    last = ns_p - 1
    return pl.pallas_call(
        functools.partial(_dispatch_kernel, ns_p=ns_p, n_s=n_s),
        grid=(ns_p + 1,),
        in_specs=[pl.BlockSpec((1, 1, 2 * PLAN_T), lambda i: (jnp.minimum(i, last), 0, 0), memory_space=pltpu.SMEM),
                  pl.BlockSpec((1, 1, PLAN_T), lambda i: (0, 0, 0), memory_space=pltpu.SMEM),
                  pl.BlockSpec((2 * PLAN_T, HX_W), lambda i: (jnp.minimum(i, last), 0)),
                  pl.BlockSpec((n_s, HX_W), lambda i: (0, 0)), any_spec],
        out_specs=any_spec,
        out_shape=jax.ShapeDtypeStruct((n_slots, HX_W), F32),
        scratch_shapes=[pltpu.VMEM((2, PLAN_T, HX_W), F32), pltpu.SemaphoreType.DMA((2,))],
        input_output_aliases={4: 0},
        compiler_params=_params(("arbitrary",)),
        name="moe_dispatch",
    )(slot_p, slot_s, hx_p, hx_s, jnp.zeros((n_slots, HX_W), F32))


def _expert_kernel(ea_ref, eb_ref, nv_ref, xs_ref, wga_ref, wua_ref, wda_ref, wgb_ref, wub_ref, wdb_ref, ys_ref,
                   gu_scr, act_scr):
    del ea_ref, eb_ref

    @pl.when(pl.program_id(0) < nv_ref[0])
    def _():
        x = xs_ref[:, :D_MODEL].astype(BF16)
        for k, w_ref in enumerate((wga_ref, wua_ref, wgb_ref, wub_ref)):
            gu_scr[k] = jnp.dot(x, w_ref[0], preferred_element_type=F32)
        for k in range(2):
            g = gu_scr[2 * k]
            act_scr[k] = (g * jax.nn.sigmoid(g) * gu_scr[2 * k + 1]).astype(BF16)
        wa = xs_ref[:, D_MODEL:D_MODEL + 1]
        wb = xs_ref[:, D_MODEL + 1:D_MODEL + 2]
        ys_ref[...] = (jnp.dot(act_scr[0], wda_ref[0], preferred_element_type=F32) * wa
                       + jnp.dot(act_scr[1], wdb_ref[0], preferred_element_type=F32) * wb)

    @pl.when(pl.program_id(0) >= nv_ref[0])
    def _():
        ys_ref[...] = jnp.zeros_like(ys_ref)


def _class_experts():
    ea, eb = [], []
    for g in range(N_GROUPS):
        for a in range(EXPERTS_PER_GROUP):
            for b in range(a + 1, EXPERTS_PER_GROUP):
                ea.append(g * EXPERTS_PER_GROUP + a)
                eb.append(g * EXPERTS_PER_GROUP + b)
    return np.asarray(ea, np.int32), np.asarray(eb, np.int32)


def _experts(xs, pends, w_gate, w_up, w_down):
    n_tiles = xs.shape[0] // MOE_TM
    nv = (pends[N_CLASSES - 1] // MOE_TM).astype(jnp.int32)
    tile_cls = jnp.sum(pends[None, :N_CLASSES] <= (jnp.arange(n_tiles) * MOE_TM)[:, None], axis=1)
    tile_cls = jnp.minimum(tile_cls, N_CLASSES - 1)
    cls_ea, cls_eb = _class_experts()
    tile_ea = jnp.asarray(cls_ea)[tile_cls]
    tile_eb = jnp.asarray(cls_eb)[tile_cls]
    wa_spec = lambda shp: pl.BlockSpec((1,) + shp, lambda i, ea, eb, nv_: (ea[i], 0, 0))
    wb_spec = lambda shp: pl.BlockSpec((1,) + shp, lambda i, ea, eb, nv_: (eb[i], 0, 0))
    up, down = (D_MODEL, D_EXPERT), (D_EXPERT, D_MODEL)
    grid_spec = pltpu.PrefetchScalarGridSpec(
        num_scalar_prefetch=3,
        grid=(n_tiles,),
        in_specs=[pl.BlockSpec((MOE_TM, HX_W), lambda i, ea, eb, nv_: (i, 0)),
                  wa_spec(up), wa_spec(up), wa_spec(down), wb_spec(up), wb_spec(up), wb_spec(down)],
        out_specs=pl.BlockSpec((MOE_TM, D_MODEL), lambda i, ea, eb, nv_: (i, 0)),
        scratch_shapes=[pltpu.VMEM((4, MOE_TM, D_EXPERT), F32), pltpu.VMEM((2, MOE_TM, D_EXPERT), BF16)],
    )
    return pl.pallas_call(
        _expert_kernel,
        grid_spec=grid_spec,
        out_shape=jax.ShapeDtypeStruct((n_tiles * MOE_TM, D_MODEL), F32),
        compiler_params=_params(("arbitrary",)),
        name="moe_experts",
    )(tile_ea, tile_eb, nv.reshape(1), xs, w_gate, w_up, w_down, w_gate, w_up, w_down)


def _ple_kernel(slot_ref, slotn_ref, x1_ref, p_ref, wproj_ref, wgate_ref, gple_ref, gfin_ref, ys_hbm,
                y_ref, ybuf, sem, *, tm):
    i = pl.program_id(0)

    def row_copy(idx_ref, k, r, s):
        return pltpu.make_async_copy(ys_hbm.at[pl.ds(idx_ref[0, 0, k * tm + r], 1)], ybuf.at[s, pl.ds(r, 1)],
                                     sem.at[s])

    def for_rows(fn):
        def body(r, c):
            fn(r)
            return c
        lax.fori_loop(0, tm, body, 0, unroll=8)

    def wait_tile(s):
        for_rows(lambda r: row_copy(slot_ref, 0, 0, s).wait())

    def compute(k, s):
        rows = slice(k * tm, (k + 1) * tm)
        x2 = x1_ref[rows, :] + ybuf[s]
        proj = jnp.dot(p_ref[rows, :].astype(BF16), wproj_ref[...], preferred_element_type=F32)
        gate = jnp.dot(_rms(x2, gple_ref[...]).astype(BF16), wgate_ref[...], preferred_element_type=F32)
        x3 = x2 + proj * jax.nn.sigmoid(gate)
        y_ref[rows, :] = _rms(x3, gfin_ref[...])

    @pl.when(i == 0)
    def _():
        for_rows(lambda r: row_copy(slot_ref, 0, r, 0).start())

    wait_tile(0)
    for r in range(tm):
        row_copy(slot_ref, 1, r, 1).start()
    compute(0, 0)
    wait_tile(1)
    for r in range(tm):
        row_copy(slotn_ref, 0, r, 0).start()
    compute(1, 1)

    @pl.when(i == pl.num_programs(0) - 1)
    def _():
        wait_tile(0)


def _ple(x1, ys, slot, p2d, lw, g_final, *, tm):
    n = x1.shape[0]
    ns = n // (2 * tm)
    assert n == ns * 2 * tm
    slot3 = slot.reshape(ns, 1, 2 * tm)
    row = lambda i: (i, 0)
    const = lambda i: (0, 0)
    idx_spec = lambda f: pl.BlockSpec((1, 1, 2 * tm), f, memory_space=pltpu.SMEM)
    return pl.pallas_call(
        functools.partial(_ple_kernel, tm=tm),
        grid=(ns,),
        in_specs=[
            idx_spec(lambda i: (i, 0, 0)), idx_spec(lambda i: (jnp.minimum(i + 1, ns - 1), 0, 0)),
            pl.BlockSpec((2 * tm, D_MODEL), row), pl.BlockSpec((2 * tm, D_PLE), row),
            pl.BlockSpec((D_PLE, D_MODEL), const), pl.BlockSpec((D_MODEL, D_MODEL), const),
            pl.BlockSpec((1, D_MODEL), const), pl.BlockSpec((1, D_MODEL), const),
            pl.BlockSpec(memory_space=pl.ANY),
        ],
        out_specs=pl.BlockSpec((2 * tm, D_MODEL), row),
        out_shape=jax.ShapeDtypeStruct((n, D_MODEL), F32),
        scratch_shapes=[pltpu.VMEM((2, tm, D_MODEL), F32), pltpu.SemaphoreType.DMA((2,))],
        compiler_params=_params(("arbitrary",)),
        name="ple",
    )(slot3, slot3, x1, p2d, lw["w_ple_proj"], lw["w_ple_gate"], lw["g_ple"], g_final.reshape(1, D_MODEL), ys)


def _layer_weights(i, norm_mix_g, w_in, ret_norm_g, w_proj_a, w_proj_b, w_out, norm_ffn_g,
                   w_router_group, b_router_group, w_router_expert, b_router_expert,
                   w_gate_e, w_up_e, w_down_e, norm_ple_g, w_ple_gate, w_ple_proj):
    w = w_in[i]
    o = _IN_OFFS
    cols = lambda k: w[:, o[k]:o[k + 1]]
    w_main = jnp.concatenate([cols(0), cols(1), cols(2), cols(3), cols(5), cols(6)], axis=1).astype(BF16)
    wr = jnp.zeros((ROUTER_ROWS, D_MODEL), F32)
    wr = wr.at[:N_GROUPS].set(w_router_group[i].T).at[8:].set(w_router_expert[i].T)
    br = jnp.full((ROUTER_ROWS,), NEG_BIG, F32)
    br = br.at[:N_GROUPS].set(b_router_group[i].astype(F32)).at[8:].set(b_router_expert[i].astype(F32))
    wr_hi, wr_lo = _split_bf16(wr)
    return {
        "g_mix": norm_mix_g[i].reshape(1, D_MODEL), "w_main": w_main, "w_kt": cols(4).T.astype(BF16),
        "w_qvt": jnp.concatenate([cols(0).T, cols(2).T], axis=0).astype(BF16),
        "w_ga": cols(7).astype(BF16), "w_gb": cols(8).astype(BF16),
        "ret_norm_g": ret_norm_g[i], "w_pa": w_proj_a[i].astype(BF16), "w_pb": w_proj_b[i].astype(BF16),
        "w_out": w_out[i].astype(BF16), "g_ffn": norm_ffn_g[i].reshape(1, D_MODEL),
        "wr_hi": wr_hi, "wr_lo": wr_lo, "b_r": br.reshape(ROUTER_ROWS, 1),
        "w_gate": w_gate_e[i].astype(BF16), "w_up": w_up_e[i].astype(BF16), "w_down": w_down_e[i].astype(BF16),
        "g_ple": norm_ple_g[i].reshape(1, D_MODEL), "w_ple_gate": w_ple_gate[i].astype(BF16),
        "w_ple_proj": w_ple_proj[i].astype(BF16),
    }


def _moe(hx_p, cls_p, hx_s, cls_s, lw):
    n_p, n_s = hx_p.shape[0], hx_s.shape[0]
    n_real = n_p + n_s
    n_pad = pl.cdiv(n_p + PLAN_T, RANK_T) * RANK_T
    cls_all = jnp.concatenate([cls_p, cls_s, jnp.zeros((8, n_pad - n_real), jnp.int32)], axis=1)
    slot, pends = _routing_plan(cls_all, n_real)
    n_slots = (pl.cdiv(n_real, MOE_TM) + N_CLASSES) * MOE_TM
    xs = _dispatch(slot, hx_p, hx_s, n_slots)
    ys = _experts(xs, pends, lw["w_gate"], lw["w_up"], lw["w_down"])
    return ys, slot[:n_p], slot[n_p:n_real]


def kernel(x_prompt, x_sample, cache_k_a, cache_v_a, state_ret, p_prompt, p_sample, norm_mix_g, w_in, rel_bias, ret_norm_g, w_proj_a, w_proj_b, w_out, norm_ffn_g, w_router_group, b_router_group, w_router_expert, b_router_expert, w_gate_e, w_up_e, w_down_e, norm_ple_g, w_ple_gate, w_ple_proj, final_norm_g):
    depth = w_in.shape[0]
    assert depth == 1, "the final norm is fused into the last layer; deeper stacks are not supported"
    bp, sp, _ = x_prompt.shape
    bs, ss, _ = x_sample.shape
    keep = min(WINDOW_A, sp)
    n_cache = cache_k_a.shape[2]
    log_g = jnp.log(1.0 - 2.0 ** (-5.0 - jnp.arange(N_HEADS_B, dtype=F32)))
    i = 0
    lw = _layer_weights(i, norm_mix_g, w_in, ret_norm_g, w_proj_a, w_proj_b, w_out, norm_ffn_g,
                        w_router_group, b_router_group, w_router_expert, b_router_expert,
                        w_gate_e, w_up_e, w_down_e, norm_ple_g, w_ple_gate, w_ple_proj)

    tm = 512
    assert sp % tm == 0 and keep == tm and sp >= ATT_WIN
    t_ret = 128
    xp2 = x_prompt.reshape(bp * sp, D_MODEL)
    qa, ka, va, qb, kbt, vb, gb, ka32, va32 = _project(
        xp2, jnp.arange(sp), lw["g_mix"], lw["w_main"], lw["w_kt"], lw["w_qvt"], tm=tm,
        tiles_per_keep=sp // tm, feature_major_qv=True)
    r3 = lambda a: a.reshape(bp, sp, a.shape[-1])
    att = _attention_prompt(qa, r3(ka), va, rel_bias[i], bp)
    per_tile = tm // t_ret
    tiles_per_b = sp // tm
    b_in, s_prompt = _retention(
        r3(qb), kbt, r3(vb), r3(gb), jnp.zeros((bp, N_HEADS_B, DK_B, DV_B), F32), log_g, lw["ret_norm_g"],
        t=t_ret, kt_index=lambda bi, c: (bi * tiles_per_b + c // per_tile, 0, c % per_tile))
    x1_p, hx_p, cls_p = _merge(xp2, att.reshape(bp * sp, W_A), b_in.reshape(bp * sp, V_B), lw, tm=tm)
    k_a_prompt = ka32.reshape(bp, keep, N_HEADS_A, HEAD_DIM_A)
    v_a_prompt = va32.reshape(bp, keep, N_HEADS_A, HEAD_DIM_A)

    ns = bs * ss
    xs2 = x_sample.reshape(ns, D_MODEL)
    pos_s = jnp.tile(PAST_LEN + jnp.arange(ss), bs)
    qa, ka, va, qb, kbt, vb, gb, ka32, va32 = _project(
        xs2, pos_s, lw["g_mix"], lw["w_main"], lw["w_kt"], lw["w_qvt"], tm=ns, tiles_per_keep=1,
        feature_major_qv=False)
    r3 = lambda a: a.reshape(bs, ss, a.shape[-1])
    k_all = jnp.concatenate([cache_k_a[i].reshape(bs, n_cache, W_A).astype(BF16), r3(ka)], axis=1)
    v_all = jnp.concatenate([cache_v_a[i].reshape(bs, n_cache, W_A).astype(BF16), r3(va)], axis=1)
    att = _attention_sample(r3(qa), k_all, v_all, rel_bias[i], n_cache)
    kbt_s = kbt.reshape(QK_B, bs, ss).transpose(1, 0, 2)
    b_in, s_sample = _retention(
        r3(qb), kbt_s, r3(vb), r3(gb), state_ret[i].astype(F32), log_g, lw["ret_norm_g"],
        t=ss, kt_index=lambda bi, c: (bi, 0, 0))
    x1_s, hx_s, cls_s = _merge(xs2, att.reshape(ns, W_A), b_in.reshape(ns, V_B), lw, tm=ns)

    ys, slot_p, slot_s = _moe(hx_p, cls_p, hx_s, cls_s, lw)
    y_prompt = _ple(x1_p, ys, slot_p, p_prompt[i].reshape(bp * sp, D_PLE), lw, final_norm_g, tm=tm)
    y_sample = _ple(x1_s, ys, slot_s, p_sample[i].reshape(ns, D_PLE), lw, final_norm_g, tm=ns // 2)
    k_a_sample = ka32.reshape(bs, ss, N_HEADS_A, HEAD_DIM_A)
    v_a_sample = va32.reshape(bs, ss, N_HEADS_A, HEAD_DIM_A)

    return (y_prompt.reshape(bp, sp, D_MODEL), y_sample.reshape(bs, ss, D_MODEL),
            k_a_prompt[None], v_a_prompt[None], s_prompt[None],
            k_a_sample[None], v_a_sample[None], s_sample.astype(state_ret.dtype)[None])
```

```python
import functools

import numpy as np
import jax
import jax.numpy as jnp
from jax import lax
from jax.experimental import pallas as pl
from jax.experimental.pallas import tpu as pltpu

F32 = jnp.float32
BF16 = jnp.bfloat16

D_MODEL = 1024
PAST_LEN = 1024
CHUNK = 64
BAND_CHUNKS = 8
WINDOW_A = BAND_CHUNKS * CHUNK
N_HEADS_A = 8
HEAD_DIM_A = 64
W_A = N_HEADS_A * HEAD_DIM_A
REL_CLIP = 128
N_HEADS_B = 4
DK_B = 128
DV_B = 256
QK_B = N_HEADS_B * DK_B
V_B = N_HEADS_B * DV_B
ROPE_BASE = 10000.0
N_GROUPS = 4
EXPERTS_PER_GROUP = 8
N_EXPERTS = N_GROUPS * EXPERTS_PER_GROUP
TOP_K = 2
D_EXPERT = 512
D_PLE = 256
EPS = 1e-6
_IN_SIZES = (W_A, W_A, W_A, QK_B, QK_B, V_B, V_B, D_MODEL, D_MODEL)
_IN_OFFS = tuple(sum(_IN_SIZES[:i]) for i in range(len(_IN_SIZES) + 1))

LANES = 128
ATT_QBLK = 2 * CHUNK
ATT_WIN = (BAND_CHUNKS + 2) * CHUNK
ATT_QPS = 2
ROUTER_ROWS = 8 + N_EXPERTS
NEG_BIG = -1e30
LOG2_E = 1.4426950408889634
PAIRS_PER_GROUP = EXPERTS_PER_GROUP * (EXPERTS_PER_GROUP - 1) // 2
N_CLASSES = N_GROUPS * PAIRS_PER_GROUP
CLASS_ROWS = 128
HX_W = D_MODEL + LANES
PLAN_T = 512
RANK_T = 1024
MOE_TM = 256
VMEM_LIMIT = 56 * 1024 * 1024


def _params(sem):
    return pltpu.CompilerParams(dimension_semantics=sem, vmem_limit_bytes=VMEM_LIMIT)


def _rms(x, g):
    return x * lax.rsqrt(jnp.mean(x * x, axis=-1, keepdims=True) + EPS) * g


def _proj_kernel(x_ref, g_ref, w_ref, wkt_ref, wqvt_ref, cos_ref, sin_ref, cost_ref, sint_ref,
                 qa_ref, ka_ref, va_ref, qb_ref, kbt_ref, vb_ref, gb_ref, ka32_ref, va32_ref,
                 *, tiles_per_keep, feature_major_qv):
    h = _rms(x_ref[...], g_ref[...]).astype(BF16)
    nt_dims = (((1,), (1,)), ((), ()))

    def seg(lo, hi):
        return jnp.dot(h, w_ref[:, lo:hi], preferred_element_type=F32)

    ka = seg(W_A, 2 * W_A)
    ka_ref[...] = ka.astype(BF16)
    q_scale = HEAD_DIM_A ** -0.5
    if feature_major_qv:
        qvt = lax.dot_general(wqvt_ref[...], h, nt_dims, preferred_element_type=F32)
        for c in range(qa_ref.shape[0]):
            cs = slice(c * LANES, (c + 1) * LANES)
            qa_ref[c] = (qvt[:W_A, cs] * (q_scale * LOG2_E)).astype(BF16)
            va_ref[c] = qvt[W_A:, cs].astype(BF16)
    else:
        qa_ref[...] = (seg(0, W_A) * q_scale).astype(BF16)
        va_ref[...] = seg(2 * W_A, 3 * W_A).astype(BF16)

    @pl.when(pl.program_id(0) % tiles_per_keep == tiles_per_keep - 1)
    def _():
        ka32_ref[...] = ka
        va32_ref[...] = seg(2 * W_A, 3 * W_A)

    qb = seg(3 * W_A, 3 * W_A + QK_B)
    cos = cos_ref[...]
    sin = sin_ref[...]
    for hd in range(N_HEADS_B):
        xh = qb[:, hd * DK_B:(hd + 1) * DK_B]
        qb_ref[:, hd * DK_B:(hd + 1) * DK_B] = (xh * cos + pltpu.roll(xh, DK_B // 2, axis=1) * sin).astype(BF16)

    vb_ref[...] = seg(3 * W_A + QK_B, 3 * W_A + QK_B + V_B).astype(BF16)
    gb_ref[...] = seg(3 * W_A + QK_B + V_B, 3 * W_A + QK_B + 2 * V_B)

    kt = lax.dot_general(wkt_ref[...], h, (((1,), (1,)), ((), ())), preferred_element_type=F32)
    cost = cost_ref[...]
    sint = sint_ref[...]
    half = DK_B // 2
    scale = DK_B ** -0.5
    for hd in range(N_HEADS_B):
        x1 = kt[hd * DK_B:hd * DK_B + half, :]
        x2 = kt[hd * DK_B + half:(hd + 1) * DK_B, :]
        kbt_ref[0, hd * DK_B:hd * DK_B + half, :] = (x1 * cost - x2 * sint) * scale
        kbt_ref[0, hd * DK_B + half:(hd + 1) * DK_B, :] = (x2 * cost + x1 * sint) * scale


def _rope_tables(pos):
    half = DK_B // 2
    freqs = ROPE_BASE ** (-jnp.arange(half, dtype=F32) / half)
    ang = pos.astype(F32)[:, None] * freqs[None, :]
    cos = jnp.cos(ang)
    sin = jnp.sin(ang)
    return (jnp.concatenate([cos, cos], axis=1), jnp.concatenate([-sin, sin], axis=1), cos.T, sin.T)


def _project(x2d, pos_rows, g_norm, w_main, w_kt, w_qvt, *, tm, tiles_per_keep, feature_major_qv):
    n = x2d.shape[0]
    period = pos_rows.shape[0]
    nt = n // tm
    ppt = period // tm
    cos2, sin2, cost, sint = _rope_tables(pos_rows)
    n_keep = n // tiles_per_keep
    row = lambda i: (i, 0)
    const = lambda i: (0, 0)
    if feature_major_qv:
        spt = tm // LANES
        qv_shape = jax.ShapeDtypeStruct((n // LANES, W_A, LANES), BF16)
        qv_spec = pl.BlockSpec((spt, W_A, LANES), lambda i: (i, 0, 0))
    else:
        qv_shape = jax.ShapeDtypeStruct((n, W_A), BF16)
        qv_spec = pl.BlockSpec((tm, W_A), row)
    outs = (
        qv_shape, jax.ShapeDtypeStruct((n, W_A), BF16), qv_shape, jax.ShapeDtypeStruct((n, QK_B), BF16),
        jax.ShapeDtypeStruct((nt, QK_B, tm), F32), jax.ShapeDtypeStruct((n, V_B), BF16),
        jax.ShapeDtypeStruct((n, V_B), F32),
        jax.ShapeDtypeStruct((n_keep, W_A), F32), jax.ShapeDtypeStruct((n_keep, W_A), F32),
    )
    keep_spec = pl.BlockSpec((tm, W_A), lambda i: (i // tiles_per_keep, 0))
    return pl.pallas_call(
        functools.partial(_proj_kernel, tiles_per_keep=tiles_per_keep, feature_major_qv=feature_major_qv),
        grid=(nt,),
        in_specs=[
            pl.BlockSpec((tm, D_MODEL), row),
            pl.BlockSpec((1, D_MODEL), const),
            pl.BlockSpec(w_main.shape, const),
            pl.BlockSpec(w_kt.shape, const),
            pl.BlockSpec(w_qvt.shape, const),
            pl.BlockSpec((tm, DK_B), lambda i: (i % ppt, 0)),
            pl.BlockSpec((tm, DK_B), lambda i: (i % ppt, 0)),
            pl.BlockSpec((DK_B // 2, tm), lambda i: (0, i % ppt)),
            pl.BlockSpec((DK_B // 2, tm), lambda i: (0, i % ppt)),
        ],
        out_specs=(
            qv_spec, pl.BlockSpec((tm, W_A), row), qv_spec,
            pl.BlockSpec((tm, QK_B), row), pl.BlockSpec((1, QK_B, tm), lambda i: (i, 0, 0)),
            pl.BlockSpec((tm, V_B), row), pl.BlockSpec((tm, V_B), row), keep_spec, keep_spec,
        ),
        out_shape=outs,
        compiler_params=_params(("arbitrary",)),
        name="proj",
    )(x2d, g_norm.reshape(1, D_MODEL), w_main, w_kt, w_qvt, cos2, sin2, cost, sint)


def _attend_pairs(q_of, k_of, v_of, bias_of, store):
    for hp in range(N_HEADS_A // 2):
        qp = q_of(hp)
        kw = k_of(hp)
        vw = v_of(hp)
        lane = lax.broadcasted_iota(jnp.int32, qp.shape, 1)
        outs = []
        for hh in range(2):
            in_head = (lane >= hh * HEAD_DIM_A) & (lane < (hh + 1) * HEAD_DIM_A)
            qh = jnp.where(in_head, qp, jnp.zeros_like(qp))
            s = lax.dot_general(qh, kw, (((1,), (1,)), ((), ())), preferred_element_type=F32)
            s = s + bias_of(2 * hp + hh)
            m = jnp.max(s, axis=-1, keepdims=True)
            p = jnp.exp(s - m)
            l = jnp.sum(p, axis=-1, keepdims=True)
            o = jnp.dot(p.astype(BF16), vw, preferred_element_type=F32)
            outs.append(o / l)
        lane_o = lax.broadcasted_iota(jnp.int32, outs[0].shape, 1)
        store(hp, jnp.where(lane_o < HEAD_DIM_A, outs[0], outs[1]).astype(BF16))


def _fold_rows(x, op, reduce_rows):
    r = x.shape[0]
    while r % 16 == 0:
        r //= 2
        x = op(x[:r], x[r:])
    parts = [x[a:a + 8] for a in range(0, r, 8)]
    while len(parts) > 1:
        parts = [op(parts[a], parts[a + 1]) if a + 1 < len(parts) else parts[a] for a in range(0, len(parts), 2)]
    return reduce_rows(parts[0], axis=0, keepdims=True)


def _attn_prompt_kernel(qt_ref, k_ref, vt_ref, *refs):
    bias_refs = refs[:ATT_QPS]
    o_ref, s_scr, p_scr = refs[ATT_QPS:]
    n_slab = ATT_WIN // LANES
    n_pairs = N_HEADS_A // 2
    units = [(qb, hp) for qb in range(ATT_QPS) for hp in range(n_pairs)]
    first = [jnp.maximum(pl.program_id(1) * ATT_QPS + qb - BAND_CHUNKS // 2, 0) for qb in range(ATT_QPS)]
    rows = lambda hp: slice(hp * LANES, (hp + 1) * LANES)

    def scores(qb, hp):
        qt = qt_ref[qb, rows(hp), :]
        dim = lax.broadcasted_iota(jnp.int32, qt.shape, 0)
        zero = jnp.zeros_like(qt)
        w = jnp.concatenate([jnp.where(dim < HEAD_DIM_A, qt, zero), jnp.where(dim >= HEAD_DIM_A, qt, zero)], axis=1)
        start = pl.multiple_of(first[qb] * ATT_QBLK, ATT_QBLK)
        kw = k_ref[0, pl.ds(start, ATT_WIN), rows(hp)]
        return jnp.dot(kw, w, preferred_element_type=F32)

    for u, (qb, hp) in enumerate(units):
        s_scr[u] = scores(qb, hp) + bias_refs[qb][0, hp]
    denom = []
    for u in range(len(units)):
        s = s_scr[u]
        m = _fold_rows(s, jnp.maximum, jnp.max)
        p = jnp.exp2(s - m)
        denom.append(_fold_rows(p, jnp.add, jnp.sum))
        p_scr[u] = p.astype(BF16)
    for u, (qb, hp) in enumerate(units):
        vt = jnp.concatenate([vt_ref[first[qb] + c, rows(hp), :] for c in range(n_slab)], axis=1)
        ot = jnp.dot(vt, p_scr[u], preferred_element_type=F32) / denom[u]
        odim = lax.broadcasted_iota(jnp.int32, (LANES, LANES), 0)
        o_pair_t = jnp.where(odim < HEAD_DIM_A, ot[:, :LANES], ot[:, LANES:])
        o_ref[0, qb * ATT_QBLK:(qb + 1) * ATT_QBLK, rows(hp)] = o_pair_t.T.astype(BF16)


def _band_bias(table):
    i = np.arange(ATT_QBLK)[None, :]
    jk = np.arange(ATT_WIN)[:, None]
    out = []
    for v in range(BAND_CHUNKS // 2 + 1):
        off_chunks = 2 * v if v < BAND_CHUNKS // 2 else BAND_CHUNKS
        dchunk = (off_chunks + i // CHUNK) - jk // CHUNK
        valid = (dchunk >= 0) & (dchunk <= BAND_CHUNKS)
        n_f = ATT_WIN + ATT_QBLK - 1
        idx = np.clip(off_chunks * CHUNK + (ATT_QBLK - 1) - np.arange(n_f + 1), -REL_CLIP, REL_CLIP) + REL_CLIP
        g = table[:, idx].astype(F32)
        rows = jnp.tile(g, (1, ATT_QBLK))[:, :ATT_QBLK * n_f].reshape(N_HEADS_A, ATT_QBLK, n_f)
        b = rows[:, :, ATT_QBLK - 1:].transpose(0, 2, 1)
        b = jnp.where(valid[None], b * LOG2_E, jnp.float32(NEG_BIG))
        b = b.reshape(N_HEADS_A // 2, 2, ATT_WIN, ATT_QBLK).transpose(0, 2, 1, 3)
        out.append(b.reshape(N_HEADS_A // 2, ATT_WIN, 2 * ATT_QBLK))
    return jnp.stack(out)


def _attention_prompt(qat, ka, vat, table, b):
    s = ka.shape[1]
    nq = s // ATT_QBLK
    ns = nq // ATT_QPS
    assert nq == ns * ATT_QPS
    bias = _band_bias(table)
    nvar = bias.shape[0]
    n_units = ATT_QPS * N_HEADS_A // 2
    bias_specs = [pl.BlockSpec((1, N_HEADS_A // 2, ATT_WIN, 2 * ATT_QBLK),
                               functools.partial(lambda bi, j, qb: (jnp.minimum(j * ATT_QPS + qb, nvar - 1), 0, 0, 0),
                                                 qb=qb)) for qb in range(ATT_QPS)]
    return pl.pallas_call(
        _attn_prompt_kernel,
        grid=(b, ns),
        in_specs=[
            pl.BlockSpec((ATT_QPS, W_A, LANES), lambda bi, j: (bi * ns + j, 0, 0)),
            pl.BlockSpec((1, s, W_A), lambda bi, j: (bi, 0, 0)),
            pl.BlockSpec((nq, W_A, LANES), lambda bi, j: (bi, 0, 0)),
        ] + bias_specs,
        out_specs=pl.BlockSpec((1, ATT_QPS * ATT_QBLK, W_A), lambda bi, j: (bi, j, 0)),
        out_shape=jax.ShapeDtypeStruct((b, s, W_A), BF16),
        scratch_shapes=[pltpu.VMEM((n_units, ATT_WIN, 2 * ATT_QBLK), F32),
                        pltpu.VMEM((n_units, ATT_WIN, 2 * ATT_QBLK), BF16)],
        compiler_params=_params(("arbitrary", "arbitrary")),
        name="attn_prompt",
    )(qat, ka, vat, *([bias] * ATT_QPS))


def _attn_sample_kernel(q_ref, k_ref, v_ref, bias_ref, o_ref):
    def sl(hp):
        return slice(hp * LANES, (hp + 1) * LANES)

    def store(hp, val):
        o_ref[0, :, sl(hp)] = val

    _attend_pairs(
        lambda hp: q_ref[0, :, sl(hp)],
        lambda hp: k_ref[0, :, sl(hp)],
        lambda hp: v_ref[0, :, sl(hp)],
        lambda hd: bias_ref[hd],
        store)


def _attention_sample(qa, k_all, v_all, table, n_cache):
    b, n, _ = qa.shape
    nk = k_all.shape[1]
    dist = jnp.arange(n)[:, None] + n_cache - jnp.arange(nk)[None, :]
    bias = table[:, jnp.clip(dist, -REL_CLIP, REL_CLIP) + REL_CLIP].astype(F32)
    return pl.pallas_call(
        _attn_sample_kernel,
        grid=(b,),
        in_specs=[
            pl.BlockSpec((1, n, W_A), lambda bi: (bi, 0, 0)),
            pl.BlockSpec((1, nk, W_A), lambda bi: (bi, 0, 0)),
            pl.BlockSpec((1, nk, W_A), lambda bi: (bi, 0, 0)),
            pl.BlockSpec((N_HEADS_A, n, nk), lambda bi: (0, 0, 0)),
        ],
        out_specs=pl.BlockSpec((1, n, W_A), lambda bi: (bi, 0, 0)),
        out_shape=jax.ShapeDtypeStruct((b, n, W_A), BF16),
        compiler_params=_params(("arbitrary",)),
        name="attn_sample",
    )(qa, k_all, v_all, bias)


def _ret_kernel(gt_ref, q_ref, *refs, nb):
    kt_refs = refs[:nb]
    v_ref, gb_ref, s0_ref, dmask_ref, qd_ref, kd_ref, gn_ref, out_ref, state_ref, s_scr, o_scr = refs[nb:]

    @pl.when(pl.program_id(1) == 0)
    def _():
        state_ref[...] = s0_ref[...]

    units = [(bb, hd) for bb in range(nb) for hd in range(N_HEADS_B)]
    qs = lambda hd: slice(hd * DK_B, (hd + 1) * DK_B)
    vs = lambda hd: slice(hd * DV_B, (hd + 1) * DV_B)

    for u, (bb, hd) in enumerate(units):
        q = q_ref[bb, :, qs(hd)]
        kt = kt_refs[bb][0, qs(hd), :]
        s_scr[u] = (jnp.dot(q, kt.astype(BF16), preferred_element_type=F32) * dmask_ref[hd]).astype(BF16)
        o_scr[u] = jnp.dot(q, state_ref[bb, hd].astype(BF16), preferred_element_type=F32) * qd_ref[hd]
    for u, (bb, hd) in enumerate(units):
        v = v_ref[bb, :, vs(hd)]
        o_scr[u] = o_scr[u] + jnp.dot(s_scr[u], v, preferred_element_type=F32)
        kd = (kt_refs[bb][0, qs(hd), :] * kd_ref[hd]).astype(BF16)
        state_ref[bb, hd] = state_ref[bb, hd] * gt_ref[hd] + jnp.dot(kd, v, preferred_element_type=F32)
    for u, (bb, hd) in enumerate(units):
        o = o_scr[u]
        mu = jnp.mean(o, axis=-1, keepdims=True)
        var = jnp.mean(jnp.square(o - mu), axis=-1, keepdims=True)
        rb = (o - mu) * lax.rsqrt(var + EPS) * gn_ref[:, vs(hd)]
        gb = gb_ref[bb, :, vs(hd)]
        out_ref[bb, :, vs(hd)] = (gb * jax.nn.sigmoid(gb) * rb).astype(BF16)


def _retention(qb, kbt, vb, gb, state0, log_g, ret_norm_g, *, t, kt_index, nb=4):
    b, s, _ = qb.shape
    assert b % nb == 0
    nc = s // t
    idx = jnp.arange(t, dtype=F32)
    diff = idx[:, None] - idx[None, :]
    dmask = jnp.where(diff[None] >= 0, jnp.exp(log_g[:, None, None] * jnp.maximum(diff, 0.0)[None]), 0.0)
    q_decay = jnp.exp(log_g[:, None] * (idx[None, :] + 1.0))
    k_decay = jnp.exp(log_g[:, None] * (t - 1.0 - idx[None, :]))
    g_t = jnp.exp(log_g * t)
    qd = jnp.broadcast_to(q_decay[:, :, None], (N_HEADS_B, t, DV_B))
    kd = k_decay[:, None, :]
    const3 = lambda bi, c: (0, 0, 0)
    seq = lambda w: pl.BlockSpec((nb, t, w), lambda bi, c: (bi, c, 0))
    state_spec = pl.BlockSpec((nb, N_HEADS_B, DK_B, DV_B), lambda bi, c: (bi, 0, 0, 0))
    kt_specs = [pl.BlockSpec((1, QK_B, t), functools.partial(lambda bi, c, k: kt_index(nb * bi + k, c), k=k))
                for k in range(nb)]
    return pl.pallas_call(
        functools.partial(_ret_kernel, nb=nb),
        grid=(b // nb, nc),
        in_specs=[pl.BlockSpec(memory_space=pltpu.SMEM), seq(QK_B)] + kt_specs + [
            seq(V_B), seq(V_B), state_spec,
            pl.BlockSpec((N_HEADS_B, t, t), const3),
            pl.BlockSpec((N_HEADS_B, t, DV_B), const3),
            pl.BlockSpec((N_HEADS_B, 1, t), const3),
            pl.BlockSpec((1, V_B), lambda bi, c: (0, 0)),
        ],
        out_specs=(seq(V_B), state_spec),
        out_shape=(jax.ShapeDtypeStruct((b, s, V_B), BF16),
                   jax.ShapeDtypeStruct((b, N_HEADS_B, DK_B, DV_B), F32)),
        scratch_shapes=[pltpu.VMEM((nb * N_HEADS_B, t, t), BF16), pltpu.VMEM((nb * N_HEADS_B, t, DV_B), F32)],
        compiler_params=_params(("arbitrary", "arbitrary")),
        name="retention",
    )(g_t, qb, *([kbt] * nb), vb, gb, state0, dmask, qd, kd, ret_norm_g.reshape(1, V_B))


def _split_bf16(x):
    hi = x.astype(BF16)
    lo = (x - hi.astype(F32)).astype(BF16)
    return hi, lo


def _merge_kernel(x_ref, att_ref, bin_ref, gmix_ref, wgab_ref, wpa_ref, wpb_ref, wout_ref,
                  gffn_ref, wr_hi_ref, wr_lo_ref, br_ref, x1_ref, hx_ref, cls_ref):
    x = x_ref[...]
    h = _rms(x, gmix_ref[...]).astype(BF16)
    gates = jnp.dot(h, wgab_ref[...], preferred_element_type=F32)
    gate_a = gates[:, :D_MODEL]
    gate_b = gates[:, D_MODEL:]
    a = jnp.dot(att_ref[...], wpa_ref[...], preferred_element_type=F32)
    b = jnp.dot(bin_ref[...], wpb_ref[...], preferred_element_type=F32)
    m = jax.nn.sigmoid(gate_a) * a + jax.nn.sigmoid(gate_b) * b
    x1 = x + jnp.dot(m.astype(BF16), wout_ref[...], preferred_element_type=F32)
    x1_ref[...] = x1
    h2 = _rms(x1, gffn_ref[...])
    hx_ref[:, :D_MODEL] = h2

    h_hi, h_lo = _split_bf16(h2)
    nt = (((1,), (1,)), ((), ()))
    lt = (lax.dot_general(wr_hi_ref[...], h_hi, nt, preferred_element_type=F32)
          + lax.dot_general(wr_hi_ref[...], h_lo, nt, preferred_element_type=F32)
          + lax.dot_general(wr_lo_ref[...], h_hi, nt, preferred_element_type=F32)) + br_ref[...]
    tm = lt.shape[1]
    row = lax.broadcasted_iota(jnp.int32, (8, tm), 0)
    lg = lt[0:8, :]
    mg = jnp.max(lg, axis=0, keepdims=True)
    grp = jnp.min(jnp.where(lg == mg, row, 8), axis=0, keepdims=True)
    p_grp = 1.0 / jnp.sum(jnp.exp(lg - mg), axis=0, keepdims=True)
    le = jnp.zeros((8, tm), F32)
    for g in range(N_GROUPS):
        le = jnp.where(grp == g, lt[8 + 8 * g:16 + 8 * g, :], le)
    m0 = jnp.max(le, axis=0, keepdims=True)
    i0 = jnp.min(jnp.where(le == m0, row, 8), axis=0, keepdims=True)
    rest = jnp.where(row == i0, jnp.float32(-jnp.inf), le)
    m1 = jnp.max(rest, axis=0, keepdims=True)
    i1 = jnp.min(jnp.where(rest == m1, row, 8), axis=0, keepdims=True)
    e = jnp.exp(m1 - m0)
    w0 = (1.0 / (1.0 + e)) * p_grp
    w1 = (e / (1.0 + e)) * p_grp
    ea = jnp.minimum(i0, i1)
    eb = jnp.maximum(i0, i1)
    pair = ((ea * (2 * EXPERTS_PER_GROUP - 1 - ea)) >> 1) + (eb - ea - 1)
    cls_ref[...] = jnp.where(row == 0, grp * PAIRS_PER_GROUP + pair, 0)
    wa = jnp.where(i0 < i1, w0, w1)
    wb = jnp.where(i0 < i1, w1, w0)
    wrow = lax.broadcasted_iota(jnp.int32, (LANES, tm), 0)
    wslab = jnp.where(wrow == 0, wa, jnp.where(wrow == 1, wb, 0.0))
    hx_ref[:, D_MODEL:] = wslab.T


def _merge(x2d, att, b_in, lw, *, tm):
    n = x2d.shape[0]
    row = lambda i: (i, 0)
    const = lambda i: (0, 0)
    full = lambda a: pl.BlockSpec(a.shape, const)
    return pl.pallas_call(
        _merge_kernel,
        grid=(n // tm,),
        in_specs=[
            pl.BlockSpec((tm, D_MODEL), row), pl.BlockSpec((tm, W_A), row), pl.BlockSpec((tm, V_B), row),
            full(lw["g_mix"]), full(lw["w_gab"]), full(lw["w_pa"]), full(lw["w_pb"]),
            full(lw["w_out"]), full(lw["g_ffn"]), full(lw["wr_hi"]), full(lw["wr_lo"]), full(lw["b_r"]),
        ],
        out_specs=(pl.BlockSpec((tm, D_MODEL), row), pl.BlockSpec((tm, HX_W), row),
                   pl.BlockSpec((8, tm), lambda i: (0, i))),
        out_shape=(jax.ShapeDtypeStruct((n, D_MODEL), F32), jax.ShapeDtypeStruct((n, HX_W), F32),
                   jax.ShapeDtypeStruct((8, n), jnp.int32)),
        compiler_params=_params(("arbitrary",)),
        name="merge",
    )(x2d, att, b_in, lw["g_mix"], lw["w_gab"], lw["w_pa"], lw["w_pb"], lw["w_out"],
      lw["g_ffn"], lw["wr_hi"], lw["wr_lo"], lw["b_r"])


def _class_onehot(cls_row, base, n_real):
    t = cls_row.shape[1]
    crow = lax.broadcasted_iota(jnp.int32, (CLASS_ROWS, t), 0)
    tok = base + lax.broadcasted_iota(jnp.int32, (CLASS_ROWS, t), 1)
    return (cls_row == crow) & (tok < n_real)


def _rank_kernel(cls_ref, tri_ref, rank_ref, counts_ref, *, n_real):
    i = pl.program_id(0)

    @pl.when(i == 0)
    def _():
        counts_ref[...] = jnp.zeros_like(counts_ref)

    t = tri_ref.shape[0]
    hot = _class_onehot(cls_ref[0:1, :], i * t, n_real)
    incl = jnp.dot(jnp.where(hot, 1.0, 0.0).astype(BF16), tri_ref[...], preferred_element_type=F32)
    carry = counts_ref[:, 0:1]
    rank = jnp.sum(jnp.where(hot, incl + carry, 0.0), axis=0, keepdims=True) - 1.0
    rank_ref[...] = jnp.broadcast_to(rank, rank_ref.shape).astype(jnp.int32)
    counts_ref[...] = counts_ref[...] + incl[:, t - 1:t]


def _slot_kernel(cls_ref, rank_ref, pstart_ref, slot_ref, *, n_real):
    t = cls_ref.shape[1]
    hot = _class_onehot(cls_ref[0:1, :], pl.program_id(0) * t, n_real)
    start = jnp.sum(jnp.where(hot, pstart_ref[...], 0.0), axis=0, keepdims=True)
    slot_ref[...] = jnp.broadcast_to(start.astype(jnp.int32) + rank_ref[0:1, :], slot_ref.shape)


def _routing_plan(cls_all, n_real):
    n_pad = cls_all.shape[1]
    nblk = n_pad // RANK_T
    tri = jnp.asarray(np.triu(np.ones((RANK_T, RANK_T), np.float32)), BF16)
    tok = lambda i: (0, i)
    const = lambda i: (0, 0)
    rank, counts = pl.pallas_call(
        functools.partial(_rank_kernel, n_real=n_real),
        grid=(nblk,),
        in_specs=[pl.BlockSpec((8, RANK_T), tok), pl.BlockSpec((RANK_T, RANK_T), const)],
        out_specs=(pl.BlockSpec((8, RANK_T), tok), pl.BlockSpec((CLASS_ROWS, LANES), const)),
        out_shape=(jax.ShapeDtypeStruct((8, n_pad), jnp.int32), jax.ShapeDtypeStruct((CLASS_ROWS, LANES), F32)),
        compiler_params=_params(("arbitrary",)),
        name="moe_rank",
    )(cls_all, tri)
    counts = counts[:, 0].astype(jnp.int32)
    psizes = ((counts + MOE_TM - 1) // MOE_TM) * MOE_TM
    pends = jnp.cumsum(psizes)
    pstart = (pends - psizes).astype(F32).reshape(CLASS_ROWS, 1)
    slot = pl.pallas_call(
        functools.partial(_slot_kernel, n_real=n_real),
        grid=(nblk,),
        in_specs=[pl.BlockSpec((8, RANK_T), tok), pl.BlockSpec((8, RANK_T), tok),
                  pl.BlockSpec((CLASS_ROWS, 1), const)],
        out_specs=pl.BlockSpec((8, RANK_T), tok),
        out_shape=jax.ShapeDtypeStruct((8, n_pad), jnp.int32),
        compiler_params=_params(("arbitrary",)),
        name="moe_slot",
    )(cls_all, rank, pstart)
    return slot[0], pends


def _dispatch_kernel(slot_ref, slots_ref, hxp_ref, hxs_ref, xs_in_hbm, xs_hbm, buf, sem, *, ns_p, n_s):
    del xs_in_hbm
    i = pl.program_id(0)

    def row_copy(idx_ref, k, r, s):
        return pltpu.make_async_copy(buf.at[s, pl.ds(r, 1)],
                                     xs_hbm.at[pl.ds(idx_ref[0, 0, k * PLAN_T + r], 1)], sem.at[s])

    def wait_block(n, s):
        def body(r, c):
            row_copy(slot_ref, 0, 0, s).wait()
            return c
        lax.fori_loop(0, n, body, 0, unroll=8)

    for k in range(2):
        @pl.when(i >= 1)
        def _():
            wait_block(PLAN_T, k)

        @pl.when(i < ns_p)
        def _():
            buf[k] = hxp_ref[k * PLAN_T:(k + 1) * PLAN_T, :]
            for r in range(PLAN_T):
                row_copy(slot_ref, k, r, k).start()

    @pl.when(i == ns_p)
    def _():
        buf[0, :n_s] = hxs_ref[...]
        for r in range(n_s):
            row_copy(slots_ref, 0, r, 0).start()
        wait_block(n_s, 0)


def _dispatch(slot, hx_p, hx_s, n_slots):
    n_p, n_s = hx_p.shape[0], hx_s.shape[0]
    ns_p = n_p // (2 * PLAN_T)
    assert n_p == ns_p * 2 * PLAN_T and ns_p >= 1 and 0 < n_s <= PLAN_T
    slot_p = slot[:n_p].reshape(ns_p, 1, 2 * PLAN_T)
    slot_s = slot[n_p:n_p + PLAN_T].reshape(1, 1, PLAN_T)
    any_spec = pl.BlockSpec(memory_space=pl.ANY)
    last = ns_p - 1
    return pl.pallas_call(
        functools.partial(_dispatch_kernel, ns_p=ns_p, n_s=n_s),
        grid=(ns_p + 1,),
        in_specs=[pl.BlockSpec((1, 1, 2 * PLAN_T), lambda i: (jnp.minimum(i, last), 0, 0), memory_space=pltpu.SMEM),
                  pl.BlockSpec((1, 1, PLAN_T), lambda i: (0, 0, 0), memory_space=pltpu.SMEM),
                  pl.BlockSpec((2 * PLAN_T, HX_W), lambda i: (jnp.minimum(i, last), 0)),
                  pl.BlockSpec((n_s, HX_W), lambda i: (0, 0)), any_spec],
        out_specs=any_spec,
        out_shape=jax.ShapeDtypeStruct((n_slots, HX_W), F32),
        scratch_shapes=[pltpu.VMEM((2, PLAN_T, HX_W), F32), pltpu.SemaphoreType.DMA((2,))],
        input_output_aliases={4: 0},
        compiler_params=_params(("arbitrary",)),
        name="moe_dispatch",
    )(slot_p, slot_s, hx_p, hx_s, jnp.zeros((n_slots, HX_W), F32))


def _expert_kernel(ea_ref, eb_ref, nv_ref, xs_ref, wga_ref, wua_ref, wda_ref, wgb_ref, wub_ref, wdb_ref, ys_ref,
                   gu_scr, act_scr):
    del ea_ref, eb_ref

    @pl.when(pl.program_id(0) < nv_ref[0])
    def _():
        x = xs_ref[:, :D_MODEL].astype(BF16)
        for k, w_ref in enumerate((wga_ref, wua_ref, wgb_ref, wub_ref)):
            gu_scr[k] = jnp.dot(x, w_ref[0], preferred_element_type=F32)
        for k in range(2):
            g = gu_scr[2 * k]
            act_scr[k] = (g * jax.nn.sigmoid(g) * gu_scr[2 * k + 1]).astype(BF16)
        wa = xs_ref[:, D_MODEL:D_MODEL + 1]
        wb = xs_ref[:, D_MODEL + 1:D_MODEL + 2]
        ys_ref[...] = (jnp.dot(act_scr[0], wda_ref[0], preferred_element_type=F32) * wa
                       + jnp.dot(act_scr[1], wdb_ref[0], preferred_element_type=F32) * wb)

    @pl.when(pl.program_id(0) >= nv_ref[0])
    def _():
        ys_ref[...] = jnp.zeros_like(ys_ref)


def _class_experts():
    ea, eb = [], []
    for g in range(N_GROUPS):
        for a in range(EXPERTS_PER_GROUP):
            for b in range(a + 1, EXPERTS_PER_GROUP):
                ea.append(g * EXPERTS_PER_GROUP + a)
                eb.append(g * EXPERTS_PER_GROUP + b)
    return np.asarray(ea, np.int32), np.asarray(eb, np.int32)


def _experts(xs, pends, w_gate, w_up, w_down):
    n_tiles = xs.shape[0] // MOE_TM
    nv = (pends[N_CLASSES - 1] // MOE_TM).astype(jnp.int32)
    tile_cls = jnp.sum(pends[None, :N_CLASSES] <= (jnp.arange(n_tiles) * MOE_TM)[:, None], axis=1)
    tile_cls = jnp.minimum(tile_cls, N_CLASSES - 1)
    cls_ea, cls_eb = _class_experts()
    tile_ea = jnp.asarray(cls_ea)[tile_cls]
    tile_eb = jnp.asarray(cls_eb)[tile_cls]
    wa_spec = lambda shp: pl.BlockSpec((1,) + shp, lambda i, ea, eb, nv_: (ea[i], 0, 0))
    wb_spec = lambda shp: pl.BlockSpec((1,) + shp, lambda i, ea, eb, nv_: (eb[i], 0, 0))
    up, down = (D_MODEL, D_EXPERT), (D_EXPERT, D_MODEL)
    grid_spec = pltpu.PrefetchScalarGridSpec(
        num_scalar_prefetch=3,
        grid=(n_tiles,),
        in_specs=[pl.BlockSpec((MOE_TM, HX_W), lambda i, ea, eb, nv_: (i, 0)),
                  wa_spec(up), wa_spec(up), wa_spec(down), wb_spec(up), wb_spec(up), wb_spec(down)],
        out_specs=pl.BlockSpec((MOE_TM, D_MODEL), lambda i, ea, eb, nv_: (i, 0)),
        scratch_shapes=[pltpu.VMEM((4, MOE_TM, D_EXPERT), F32), pltpu.VMEM((2, MOE_TM, D_EXPERT), BF16)],
    )
    return pl.pallas_call(
        _expert_kernel,
        grid_spec=grid_spec,
        out_shape=jax.ShapeDtypeStruct((n_tiles * MOE_TM, D_MODEL), F32),
        compiler_params=_params(("arbitrary",)),
        name="moe_experts",
    )(tile_ea, tile_eb, nv.reshape(1), xs, w_gate, w_up, w_down, w_gate, w_up, w_down)


def _ple_kernel(slot_ref, slotn_ref, x1_ref, p_ref, wproj_ref, wgate_ref, gple_ref, gfin_ref, ys_hbm,
                y_ref, ybuf, sem, *, tm):
    i = pl.program_id(0)

    def row_copy(idx_ref, k, r, s):
        return pltpu.make_async_copy(ys_hbm.at[pl.ds(idx_ref[0, 0, k * tm + r], 1)], ybuf.at[s, pl.ds(r, 1)],
                                     sem.at[s])

    def for_rows(fn):
        def body(r, c):
            fn(r)
            return c
        lax.fori_loop(0, tm, body, 0, unroll=8)

    def wait_tile(s):
        for_rows(lambda r: row_copy(slot_ref, 0, 0, s).wait())

    def compute(k, s):
        rows = slice(k * tm, (k + 1) * tm)
        x2 = x1_ref[rows, :] + ybuf[s]
        proj = jnp.dot(p_ref[rows, :].astype(BF16), wproj_ref[...], preferred_element_type=F32)
        gate = jnp.dot(_rms(x2, gple_ref[...]).astype(BF16), wgate_ref[...], preferred_element_type=F32)
        x3 = x2 + proj * jax.nn.sigmoid(gate)
        y_ref[rows, :] = _rms(x3, gfin_ref[...])

    @pl.when(i == 0)
    def _():
        for_rows(lambda r: row_copy(slot_ref, 0, r, 0).start())

    wait_tile(0)
    for r in range(tm):
        row_copy(slot_ref, 1, r, 1).start()
    compute(0, 0)
    wait_tile(1)
    for r in range(tm):
        row_copy(slotn_ref, 0, r, 0).start()
    compute(1, 1)

    @pl.when(i == pl.num_programs(0) - 1)
    def _():
        wait_tile(0)


def _ple(x1, ys, slot, p2d, lw, g_final, *, tm):
    n = x1.shape[0]
    ns = n // (2 * tm)
    assert n == ns * 2 * tm
    slot3 = slot.reshape(ns, 1, 2 * tm)
    row = lambda i: (i, 0)
    const = lambda i: (0, 0)
    idx_spec = lambda f: pl.BlockSpec((1, 1, 2 * tm), f, memory_space=pltpu.SMEM)
    return pl.pallas_call(
        functools.partial(_ple_kernel, tm=tm),
        grid=(ns,),
        in_specs=[
            idx_spec(lambda i: (i, 0, 0)), idx_spec(lambda i: (jnp.minimum(i + 1, ns - 1), 0, 0)),
            pl.BlockSpec((2 * tm, D_MODEL), row), pl.BlockSpec((2 * tm, D_PLE), row),
            pl.BlockSpec((D_PLE, D_MODEL), const), pl.BlockSpec((D_MODEL, D_MODEL), const),
            pl.BlockSpec((1, D_MODEL), const), pl.BlockSpec((1, D_MODEL), const),
            pl.BlockSpec(memory_space=pl.ANY),
        ],
        out_specs=pl.BlockSpec((2 * tm, D_MODEL), row),
        out_shape=jax.ShapeDtypeStruct((n, D_MODEL), F32),
        scratch_shapes=[pltpu.VMEM((2, tm, D_MODEL), F32), pltpu.SemaphoreType.DMA((2,))],
        compiler_params=_params(("arbitrary",)),
        name="ple",
    )(slot3, slot3, x1, p2d, lw["w_ple_proj"], lw["w_ple_gate"], lw["g_ple"], g_final.reshape(1, D_MODEL), ys)


def _layer_weights(i, norm_mix_g, w_in, ret_norm_g, w_proj_a, w_proj_b, w_out, norm_ffn_g,
                   w_router_group, b_router_group, w_router_expert, b_router_expert,
                   w_gate_e, w_up_e, w_down_e, norm_ple_g, w_ple_gate, w_ple_proj):
    w = w_in[i]
    o = _IN_OFFS
    cols = lambda k: w[:, o[k]:o[k + 1]]
    w_main = jnp.concatenate([cols(0), cols(1), cols(2), cols(3), cols(5), cols(6)], axis=1).astype(BF16)
    wr = jnp.zeros((ROUTER_ROWS, D_MODEL), F32)
    wr = wr.at[:N_GROUPS].set(w_router_group[i].T).at[8:].set(w_router_expert[i].T)
    br = jnp.full((ROUTER_ROWS,), NEG_BIG, F32)
    br = br.at[:N_GROUPS].set(b_router_group[i].astype(F32)).at[8:].set(b_router_expert[i].astype(F32))
    wr_hi, wr_lo = _split_bf16(wr)
    return {
        "g_mix": norm_mix_g[i].reshape(1, D_MODEL), "w_main": w_main, "w_kt": cols(4).T.astype(BF16),
        "w_qvt": jnp.concatenate([cols(0).T, cols(2).T], axis=0).astype(BF16),
        "w_gab": w[:, o[7]:o[9]].astype(BF16),
        "ret_norm_g": ret_norm_g[i], "w_pa": w_proj_a[i].astype(BF16), "w_pb": w_proj_b[i].astype(BF16),
        "w_out": w_out[i].astype(BF16), "g_ffn": norm_ffn_g[i].reshape(1, D_MODEL),
        "wr_hi": wr_hi, "wr_lo": wr_lo, "b_r": br.reshape(ROUTER_ROWS, 1),
        "w_gate": w_gate_e[i].astype(BF16), "w_up": w_up_e[i].astype(BF16), "w_down": w_down_e[i].astype(BF16),
        "g_ple": norm_ple_g[i].reshape(1, D_MODEL), "w_ple_gate": w_ple_gate[i].astype(BF16),
        "w_ple_proj": w_ple_proj[i].astype(BF16),
    }


def _moe(hx_p, cls_p, hx_s, cls_s, lw):
    n_p, n_s = hx_p.shape[0], hx_s.shape[0]
    n_real = n_p + n_s
    n_pad = pl.cdiv(n_p + PLAN_T, RANK_T) * RANK_T
    cls_all = jnp.concatenate([cls_p, cls_s, jnp.zeros((8, n_pad - n_real), jnp.int32)], axis=1)
    slot, pends = _routing_plan(cls_all, n_real)
    n_slots = (pl.cdiv(n_real, MOE_TM) + N_CLASSES) * MOE_TM
    xs = _dispatch(slot, hx_p, hx_s, n_slots)
    ys = _experts(xs, pends, lw["w_gate"], lw["w_up"], lw["w_down"])
    return ys, slot[:n_p], slot[n_p:n_real]


def kernel(x_prompt, x_sample, cache_k_a, cache_v_a, state_ret, p_prompt, p_sample, norm_mix_g, w_in, rel_bias, ret_norm_g, w_proj_a, w_proj_b, w_out, norm_ffn_g, w_router_group, b_router_group, w_router_expert, b_router_expert, w_gate_e, w_up_e, w_down_e, norm_ple_g, w_ple_gate, w_ple_proj, final_norm_g):
    depth = w_in.shape[0]
    assert depth == 1, "the final norm is fused into the last layer; deeper stacks are not supported"
    bp, sp, _ = x_prompt.shape
    bs, ss, _ = x_sample.shape
    keep = min(WINDOW_A, sp)
    n_cache = cache_k_a.shape[2]
    log_g = jnp.log(1.0 - 2.0 ** (-5.0 - jnp.arange(N_HEADS_B, dtype=F32)))
    i = 0
    lw = _layer_weights(i, norm_mix_g, w_in, ret_norm_g, w_proj_a, w_proj_b, w_out, norm_ffn_g,
                        w_router_group, b_router_group, w_router_expert, b_router_expert,
                        w_gate_e, w_up_e, w_down_e, norm_ple_g, w_ple_gate, w_ple_proj)

    tm = 512
    assert sp % tm == 0 and keep == tm and sp >= ATT_WIN
    t_ret = 128
    xp2 = x_prompt.reshape(bp * sp, D_MODEL)
    qa, ka, va, qb, kbt, vb, gb, ka32, va32 = _project(
        xp2, jnp.arange(sp), lw["g_mix"], lw["w_main"], lw["w_kt"], lw["w_qvt"], tm=tm,
        tiles_per_keep=sp // tm, feature_major_qv=True)
    r3 = lambda a: a.reshape(bp, sp, a.shape[-1])
    att = _attention_prompt(qa, r3(ka), va, rel_bias[i], bp)
    per_tile = tm // t_ret
    tiles_per_b = sp // tm
    b_in, s_prompt = _retention(
        r3(qb), kbt, r3(vb), r3(gb), jnp.zeros((bp, N_HEADS_B, DK_B, DV_B), F32), log_g, lw["ret_norm_g"],
        t=t_ret, kt_index=lambda bi, c: (bi * tiles_per_b + c // per_tile, 0, c % per_tile))
    x1_p, hx_p, cls_p = _merge(xp2, att.reshape(bp * sp, W_A), b_in.reshape(bp * sp, V_B), lw, tm=tm)
    k_a_prompt = ka32.reshape(bp, keep, N_HEADS_A, HEAD_DIM_A)
    v_a_prompt = va32.reshape(bp, keep, N_HEADS_A, HEAD_DIM_A)

    ns = bs * ss
    xs2 = x_sample.reshape(ns, D_MODEL)
    pos_s = jnp.tile(PAST_LEN + jnp.arange(ss), bs)
    qa, ka, va, qb, kbt, vb, gb, ka32, va32 = _project(
        xs2, pos_s, lw["g_mix"], lw["w_main"], lw["w_kt"], lw["w_qvt"], tm=ns, tiles_per_keep=1,
        feature_major_qv=False)
    r3 = lambda a: a.reshape(bs, ss, a.shape[-1])
    k_all = jnp.concatenate([cache_k_a[i].reshape(bs, n_cache, W_A).astype(BF16), r3(ka)], axis=1)
    v_all = jnp.concatenate([cache_v_a[i].reshape(bs, n_cache, W_A).astype(BF16), r3(va)], axis=1)
    att = _attention_sample(r3(qa), k_all, v_all, rel_bias[i], n_cache)
    kbt_s = kbt.reshape(QK_B, bs, ss).transpose(1, 0, 2)
    b_in, s_sample = _retention(
        r3(qb), kbt_s, r3(vb), r3(gb), state_ret[i].astype(F32), log_g, lw["ret_norm_g"],
        t=ss, kt_index=lambda bi, c: (bi, 0, 0))
    x1_s, hx_s, cls_s = _merge(xs2, att.reshape(ns, W_A), b_in.reshape(ns, V_B), lw, tm=ns)

    ys, slot_p, slot_s = _moe(hx_p, cls_p, hx_s, cls_s, lw)
    y_prompt = _ple(x1_p, ys, slot_p, p_prompt[i].reshape(bp * sp, D_PLE), lw, final_norm_g, tm=tm)
    y_sample = _ple(x1_s, ys, slot_s, p_sample[i].reshape(ns, D_PLE), lw, final_norm_g, tm=ns // 2)
    k_a_sample = ka32.reshape(bs, ss, N_HEADS_A, HEAD_DIM_A)
    v_a_sample = va32.reshape(bs, ss, N_HEADS_A, HEAD_DIM_A)

    return (y_prompt.reshape(bp, sp, D_MODEL), y_sample.reshape(bs, ss, D_MODEL),
            k_a_prompt[None], v_a_prompt[None], s_prompt[None],
            k_a_sample[None], v_a_sample[None], s_sample.astype(state_ret.dtype)[None])
```

```python
import functools

import numpy as np
import jax
import jax.numpy as jnp
from jax import lax
from jax.experimental import pallas as pl
from jax.experimental.pallas import tpu as pltpu

F32 = jnp.float32
BF16 = jnp.bfloat16

D_MODEL = 1024
PAST_LEN = 1024
CHUNK = 64
BAND_CHUNKS = 8
WINDOW_A = BAND_CHUNKS * CHUNK
N_HEADS_A = 8
HEAD_DIM_A = 64
W_A = N_HEADS_A * HEAD_DIM_A
REL_CLIP = 128
N_HEADS_B = 4
DK_B = 128
DV_B = 256
QK_B = N_HEADS_B * DK_B
V_B = N_HEADS_B * DV_B
ROPE_BASE = 10000.0
N_GROUPS = 4
EXPERTS_PER_GROUP = 8
N_EXPERTS = N_GROUPS * EXPERTS_PER_GROUP
TOP_K = 2
D_EXPERT = 512
D_PLE = 256
EPS = 1e-6
_IN_SIZES = (W_A, W_A, W_A, QK_B, QK_B, V_B, V_B, D_MODEL, D_MODEL)
_IN_OFFS = tuple(sum(_IN_SIZES[:i]) for i in range(len(_IN_SIZES) + 1))

LANES = 128
ATT_QBLK = 2 * CHUNK
ATT_WIN = (BAND_CHUNKS + 2) * CHUNK
ATT_QPS = 2
ROUTER_ROWS = 8 + N_EXPERTS
NEG_BIG = -1e30
LOG2_E = 1.4426950408889634
PAIRS_PER_GROUP = EXPERTS_PER_GROUP * (EXPERTS_PER_GROUP - 1) // 2
N_CLASSES = N_GROUPS * PAIRS_PER_GROUP
CLASS_ROWS = 128
HX_W = D_MODEL + LANES
PLAN_T = 512
RANK_T = 1024
MOE_TM = 256
VMEM_LIMIT = 56 * 1024 * 1024


def _params(sem):
    return pltpu.CompilerParams(dimension_semantics=sem, vmem_limit_bytes=VMEM_LIMIT)


def _rms(x, g):
    return x * lax.rsqrt(jnp.mean(x * x, axis=-1, keepdims=True) + EPS) * g


def _proj_kernel(x_ref, g_ref, w_ref, wkt_ref, wqvt_ref, cos_ref, sin_ref, cost_ref, sint_ref,
                 qa_ref, ka_ref, va_ref, qb_ref, kbt_ref, vb_ref, gb_ref, ka32_ref, va32_ref,
                 *, tiles_per_keep, feature_major_qv):
    h = _rms(x_ref[...], g_ref[...]).astype(BF16)
    nt_dims = (((1,), (1,)), ((), ()))

    def seg(lo, hi):
        return jnp.dot(h, w_ref[:, lo:hi], preferred_element_type=F32)

    ka = seg(W_A, 2 * W_A)
    ka_ref[...] = ka.astype(BF16)
    q_scale = HEAD_DIM_A ** -0.5
    if feature_major_qv:
        qvt = lax.dot_general(wqvt_ref[...], h, nt_dims, preferred_element_type=F32)
        for c in range(qa_ref.shape[0]):
            cs = slice(c * LANES, (c + 1) * LANES)
            qa_ref[c] = (qvt[:W_A, cs] * (q_scale * LOG2_E)).astype(BF16)
            va_ref[c] = qvt[W_A:, cs].astype(BF16)
    else:
        qa_ref[...] = (seg(0, W_A) * q_scale).astype(BF16)
        va_ref[...] = seg(2 * W_A, 3 * W_A).astype(BF16)

    @pl.when(pl.program_id(0) % tiles_per_keep == tiles_per_keep - 1)
    def _():
        ka32_ref[...] = ka
        va32_ref[...] = seg(2 * W_A, 3 * W_A)

    qb = seg(3 * W_A, 3 * W_A + QK_B)
    cos = cos_ref[...]
    sin = sin_ref[...]
    for hd in range(N_HEADS_B):
        xh = qb[:, hd * DK_B:(hd + 1) * DK_B]
        qb_ref[:, hd * DK_B:(hd + 1) * DK_B] = (xh * cos + pltpu.roll(xh, DK_B // 2, axis=1) * sin).astype(BF16)

    vb_ref[...] = seg(3 * W_A + QK_B, 3 * W_A + QK_B + V_B).astype(BF16)
    gb_ref[...] = seg(3 * W_A + QK_B + V_B, 3 * W_A + QK_B + 2 * V_B)

    kt = lax.dot_general(wkt_ref[...], h, (((1,), (1,)), ((), ())), preferred_element_type=F32)
    cost = cost_ref[...]
    sint = sint_ref[...]
    half = DK_B // 2
    scale = DK_B ** -0.5
    for hd in range(N_HEADS_B):
        x1 = kt[hd * DK_B:hd * DK_B + half, :]
        x2 = kt[hd * DK_B + half:(hd + 1) * DK_B, :]
        kbt_ref[0, hd * DK_B:hd * DK_B + half, :] = (x1 * cost - x2 * sint) * scale
        kbt_ref[0, hd * DK_B + half:(hd + 1) * DK_B, :] = (x2 * cost + x1 * sint) * scale


def _rope_tables(pos):
    half = DK_B // 2
    freqs = ROPE_BASE ** (-jnp.arange(half, dtype=F32) / half)
    ang = pos.astype(F32)[:, None] * freqs[None, :]
    cos = jnp.cos(ang)
    sin = jnp.sin(ang)
    return (jnp.concatenate([cos, cos], axis=1), jnp.concatenate([-sin, sin], axis=1), cos.T, sin.T)


def _project(x2d, pos_rows, g_norm, w_main, w_kt, w_qvt, *, tm, tiles_per_keep, feature_major_qv):
    n = x2d.shape[0]
    period = pos_rows.shape[0]
    nt = n // tm
    ppt = period // tm
    cos2, sin2, cost, sint = _rope_tables(pos_rows)
    n_keep = n // tiles_per_keep
    row = lambda i: (i, 0)
    const = lambda i: (0, 0)
    if feature_major_qv:
        spt = tm // LANES
        qv_shape = jax.ShapeDtypeStruct((n // LANES, W_A, LANES), BF16)
        qv_spec = pl.BlockSpec((spt, W_A, LANES), lambda i: (i, 0, 0))
    else:
        qv_shape = jax.ShapeDtypeStruct((n, W_A), BF16)
        qv_spec = pl.BlockSpec((tm, W_A), row)
    outs = (
        qv_shape, jax.ShapeDtypeStruct((n, W_A), BF16), qv_shape, jax.ShapeDtypeStruct((n, QK_B), BF16),
        jax.ShapeDtypeStruct((nt, QK_B, tm), F32), jax.ShapeDtypeStruct((n, V_B), BF16),
        jax.ShapeDtypeStruct((n, V_B), F32),
        jax.ShapeDtypeStruct((n_keep, W_A), F32), jax.ShapeDtypeStruct((n_keep, W_A), F32),
    )
    keep_spec = pl.BlockSpec((tm, W_A), lambda i: (i // tiles_per_keep, 0))
    return pl.pallas_call(
        functools.partial(_proj_kernel, tiles_per_keep=tiles_per_keep, feature_major_qv=feature_major_qv),
        grid=(nt,),
        in_specs=[
            pl.BlockSpec((tm, D_MODEL), row),
            pl.BlockSpec((1, D_MODEL), const),
            pl.BlockSpec(w_main.shape, const),
            pl.BlockSpec(w_kt.shape, const),
            pl.BlockSpec(w_qvt.shape, const),
            pl.BlockSpec((tm, DK_B), lambda i: (i % ppt, 0)),
            pl.BlockSpec((tm, DK_B), lambda i: (i % ppt, 0)),
            pl.BlockSpec((DK_B // 2, tm), lambda i: (0, i % ppt)),
            pl.BlockSpec((DK_B // 2, tm), lambda i: (0, i % ppt)),
        ],
        out_specs=(
            qv_spec, pl.BlockSpec((tm, W_A), row), qv_spec,
            pl.BlockSpec((tm, QK_B), row), pl.BlockSpec((1, QK_B, tm), lambda i: (i, 0, 0)),
            pl.BlockSpec((tm, V_B), row), pl.BlockSpec((tm, V_B), row), keep_spec, keep_spec,
        ),
        out_shape=outs,
        compiler_params=_params(("arbitrary",)),
        name="proj",
    )(x2d, g_norm.reshape(1, D_MODEL), w_main, w_kt, w_qvt, cos2, sin2, cost, sint)


def _attend_pairs(q_of, k_of, v_of, bias_of, store):
    for hp in range(N_HEADS_A // 2):
        qp = q_of(hp)
        kw = k_of(hp)
        vw = v_of(hp)
        lane = lax.broadcasted_iota(jnp.int32, qp.shape, 1)
        outs = []
        for hh in range(2):
            in_head = (lane >= hh * HEAD_DIM_A) & (lane < (hh + 1) * HEAD_DIM_A)
            qh = jnp.where(in_head, qp, jnp.zeros_like(qp))
            s = lax.dot_general(qh, kw, (((1,), (1,)), ((), ())), preferred_element_type=F32)
            s = s + bias_of(2 * hp + hh)
            m = jnp.max(s, axis=-1, keepdims=True)
            p = jnp.exp(s - m)
            l = jnp.sum(p, axis=-1, keepdims=True)
            o = jnp.dot(p.astype(BF16), vw, preferred_element_type=F32)
            outs.append(o / l)
        lane_o = lax.broadcasted_iota(jnp.int32, outs[0].shape, 1)
        store(hp, jnp.where(lane_o < HEAD_DIM_A, outs[0], outs[1]).astype(BF16))


def _fold_rows(x, op, reduce_rows):
    r = x.shape[0]
    while r % 16 == 0:
        r //= 2
        x = op(x[:r], x[r:])
    parts = [x[a:a + 8] for a in range(0, r, 8)]
    while len(parts) > 1:
        parts = [op(parts[a], parts[a + 1]) if a + 1 < len(parts) else parts[a] for a in range(0, len(parts), 2)]
    return reduce_rows(parts[0], axis=0, keepdims=True)


def _attn_prompt_kernel(qt_ref, k_ref, vt_ref, *refs):
    bias_refs = refs[:ATT_QPS]
    o_ref, s_scr, p_scr = refs[ATT_QPS:]
    n_slab = ATT_WIN // LANES
    n_pairs = N_HEADS_A // 2
    units = [(qb, hp) for qb in range(ATT_QPS) for hp in range(n_pairs)]
    first = [jnp.maximum(pl.program_id(1) * ATT_QPS + qb - BAND_CHUNKS // 2, 0) for qb in range(ATT_QPS)]
    rows = lambda hp: slice(hp * LANES, (hp + 1) * LANES)

    def scores(qb, hp):
        qt = qt_ref[qb, rows(hp), :]
        dim = lax.broadcasted_iota(jnp.int32, qt.shape, 0)
        zero = jnp.zeros_like(qt)
        w = jnp.concatenate([jnp.where(dim < HEAD_DIM_A, qt, zero), jnp.where(dim >= HEAD_DIM_A, qt, zero)], axis=1)
        start = pl.multiple_of(first[qb] * ATT_QBLK, ATT_QBLK)
        kw = k_ref[0, pl.ds(start, ATT_WIN), rows(hp)]
        return jnp.dot(kw, w, preferred_element_type=F32)

    for u, (qb, hp) in enumerate(units):
        s_scr[u] = scores(qb, hp) + bias_refs[qb][0, hp]
    denom = []
    for u in range(len(units)):
        s = s_scr[u]
        m = _fold_rows(s, jnp.maximum, jnp.max)
        p = jnp.exp2(s - m)
        denom.append(_fold_rows(p, jnp.add, jnp.sum))
        p_scr[u] = p.astype(BF16)
    for u, (qb, hp) in enumerate(units):
        vt = jnp.concatenate([vt_ref[first[qb] + c, rows(hp), :] for c in range(n_slab)], axis=1)
        ot = jnp.dot(vt, p_scr[u], preferred_element_type=F32) / denom[u]
        odim = lax.broadcasted_iota(jnp.int32, (LANES, LANES), 0)
        o_pair_t = jnp.where(odim < HEAD_DIM_A, ot[:, :LANES], ot[:, LANES:])
        o_ref[0, qb * ATT_QBLK:(qb + 1) * ATT_QBLK, rows(hp)] = o_pair_t.T.astype(BF16)


def _band_bias(table):
    i = np.arange(ATT_QBLK)[None, :]
    jk = np.arange(ATT_WIN)[:, None]
    out = []
    for v in range(BAND_CHUNKS // 2 + 1):
        off_chunks = 2 * v if v < BAND_CHUNKS // 2 else BAND_CHUNKS
        dchunk = (off_chunks + i // CHUNK) - jk // CHUNK
        valid = (dchunk >= 0) & (dchunk <= BAND_CHUNKS)
        n_f = ATT_WIN + ATT_QBLK - 1
        idx = np.clip(off_chunks * CHUNK + (ATT_QBLK - 1) - np.arange(n_f + 1), -REL_CLIP, REL_CLIP) + REL_CLIP
        g = table[:, idx].astype(F32)
        rows = jnp.tile(g, (1, ATT_QBLK))[:, :ATT_QBLK * n_f].reshape(N_HEADS_A, ATT_QBLK, n_f)
        b = rows[:, :, ATT_QBLK - 1:].transpose(0, 2, 1)
        b = jnp.where(valid[None], b * LOG2_E, jnp.float32(NEG_BIG))
        b = b.reshape(N_HEADS_A // 2, 2, ATT_WIN, ATT_QBLK).transpose(0, 2, 1, 3)
        out.append(b.reshape(N_HEADS_A // 2, ATT_WIN, 2 * ATT_QBLK))
    return jnp.stack(out)


def _attention_prompt(qat, ka, vat, table, b):
    s = ka.shape[1]
    nq = s // ATT_QBLK
    ns = nq // ATT_QPS
    assert nq == ns * ATT_QPS
    bias = _band_bias(table)
    nvar = bias.shape[0]
    n_units = ATT_QPS * N_HEADS_A // 2
    bias_specs = [pl.BlockSpec((1, N_HEADS_A // 2, ATT_WIN, 2 * ATT_QBLK),
                               functools.partial(lambda bi, j, qb: (jnp.minimum(j * ATT_QPS + qb, nvar - 1), 0, 0, 0),
                                                 qb=qb)) for qb in range(ATT_QPS)]
    return pl.pallas_call(
        _attn_prompt_kernel,
        grid=(b, ns),
        in_specs=[
            pl.BlockSpec((ATT_QPS, W_A, LANES), lambda bi, j: (bi * ns + j, 0, 0)),
            pl.BlockSpec((1, s, W_A), lambda bi, j: (bi, 0, 0)),
            pl.BlockSpec((nq, W_A, LANES), lambda bi, j: (bi, 0, 0)),
        ] + bias_specs,
        out_specs=pl.BlockSpec((1, ATT_QPS * ATT_QBLK, W_A), lambda bi, j: (bi, j, 0)),
        out_shape=jax.ShapeDtypeStruct((b, s, W_A), BF16),
        scratch_shapes=[pltpu.VMEM((n_units, ATT_WIN, 2 * ATT_QBLK), F32),
                        pltpu.VMEM((n_units, ATT_WIN, 2 * ATT_QBLK), BF16)],
        compiler_params=_params(("arbitrary", "arbitrary")),
        name="attn_prompt",
    )(qat, ka, vat, *([bias] * ATT_QPS))


def _attn_sample_kernel(q_ref, k_ref, v_ref, bias_ref, o_ref):
    def sl(hp):
        return slice(hp * LANES, (hp + 1) * LANES)

    def store(hp, val):
        o_ref[0, :, sl(hp)] = val

    _attend_pairs(
        lambda hp: q_ref[0, :, sl(hp)],
        lambda hp: k_ref[0, :, sl(hp)],
        lambda hp: v_ref[0, :, sl(hp)],
        lambda hd: bias_ref[hd],
        store)


def _attention_sample(qa, k_all, v_all, table, n_cache):
    b, n, _ = qa.shape
    nk = k_all.shape[1]
    dist = jnp.arange(n)[:, None] + n_cache - jnp.arange(nk)[None, :]
    bias = table[:, jnp.clip(dist, -REL_CLIP, REL_CLIP) + REL_CLIP].astype(F32)
    return pl.pallas_call(
        _attn_sample_kernel,
        grid=(b,),
        in_specs=[
            pl.BlockSpec((1, n, W_A), lambda bi: (bi, 0, 0)),
            pl.BlockSpec((1, nk, W_A), lambda bi: (bi, 0, 0)),
            pl.BlockSpec((1, nk, W_A), lambda bi: (bi, 0, 0)),
            pl.BlockSpec((N_HEADS_A, n, nk), lambda bi: (0, 0, 0)),
        ],
        out_specs=pl.BlockSpec((1, n, W_A), lambda bi: (bi, 0, 0)),
        out_shape=jax.ShapeDtypeStruct((b, n, W_A), BF16),
        compiler_params=_params(("arbitrary",)),
        name="attn_sample",
    )(qa, k_all, v_all, bias)


def _ret_kernel(gt_ref, q_ref, *refs, nb):
    kt_refs = refs[:nb]
    v_ref, gb_ref, s0_ref, dmask_ref, qd_ref, kd_ref, gn_ref, out_ref, state_ref, s_scr, o_scr = refs[nb:]

    @pl.when(pl.program_id(1) == 0)
    def _():
        state_ref[...] = s0_ref[...]

    units = [(bb, hd) for bb in range(nb) for hd in range(N_HEADS_B)]
    qs = lambda hd: slice(hd * DK_B, (hd + 1) * DK_B)
    vs = lambda hd: slice(hd * DV_B, (hd + 1) * DV_B)

    for u, (bb, hd) in enumerate(units):
        q = q_ref[bb, :, qs(hd)]
        kt = kt_refs[bb][0, qs(hd), :]
        s_scr[u] = (jnp.dot(q, kt.astype(BF16), preferred_element_type=F32) * dmask_ref[hd]).astype(BF16)
        o_scr[u] = jnp.dot(q, state_ref[bb, hd].astype(BF16), preferred_element_type=F32) * qd_ref[hd]
    for u, (bb, hd) in enumerate(units):
        v = v_ref[bb, :, vs(hd)]
        o_scr[u] = o_scr[u] + jnp.dot(s_scr[u], v, preferred_element_type=F32)
        kd = (kt_refs[bb][0, qs(hd), :] * kd_ref[hd]).astype(BF16)
        state_ref[bb, hd] = state_ref[bb, hd] * gt_ref[hd] + jnp.dot(kd, v, preferred_element_type=F32)
    for u, (bb, hd) in enumerate(units):
        o = o_scr[u]
        mu = jnp.mean(o, axis=-1, keepdims=True)
        var = jnp.mean(jnp.square(o - mu), axis=-1, keepdims=True)
        rb = (o - mu) * lax.rsqrt(var + EPS) * gn_ref[:, vs(hd)]
        gb = gb_ref[bb, :, vs(hd)]
        out_ref[bb, :, vs(hd)] = (gb * jax.nn.sigmoid(gb) * rb).astype(BF16)


def _retention(qb, kbt, vb, gb, state0, log_g, ret_norm_g, *, t, kt_index, nb=4):
    b, s, _ = qb.shape
    assert b % nb == 0
    nc = s // t
    idx = jnp.arange(t, dtype=F32)
    diff = idx[:, None] - idx[None, :]
    dmask = jnp.where(diff[None] >= 0, jnp.exp(log_g[:, None, None] * jnp.maximum(diff, 0.0)[None]), 0.0)
    q_decay = jnp.exp(log_g[:, None] * (idx[None, :] + 1.0))
    k_decay = jnp.exp(log_g[:, None] * (t - 1.0 - idx[None, :]))
    g_t = jnp.exp(log_g * t)
    qd = jnp.broadcast_to(q_decay[:, :, None], (N_HEADS_B, t, DV_B))
    kd = k_decay[:, None, :]
    const3 = lambda bi, c: (0, 0, 0)
    seq = lambda w: pl.BlockSpec((nb, t, w), lambda bi, c: (bi, c, 0))
    state_spec = pl.BlockSpec((nb, N_HEADS_B, DK_B, DV_B), lambda bi, c: (bi, 0, 0, 0))
    kt_specs = [pl.BlockSpec((1, QK_B, t), functools.partial(lambda bi, c, k: kt_index(nb * bi + k, c), k=k))
                for k in range(nb)]
    return pl.pallas_call(
        functools.partial(_ret_kernel, nb=nb),
        grid=(b // nb, nc),
        in_specs=[pl.BlockSpec(memory_space=pltpu.SMEM), seq(QK_B)] + kt_specs + [
            seq(V_B), seq(V_B), state_spec,
            pl.BlockSpec((N_HEADS_B, t, t), const3),
            pl.BlockSpec((N_HEADS_B, t, DV_B), const3),
            pl.BlockSpec((N_HEADS_B, 1, t), const3),
            pl.BlockSpec((1, V_B), lambda bi, c: (0, 0)),
        ],
        out_specs=(seq(V_B), state_spec),
        out_shape=(jax.ShapeDtypeStruct((b, s, V_B), BF16),
                   jax.ShapeDtypeStruct((b, N_HEADS_B, DK_B, DV_B), F32)),
        scratch_shapes=[pltpu.VMEM((nb * N_HEADS_B, t, t), BF16), pltpu.VMEM((nb * N_HEADS_B, t, DV_B), F32)],
        compiler_params=_params(("arbitrary", "arbitrary")),
        name="retention",
    )(g_t, qb, *([kbt] * nb), vb, gb, state0, dmask, qd, kd, ret_norm_g.reshape(1, V_B))


def _split_bf16(x):
    hi = x.astype(BF16)
    lo = (x - hi.astype(F32)).astype(BF16)
    return hi, lo


def _merge_kernel(x_ref, att_ref, bin_ref, gmix_ref, wgab_ref, wpa_ref, wpb_ref, wout_ref,
                  gffn_ref, wr_hi_ref, wr_lo_ref, br_ref, x1_ref, hx_ref, cls_ref):
    x = x_ref[...]
    h = _rms(x, gmix_ref[...]).astype(BF16)
    gates = jnp.dot(h, wgab_ref[...], preferred_element_type=F32)
    gate_a = gates[:, :D_MODEL]
    gate_b = gates[:, D_MODEL:]
    a = jnp.dot(att_ref[...], wpa_ref[...], preferred_element_type=F32)
    b = jnp.dot(bin_ref[...], wpb_ref[...], preferred_element_type=F32)
    m = jax.nn.sigmoid(gate_a) * a + jax.nn.sigmoid(gate_b) * b
    x1 = x + jnp.dot(m.astype(BF16), wout_ref[...], preferred_element_type=F32)
    x1_ref[...] = x1
    h2 = _rms(x1, gffn_ref[...])
    hx_ref[:, :D_MODEL] = h2

    h_hi, h_lo = _split_bf16(h2)
    nt = (((1,), (1,)), ((), ()))
    lt = (lax.dot_general(wr_hi_ref[...], h_hi, nt, preferred_element_type=F32)
          + lax.dot_general(wr_hi_ref[...], h_lo, nt, preferred_element_type=F32)
          + lax.dot_general(wr_lo_ref[...], h_hi, nt, preferred_element_type=F32)) + br_ref[...]
    tm = lt.shape[1]
    row = lax.broadcasted_iota(jnp.int32, (8, tm), 0)
    lg = lt[0:8, :]
    mg = jnp.max(lg, axis=0, keepdims=True)
    grp = jnp.min(jnp.where(lg == mg, row, 8), axis=0, keepdims=True)
    p_grp = 1.0 / jnp.sum(jnp.exp(lg - mg), axis=0, keepdims=True)
    le = jnp.zeros((8, tm), F32)
    for g in range(N_GROUPS):
        le = jnp.where(grp == g, lt[8 + 8 * g:16 + 8 * g, :], le)
    m0 = jnp.max(le, axis=0, keepdims=True)
    i0 = jnp.min(jnp.where(le == m0, row, 8), axis=0, keepdims=True)
    rest = jnp.where(row == i0, jnp.float32(-jnp.inf), le)
    m1 = jnp.max(rest, axis=0, keepdims=True)
    i1 = jnp.min(jnp.where(rest == m1, row, 8), axis=0, keepdims=True)
    e = jnp.exp(m1 - m0)
    w0 = (1.0 / (1.0 + e)) * p_grp
    w1 = (e / (1.0 + e)) * p_grp
    ea = jnp.minimum(i0, i1)
    eb = jnp.maximum(i0, i1)
    pair = ((ea * (2 * EXPERTS_PER_GROUP - 1 - ea)) >> 1) + (eb - ea - 1)
    cls_ref[...] = jnp.where(row == 0, grp * PAIRS_PER_GROUP + pair, 0)
    wa = jnp.where(i0 < i1, w0, w1)
    wb = jnp.where(i0 < i1, w1, w0)
    wrow = lax.broadcasted_iota(jnp.int32, (LANES, tm), 0)
    wslab = jnp.where(wrow == 0, wa, jnp.where(wrow == 1, wb, 0.0))
    hx_ref[:, D_MODEL:] = wslab.T


def _merge(x2d, att, b_in, lw, *, tm):
    n = x2d.shape[0]
    row = lambda i: (i, 0)
    const = lambda i: (0, 0)
    full = lambda a: pl.BlockSpec(a.shape, const)
    return pl.pallas_call(
        _merge_kernel,
        grid=(n // tm,),
        in_specs=[
            pl.BlockSpec((tm, D_MODEL), row), pl.BlockSpec((tm, W_A), row), pl.BlockSpec((tm, V_B), row),
            full(lw["g_mix"]), full(lw["w_gab"]), full(lw["w_pa"]), full(lw["w_pb"]),
            full(lw["w_out"]), full(lw["g_ffn"]), full(lw["wr_hi"]), full(lw["wr_lo"]), full(lw["b_r"]),
        ],
        out_specs=(pl.BlockSpec((tm, D_MODEL), row), pl.BlockSpec((tm, HX_W), row),
                   pl.BlockSpec((8, tm), lambda i: (0, i))),
        out_shape=(jax.ShapeDtypeStruct((n, D_MODEL), F32), jax.ShapeDtypeStruct((n, HX_W), F32),
                   jax.ShapeDtypeStruct((8, n), jnp.int32)),
        compiler_params=_params(("arbitrary",)),
        name="merge",
    )(x2d, att, b_in, lw["g_mix"], lw["w_gab"], lw["w_pa"], lw["w_pb"], lw["w_out"],
      lw["g_ffn"], lw["wr_hi"], lw["wr_lo"], lw["b_r"])


def _class_onehot(cls_row, base, n_real):
    t = cls_row.shape[1]
    crow = lax.broadcasted_iota(jnp.int32, (CLASS_ROWS, t), 0)
    tok = base + lax.broadcasted_iota(jnp.int32, (CLASS_ROWS, t), 1)
    return (cls_row == crow) & (tok < n_real)


def _rank_kernel(cls_ref, tri_ref, rank_ref, counts_ref, *, n_real):
    i = pl.program_id(0)

    @pl.when(i == 0)
    def _():
        counts_ref[...] = jnp.zeros_like(counts_ref)

    t = tri_ref.shape[0]
    hot = _class_onehot(cls_ref[0:1, :], i * t, n_real)
    incl = jnp.dot(jnp.where(hot, 1.0, 0.0).astype(BF16), tri_ref[...], preferred_element_type=F32)
    carry = counts_ref[:, 0:1]
    rank = jnp.sum(jnp.where(hot, incl + carry, 0.0), axis=0, keepdims=True) - 1.0
    rank_ref[...] = jnp.broadcast_to(rank, rank_ref.shape).astype(jnp.int32)
    counts_ref[...] = counts_ref[...] + incl[:, t - 1:t]


def _slot_kernel(cls_ref, rank_ref, pstart_ref, slot_ref, *, n_real):
    t = cls_ref.shape[1]
    hot = _class_onehot(cls_ref[0:1, :], pl.program_id(0) * t, n_real)
    start = jnp.sum(jnp.where(hot, pstart_ref[...], 0.0), axis=0, keepdims=True)
    slot_ref[...] = jnp.broadcast_to(start.astype(jnp.int32) + rank_ref[0:1, :], slot_ref.shape)


def _routing_plan(cls_all, n_real):
    n_pad = cls_all.shape[1]
    nblk = n_pad // RANK_T
    tri = jnp.asarray(np.triu(np.ones((RANK_T, RANK_T), np.float32)), BF16)
    tok = lambda i: (0, i)
    const = lambda i: (0, 0)
    rank, counts = pl.pallas_call(
        functools.partial(_rank_kernel, n_real=n_real),
        grid=(nblk,),
        in_specs=[pl.BlockSpec((8, RANK_T), tok), pl.BlockSpec((RANK_T, RANK_T), const)],
        out_specs=(pl.BlockSpec((8, RANK_T), tok), pl.BlockSpec((CLASS_ROWS, LANES), const)),
        out_shape=(jax.ShapeDtypeStruct((8, n_pad), jnp.int32), jax.ShapeDtypeStruct((CLASS_ROWS, LANES), F32)),
        compiler_params=_params(("arbitrary",)),
        name="moe_rank",
    )(cls_all, tri)
    counts = counts[:, 0].astype(jnp.int32)
    psizes = ((counts + MOE_TM - 1) // MOE_TM) * MOE_TM
    pends = jnp.cumsum(psizes)
    pstart = (pends - psizes).astype(F32).reshape(CLASS_ROWS, 1)
    slot = pl.pallas_call(
        functools.partial(_slot_kernel, n_real=n_real),
        grid=(nblk,),
        in_specs=[pl.BlockSpec((8, RANK_T), tok), pl.BlockSpec((8, RANK_T), tok),
                  pl.BlockSpec((CLASS_ROWS, 1), const)],
        out_specs=pl.BlockSpec((8, RANK_T), tok),
        out_shape=jax.ShapeDtypeStruct((8, n_pad), jnp.int32),
        compiler_params=_params(("arbitrary",)),
        name="moe_slot",
    )(cls_all, rank, pstart)
    return slot[0], pends, counts


def _dispatch_kernel(slot_ref, slots_ref, hxp_ref, hxs_ref, xs_in_hbm, xs_hbm, buf, sem, *, ns_p, n_s):
    del xs_in_hbm
    i = pl.program_id(0)

    def row_copy(idx_ref, k, r, s):
        return pltpu.make_async_copy(buf.at[s, pl.ds(r, 1)],
                                     xs_hbm.at[pl.ds(idx_ref[0, 0, k * PLAN_T + r], 1)], sem.at[s])

    def wait_block(n, s):
        def body(r, c):
            row_copy(slot_ref, 0, 0, s).wait()
            return c
        lax.fori_loop(0, n, body, 0, unroll=8)

    for k in range(2):
        @pl.when(i >= 1)
        def _():
            wait_block(PLAN_T, k)

        @pl.when(i < ns_p)
        def _():
            buf[k] = hxp_ref[k * PLAN_T:(k + 1) * PLAN_T, :]
            for r in range(PLAN_T):
                row_copy(slot_ref, k, r, k).start()

    @pl.when(i == ns_p)
    def _():
        buf[0, :n_s] = hxs_ref[...]
        for r in range(n_s):
            row_copy(slots_ref, 0, r, 0).start()
        wait_block(n_s, 0)


def _dispatch(slot, hx_p, hx_s, n_slots):
    n_p, n_s = hx_p.shape[0], hx_s.shape[0]
    ns_p = n_p // (2 * PLAN_T)
    assert n_p == ns_p * 2 * PLAN_T and ns_p >= 1 and 0 < n_s <= PLAN_T
    slot_p = slot[:n_p].reshape(ns_p, 1, 2 * PLAN_T)
    slot_s = slot[n_p:n_p + PLAN_T].reshape(1, 1, PLAN_T)
    any_spec = pl.BlockSpec(memory_space=pl.ANY)
    last = ns_p - 1
    return pl.pallas_call(
        functools.partial(_dispatch_kernel, ns_p=ns_p, n_s=n_s),
        grid=(ns_p + 1,),
        in_specs=[pl.BlockSpec((1, 1, 2 * PLAN_T), lambda i: (jnp.minimum(i, last), 0, 0), memory_space=pltpu.SMEM),
                  pl.BlockSpec((1, 1, PLAN_T), lambda i: (0, 0, 0), memory_space=pltpu.SMEM),
                  pl.BlockSpec((2 * PLAN_T, HX_W), lambda i: (jnp.minimum(i, last), 0)),
                  pl.BlockSpec((n_s, HX_W), lambda i: (0, 0)), any_spec],
        out_specs=any_spec,
        out_shape=jax.ShapeDtypeStruct((n_slots, HX_W), F32),
        scratch_shapes=[pltpu.VMEM((2, PLAN_T, HX_W), F32), pltpu.SemaphoreType.DMA((2,))],
        input_output_aliases={4: 0},
        compiler_params=_params(("arbitrary",)),
        name="moe_dispatch",
    )(slot_p, slot_s, hx_p, hx_s, jnp.zeros((n_slots, HX_W), F32))


def _expert_kernel(ea_ref, eb_ref, rows_ref, xs_ref, wga_ref, wua_ref, wda_ref, wgb_ref, wub_ref, wdb_ref, ys_ref,
                   gu_scr, act_scr):
    del ea_ref, eb_ref
    n_rows = rows_ref[pl.program_id(0)]
    half = MOE_TM // 2

    def run(m):
        x = xs_ref[:m, :D_MODEL].astype(BF16)
        for k, w_ref in enumerate((wga_ref, wua_ref, wgb_ref, wub_ref)):
            gu_scr[k, :m] = jnp.dot(x, w_ref[0], preferred_element_type=F32)
        for k in range(2):
            g = gu_scr[2 * k, :m]
            act_scr[k, :m] = (g * jax.nn.sigmoid(g) * gu_scr[2 * k + 1, :m]).astype(BF16)
        wa = xs_ref[:m, D_MODEL:D_MODEL + 1]
        wb = xs_ref[:m, D_MODEL + 1:D_MODEL + 2]
        ys_ref[:m, :] = (jnp.dot(act_scr[0, :m], wda_ref[0], preferred_element_type=F32) * wa
                         + jnp.dot(act_scr[1, :m], wdb_ref[0], preferred_element_type=F32) * wb)

    @pl.when(n_rows > half)
    def _():
        run(MOE_TM)

    @pl.when((n_rows > 0) & (n_rows <= half))
    def _():
        run(half)
        ys_ref[half:, :] = jnp.zeros((MOE_TM - half, D_MODEL), F32)

    @pl.when(n_rows == 0)
    def _():
        ys_ref[...] = jnp.zeros_like(ys_ref)


def _class_experts():
    ea, eb = [], []
    for g in range(N_GROUPS):
        for a in range(EXPERTS_PER_GROUP):
            for b in range(a + 1, EXPERTS_PER_GROUP):
                ea.append(g * EXPERTS_PER_GROUP + a)
                eb.append(g * EXPERTS_PER_GROUP + b)
    return np.asarray(ea, np.int32), np.asarray(eb, np.int32)


def _experts(xs, pends, counts, w_gate, w_up, w_down):
    n_tiles = xs.shape[0] // MOE_TM
    tile_start = jnp.arange(n_tiles, dtype=jnp.int32) * MOE_TM
    tile_cls = jnp.sum(pends[None, :N_CLASSES] <= tile_start[:, None], axis=1)
    tile_cls = jnp.minimum(tile_cls, N_CLASSES - 1)
    used_end = (pends - ((counts + MOE_TM - 1) // MOE_TM) * MOE_TM + counts)[tile_cls]
    tile_rows = jnp.clip(used_end - tile_start, 0, MOE_TM).astype(jnp.int32)
    cls_ea, cls_eb = _class_experts()
    tile_ea = jnp.asarray(cls_ea)[tile_cls]
    tile_eb = jnp.asarray(cls_eb)[tile_cls]
    wa_spec = lambda shp: pl.BlockSpec((1,) + shp, lambda i, ea, eb, nv_: (ea[i], 0, 0))
    wb_spec = lambda shp: pl.BlockSpec((1,) + shp, lambda i, ea, eb, nv_: (eb[i], 0, 0))
    up, down = (D_MODEL, D_EXPERT), (D_EXPERT, D_MODEL)
    grid_spec = pltpu.PrefetchScalarGridSpec(
        num_scalar_prefetch=3,
        grid=(n_tiles,),
        in_specs=[pl.BlockSpec((MOE_TM, HX_W), lambda i, ea, eb, nv_: (i, 0)),
                  wa_spec(up), wa_spec(up), wa_spec(down), wb_spec(up), wb_spec(up), wb_spec(down)],
        out_specs=pl.BlockSpec((MOE_TM, D_MODEL), lambda i, ea, eb, nv_: (i, 0)),
        scratch_shapes=[pltpu.VMEM((4, MOE_TM, D_EXPERT), F32), pltpu.VMEM((2, MOE_TM, D_EXPERT), BF16)],
    )
    return pl.pallas_call(
        _expert_kernel,
        grid_spec=grid_spec,
        out_shape=jax.ShapeDtypeStruct((n_tiles * MOE_TM, D_MODEL), F32),
        compiler_params=_params(("arbitrary",)),
        name="moe_experts",
    )(tile_ea, tile_eb, tile_rows, xs, w_gate, w_up, w_down, w_gate, w_up, w_down)


def _ple_kernel(slot_ref, slotn_ref, x1_ref, p_ref, wproj_ref, wgate_ref, gple_ref, gfin_ref, ys_hbm,
                y_ref, ybuf, sem, *, tm):
    i = pl.program_id(0)

    def row_copy(idx_ref, k, r, s):
        return pltpu.make_async_copy(ys_hbm.at[pl.ds(idx_ref[0, 0, k * tm + r], 1)], ybuf.at[s, pl.ds(r, 1)],
                                     sem.at[s])

    def for_rows(fn):
        def body(r, c):
            fn(r)
            return c
        lax.fori_loop(0, tm, body, 0, unroll=8)

    def wait_tile(s):
        for_rows(lambda r: row_copy(slot_ref, 0, 0, s).wait())

    def compute(k, s):
        rows = slice(k * tm, (k + 1) * tm)
        x2 = x1_ref[rows, :] + ybuf[s]
        proj = jnp.dot(p_ref[rows, :].astype(BF16), wproj_ref[...], preferred_element_type=F32)
        gate = jnp.dot(_rms(x2, gple_ref[...]).astype(BF16), wgate_ref[...], preferred_element_type=F32)
        x3 = x2 + proj * jax.nn.sigmoid(gate)
        y_ref[rows, :] = _rms(x3, gfin_ref[...])

    @pl.when(i == 0)
    def _():
        for_rows(lambda r: row_copy(slot_ref, 0, r, 0).start())

    wait_tile(0)
    for r in range(tm):
        row_copy(slot_ref, 1, r, 1).start()
    compute(0, 0)
    wait_tile(1)
    for r in range(tm):
        row_copy(slotn_ref, 0, r, 0).start()
    compute(1, 1)

    @pl.when(i == pl.num_programs(0) - 1)
    def _():
        wait_tile(0)


def _ple(x1, ys, slot, p2d, lw, g_final, *, tm):
    n = x1.shape[0]
    ns = n // (2 * tm)
    assert n == ns * 2 * tm
    slot3 = slot.reshape(ns, 1, 2 * tm)
    row = lambda i: (i, 0)
    const = lambda i: (0, 0)
    idx_spec = lambda f: pl.BlockSpec((1, 1, 2 * tm), f, memory_space=pltpu.SMEM)
    return pl.pallas_call(
        functools.partial(_ple_kernel, tm=tm),
        grid=(ns,),
        in_specs=[
            idx_spec(lambda i: (i, 0, 0)), idx_spec(lambda i: (jnp.minimum(i + 1, ns - 1), 0, 0)),
            pl.BlockSpec((2 * tm, D_MODEL), row), pl.BlockSpec((2 * tm, D_PLE), row),
            pl.BlockSpec((D_PLE, D_MODEL), const), pl.BlockSpec((D_MODEL, D_MODEL), const),
            pl.BlockSpec((1, D_MODEL), const), pl.BlockSpec((1, D_MODEL), const),
            pl.BlockSpec(memory_space=pl.ANY),
        ],
        out_specs=pl.BlockSpec((2 * tm, D_MODEL), row),
        out_shape=jax.ShapeDtypeStruct((n, D_MODEL), F32),
        scratch_shapes=[pltpu.VMEM((2, tm, D_MODEL), F32), pltpu.SemaphoreType.DMA((2,))],
        compiler_params=_params(("arbitrary",)),
        name="ple",
    )(slot3, slot3, x1, p2d, lw["w_ple_proj"], lw["w_ple_gate"], lw["g_ple"], g_final.reshape(1, D_MODEL), ys)


def _layer_weights(i, norm_mix_g, w_in, ret_norm_g, w_proj_a, w_proj_b, w_out, norm_ffn_g,
                   w_router_group, b_router_group, w_router_expert, b_router_expert,
                   w_gate_e, w_up_e, w_down_e, norm_ple_g, w_ple_gate, w_ple_proj):
    w = w_in[i]
    o = _IN_OFFS
    cols = lambda k: w[:, o[k]:o[k + 1]]
    w_main = jnp.concatenate([cols(0), cols(1), cols(2), cols(3), cols(5), cols(6)], axis=1).astype(BF16)
    wr = jnp.zeros((ROUTER_ROWS, D_MODEL), F32)
    wr = wr.at[:N_GROUPS].set(w_router_group[i].T).at[8:].set(w_router_expert[i].T)
    br = jnp.full((ROUTER_ROWS,), NEG_BIG, F32)
    br = br.at[:N_GROUPS].set(b_router_group[i].astype(F32)).at[8:].set(b_router_expert[i].astype(F32))
    wr_hi, wr_lo = _split_bf16(wr)
    return {
        "g_mix": norm_mix_g[i].reshape(1, D_MODEL), "w_main": w_main, "w_kt": cols(4).T.astype(BF16),
        "w_qvt": jnp.concatenate([cols(0).T, cols(2).T], axis=0).astype(BF16),
        "w_gab": w[:, o[7]:o[9]].astype(BF16),
        "ret_norm_g": ret_norm_g[i], "w_pa": w_proj_a[i].astype(BF16), "w_pb": w_proj_b[i].astype(BF16),
        "w_out": w_out[i].astype(BF16), "g_ffn": norm_ffn_g[i].reshape(1, D_MODEL),
        "wr_hi": wr_hi, "wr_lo": wr_lo, "b_r": br.reshape(ROUTER_ROWS, 1),
        "w_gate": w_gate_e[i].astype(BF16), "w_up": w_up_e[i].astype(BF16), "w_down": w_down_e[i].astype(BF16),
        "g_ple": norm_ple_g[i].reshape(1, D_MODEL), "w_ple_gate": w_ple_gate[i].astype(BF16),
        "w_ple_proj": w_ple_proj[i].astype(BF16),
    }


def _moe(hx_p, cls_p, hx_s, cls_s, lw):
    n_p, n_s = hx_p.shape[0], hx_s.shape[0]
    n_real = n_p + n_s
    n_pad = pl.cdiv(n_p + PLAN_T, RANK_T) * RANK_T
    cls_all = jnp.concatenate([cls_p, cls_s, jnp.zeros((8, n_pad - n_real), jnp.int32)], axis=1)
    slot, pends, counts = _routing_plan(cls_all, n_real)
    n_slots = (pl.cdiv(n_real, MOE_TM) + N_CLASSES) * MOE_TM
    xs = _dispatch(slot, hx_p, hx_s, n_slots)
    ys = _experts(xs, pends, counts, lw["w_gate"], lw["w_up"], lw["w_down"])
    return ys, slot[:n_p], slot[n_p:n_real]


def kernel(x_prompt, x_sample, cache_k_a, cache_v_a, state_ret, p_prompt, p_sample, norm_mix_g, w_in, rel_bias, ret_norm_g, w_proj_a, w_proj_b, w_out, norm_ffn_g, w_router_group, b_router_group, w_router_expert, b_router_expert, w_gate_e, w_up_e, w_down_e, norm_ple_g, w_ple_gate, w_ple_proj, final_norm_g):
    depth = w_in.shape[0]
    assert depth == 1, "the final norm is fused into the last layer; deeper stacks are not supported"
    bp, sp, _ = x_prompt.shape
    bs, ss, _ = x_sample.shape
    keep = min(WINDOW_A, sp)
    n_cache = cache_k_a.shape[2]
    log_g = jnp.log(1.0 - 2.0 ** (-5.0 - jnp.arange(N_HEADS_B, dtype=F32)))
    i = 0
    lw = _layer_weights(i, norm_mix_g, w_in, ret_norm_g, w_proj_a, w_proj_b, w_out, norm_ffn_g,
                        w_router_group, b_router_group, w_router_expert, b_router_expert,
                        w_gate_e, w_up_e, w_down_e, norm_ple_g, w_ple_gate, w_ple_proj)

    tm = 512
    assert sp % tm == 0 and keep == tm and sp >= ATT_WIN
    t_ret = 128
    xp2 = x_prompt.reshape(bp * sp, D_MODEL)
    qa, ka, va, qb, kbt, vb, gb, ka32, va32 = _project(
        xp2, jnp.arange(sp), lw["g_mix"], lw["w_main"], lw["w_kt"], lw["w_qvt"], tm=tm,
        tiles_per_keep=sp // tm, feature_major_qv=True)
    r3 = lambda a: a.reshape(bp, sp, a.shape[-1])
    att = _attention_prompt(qa, r3(ka), va, rel_bias[i], bp)
    per_tile = tm // t_ret
    tiles_per_b = sp // tm
    b_in, s_prompt = _retention(
        r3(qb), kbt, r3(vb), r3(gb), jnp.zeros((bp, N_HEADS_B, DK_B, DV_B), F32), log_g, lw["ret_norm_g"],
        t=t_ret, kt_index=lambda bi, c: (bi * tiles_per_b + c // per_tile, 0, c % per_tile))
    x1_p, hx_p, cls_p = _merge(xp2, att.reshape(bp * sp, W_A), b_in.reshape(bp * sp, V_B), lw, tm=tm)
    k_a_prompt = ka32.reshape(bp, keep, N_HEADS_A, HEAD_DIM_A)
    v_a_prompt = va32.reshape(bp, keep, N_HEADS_A, HEAD_DIM_A)

    ns = bs * ss
    xs2 = x_sample.reshape(ns, D_MODEL)
    pos_s = jnp.tile(PAST_LEN + jnp.arange(ss), bs)
    qa, ka, va, qb, kbt, vb, gb, ka32, va32 = _project(
        xs2, pos_s, lw["g_mix"], lw["w_main"], lw["w_kt"], lw["w_qvt"], tm=ns, tiles_per_keep=1,
        feature_major_qv=False)
    r3 = lambda a: a.reshape(bs, ss, a.shape[-1])
    k_all = jnp.concatenate([cache_k_a[i].reshape(bs, n_cache, W_A).astype(BF16), r3(ka)], axis=1)
    v_all = jnp.concatenate([cache_v_a[i].reshape(bs, n_cache, W_A).astype(BF16), r3(va)], axis=1)
    att = _attention_sample(r3(qa), k_all, v_all, rel_bias[i], n_cache)
    kbt_s = kbt.reshape(QK_B, bs, ss).transpose(1, 0, 2)
    b_in, s_sample = _retention(
        r3(qb), kbt_s, r3(vb), r3(gb), state_ret[i].astype(F32), log_g, lw["ret_norm_g"],
        t=ss, kt_index=lambda bi, c: (bi, 0, 0))
    x1_s, hx_s, cls_s = _merge(xs2, att.reshape(ns, W_A), b_in.reshape(ns, V_B), lw, tm=ns)

    ys, slot_p, slot_s = _moe(hx_p, cls_p, hx_s, cls_s, lw)
    y_prompt = _ple(x1_p, ys, slot_p, p_prompt[i].reshape(bp * sp, D_PLE), lw, final_norm_g, tm=tm)
    y_sample = _ple(x1_s, ys, slot_s, p_sample[i].reshape(ns, D_PLE), lw, final_norm_g, tm=ns // 2)
    k_a_sample = ka32.reshape(bs, ss, N_HEADS_A, HEAD_DIM_A)
    v_a_sample = va32.reshape(bs, ss, N_HEADS_A, HEAD_DIM_A)

    return (y_prompt.reshape(bp, sp, D_MODEL), y_sample.reshape(bs, ss, D_MODEL),
            k_a_prompt[None], v_a_prompt[None], s_prompt[None],
            k_a_sample[None], v_a_sample[None], s_sample.astype(state_ret.dtype)[None])
```

```python
import functools

import numpy as np
import jax
import jax.numpy as jnp
from jax import lax
from jax.experimental import pallas as pl
from jax.experimental.pallas import tpu as pltpu

F32 = jnp.float32
BF16 = jnp.bfloat16

D_MODEL = 1024
PAST_LEN = 1024
CHUNK = 64
BAND_CHUNKS = 8
WINDOW_A = BAND_CHUNKS * CHUNK
N_HEADS_A = 8
HEAD_DIM_A = 64
W_A = N_HEADS_A * HEAD_DIM_A
REL_CLIP = 128
N_HEADS_B = 4
DK_B = 128
DV_B = 256
QK_B = N_HEADS_B * DK_B
V_B = N_HEADS_B * DV_B
ROPE_BASE = 10000.0
N_GROUPS = 4
EXPERTS_PER_GROUP = 8
N_EXPERTS = N_GROUPS * EXPERTS_PER_GROUP
TOP_K = 2
D_EXPERT = 512
D_PLE = 256
EPS = 1e-6
_IN_SIZES = (W_A, W_A, W_A, QK_B, QK_B, V_B, V_B, D_MODEL, D_MODEL)
_IN_OFFS = tuple(sum(_IN_SIZES[:i]) for i in range(len(_IN_SIZES) + 1))

LANES = 128
ATT_QBLK = 2 * CHUNK
ATT_WIN = (BAND_CHUNKS + 2) * CHUNK
ATT_QPS = 2
ROUTER_ROWS = 8 + N_EXPERTS
NEG_BIG = -1e30
LOG2_E = 1.4426950408889634
PAIRS_PER_GROUP = EXPERTS_PER_GROUP * (EXPERTS_PER_GROUP - 1) // 2
N_CLASSES = N_GROUPS * PAIRS_PER_GROUP
CLASS_ROWS = 128
HX_W = D_MODEL + LANES
PLAN_T = 512
RANK_T = 1024
MOE_TM = 256
VMEM_LIMIT = 56 * 1024 * 1024


def _params(sem):
    return pltpu.CompilerParams(dimension_semantics=sem, vmem_limit_bytes=VMEM_LIMIT)


def _rms(x, g):
    return x * lax.rsqrt(jnp.mean(x * x, axis=-1, keepdims=True) + EPS) * g


def _proj_kernel(x_ref, g_ref, w_ref, wkt_ref, wqvt_ref, cos_ref, sin_ref, cost_ref, sint_ref,
                 qa_ref, ka_ref, va_ref, qb_ref, kbt_ref, vb_ref, gb_ref, ka32_ref, va32_ref,
                 *, tiles_per_keep, feature_major_qv):
    h = _rms(x_ref[...], g_ref[...]).astype(BF16)
    nt_dims = (((1,), (1,)), ((), ()))

    def seg(lo, hi):
        return jnp.dot(h, w_ref[:, lo:hi], preferred_element_type=F32)

    ka = seg(W_A, 2 * W_A)
    ka_ref[...] = ka.astype(BF16)
    q_scale = HEAD_DIM_A ** -0.5
    if feature_major_qv:
        qvt = lax.dot_general(wqvt_ref[...], h, nt_dims, preferred_element_type=F32)
        for c in range(qa_ref.shape[0]):
            cs = slice(c * LANES, (c + 1) * LANES)
            qa_ref[c] = (qvt[:W_A, cs] * (q_scale * LOG2_E)).astype(BF16)
            va_ref[c] = qvt[W_A:, cs].astype(BF16)
    else:
        qa_ref[...] = (seg(0, W_A) * q_scale).astype(BF16)
        va_ref[...] = seg(2 * W_A, 3 * W_A).astype(BF16)

    @pl.when(pl.program_id(0) % tiles_per_keep == tiles_per_keep - 1)
    def _():
        ka32_ref[...] = ka
        va32_ref[...] = seg(2 * W_A, 3 * W_A)

    qb = seg(3 * W_A, 3 * W_A + QK_B)
    cos = cos_ref[...]
    sin = sin_ref[...]
    for hd in range(N_HEADS_B):
        xh = qb[:, hd * DK_B:(hd + 1) * DK_B]
        qb_ref[:, hd * DK_B:(hd + 1) * DK_B] = (xh * cos + pltpu.roll(xh, DK_B // 2, axis=1) * sin).astype(BF16)

    vb_ref[...] = seg(3 * W_A + QK_B, 3 * W_A + QK_B + V_B).astype(BF16)
    gb_ref[...] = seg(3 * W_A + QK_B + V_B, 3 * W_A + QK_B + 2 * V_B)

    kt = lax.dot_general(wkt_ref[...], h, (((1,), (1,)), ((), ())), preferred_element_type=F32)
    cost = cost_ref[...]
    sint = sint_ref[...]
    half = DK_B // 2
    scale = DK_B ** -0.5
    for hd in range(N_HEADS_B):
        x1 = kt[hd * DK_B:hd * DK_B + half, :]
        x2 = kt[hd * DK_B + half:(hd + 1) * DK_B, :]
        kbt_ref[0, hd * DK_B:hd * DK_B + half, :] = (x1 * cost - x2 * sint) * scale
        kbt_ref[0, hd * DK_B + half:(hd + 1) * DK_B, :] = (x2 * cost + x1 * sint) * scale


def _rope_tables(pos):
    half = DK_B // 2
    freqs = ROPE_BASE ** (-jnp.arange(half, dtype=F32) / half)
    ang = pos.astype(F32)[:, None] * freqs[None, :]
    cos = jnp.cos(ang)
    sin = jnp.sin(ang)
    return (jnp.concatenate([cos, cos], axis=1), jnp.concatenate([-sin, sin], axis=1), cos.T, sin.T)


def _project(x2d, pos_rows, g_norm, w_main, w_kt, w_qvt, *, tm, tiles_per_keep, feature_major_qv):
    n = x2d.shape[0]
    period = pos_rows.shape[0]
    nt = n // tm
    ppt = period // tm
    cos2, sin2, cost, sint = _rope_tables(pos_rows)
    n_keep = n // tiles_per_keep
    row = lambda i: (i, 0)
    const = lambda i: (0, 0)
    if feature_major_qv:
        spt = tm // LANES
        qv_shape = jax.ShapeDtypeStruct((n // LANES, W_A, LANES), BF16)
        qv_spec = pl.BlockSpec((spt, W_A, LANES), lambda i: (i, 0, 0))
    else:
        qv_shape = jax.ShapeDtypeStruct((n, W_A), BF16)
        qv_spec = pl.BlockSpec((tm, W_A), row)
    outs = (
        qv_shape, jax.ShapeDtypeStruct((n, W_A), BF16), qv_shape, jax.ShapeDtypeStruct((n, QK_B), BF16),
        jax.ShapeDtypeStruct((nt, QK_B, tm), F32), jax.ShapeDtypeStruct((n, V_B), BF16),
        jax.ShapeDtypeStruct((n, V_B), F32),
        jax.ShapeDtypeStruct((n_keep, W_A), F32), jax.ShapeDtypeStruct((n_keep, W_A), F32),
    )
    keep_spec = pl.BlockSpec((tm, W_A), lambda i: (i // tiles_per_keep, 0))
    return pl.pallas_call(
        functools.partial(_proj_kernel, tiles_per_keep=tiles_per_keep, feature_major_qv=feature_major_qv),
        grid=(nt,),
        in_specs=[
            pl.BlockSpec((tm, D_MODEL), row),
            pl.BlockSpec((1, D_MODEL), const),
            pl.BlockSpec(w_main.shape, const),
            pl.BlockSpec(w_kt.shape, const),
            pl.BlockSpec(w_qvt.shape, const),
            pl.BlockSpec((tm, DK_B), lambda i: (i % ppt, 0)),
            pl.BlockSpec((tm, DK_B), lambda i: (i % ppt, 0)),
            pl.BlockSpec((DK_B // 2, tm), lambda i: (0, i % ppt)),
            pl.BlockSpec((DK_B // 2, tm), lambda i: (0, i % ppt)),
        ],
        out_specs=(
            qv_spec, pl.BlockSpec((tm, W_A), row), qv_spec,
            pl.BlockSpec((tm, QK_B), row), pl.BlockSpec((1, QK_B, tm), lambda i: (i, 0, 0)),
            pl.BlockSpec((tm, V_B), row), pl.BlockSpec((tm, V_B), row), keep_spec, keep_spec,
        ),
        out_shape=outs,
        compiler_params=_params(("arbitrary",)),
        name="proj",
    )(x2d, g_norm.reshape(1, D_MODEL), w_main, w_kt, w_qvt, cos2, sin2, cost, sint)


def _attend_pairs(q_of, k_of, v_of, bias_of, store):
    for hp in range(N_HEADS_A // 2):
        qp = q_of(hp)
        kw = k_of(hp)
        vw = v_of(hp)
        lane = lax.broadcasted_iota(jnp.int32, qp.shape, 1)
        outs = []
        for hh in range(2):
            in_head = (lane >= hh * HEAD_DIM_A) & (lane < (hh + 1) * HEAD_DIM_A)
            qh = jnp.where(in_head, qp, jnp.zeros_like(qp))
            s = lax.dot_general(qh, kw, (((1,), (1,)), ((), ())), preferred_element_type=F32)
            s = s + bias_of(2 * hp + hh)
            m = jnp.max(s, axis=-1, keepdims=True)
            p = jnp.exp(s - m)
            l = jnp.sum(p, axis=-1, keepdims=True)
            o = jnp.dot(p.astype(BF16), vw, preferred_element_type=F32)
            outs.append(o / l)
        lane_o = lax.broadcasted_iota(jnp.int32, outs[0].shape, 1)
        store(hp, jnp.where(lane_o < HEAD_DIM_A, outs[0], outs[1]).astype(BF16))


def _fold_rows(x, op, reduce_rows):
    r = x.shape[0]
    while r % 16 == 0:
        r //= 2
        x = op(x[:r], x[r:])
    parts = [x[a:a + 8] for a in range(0, r, 8)]
    while len(parts) > 1:
        parts = [op(parts[a], parts[a + 1]) if a + 1 < len(parts) else parts[a] for a in range(0, len(parts), 2)]
    return reduce_rows(parts[0], axis=0, keepdims=True)


def _attn_prompt_kernel(qt_ref, k_ref, vt_ref, *refs):
    bias_refs = refs[:ATT_QPS]
    o_ref, s_scr, p_scr = refs[ATT_QPS:]
    n_slab = ATT_WIN // LANES
    n_pairs = N_HEADS_A // 2
    units = [(qb, hp) for qb in range(ATT_QPS) for hp in range(n_pairs)]
    first = [jnp.maximum(pl.program_id(1) * ATT_QPS + qb - BAND_CHUNKS // 2, 0) for qb in range(ATT_QPS)]
    rows = lambda hp: slice(hp * LANES, (hp + 1) * LANES)

    def scores(qb, hp):
        qt = qt_ref[qb, rows(hp), :]
        dim = lax.broadcasted_iota(jnp.int32, qt.shape, 0)
        zero = jnp.zeros_like(qt)
        w = jnp.concatenate([jnp.where(dim < HEAD_DIM_A, qt, zero), jnp.where(dim >= HEAD_DIM_A, qt, zero)], axis=1)
        start = pl.multiple_of(first[qb] * ATT_QBLK, ATT_QBLK)
        kw = k_ref[0, pl.ds(start, ATT_WIN), rows(hp)]
        return jnp.dot(kw, w, preferred_element_type=F32)

    for u, (qb, hp) in enumerate(units):
        s_scr[u] = scores(qb, hp) + bias_refs[qb][0, hp]
    denom = []
    for u in range(len(units)):
        s = s_scr[u]
        m = _fold_rows(s, jnp.maximum, jnp.max)
        p = jnp.exp2(s - m)
        denom.append(_fold_rows(p, jnp.add, jnp.sum))
        p_scr[u] = p.astype(BF16)
    for u, (qb, hp) in enumerate(units):
        vt = jnp.concatenate([vt_ref[first[qb] + c, rows(hp), :] for c in range(n_slab)], axis=1)
        ot = jnp.dot(vt, p_scr[u], preferred_element_type=F32) / denom[u]
        odim = lax.broadcasted_iota(jnp.int32, (LANES, LANES), 0)
        o_pair_t = jnp.where(odim < HEAD_DIM_A, ot[:, :LANES], ot[:, LANES:])
        o_ref[0, qb * ATT_QBLK:(qb + 1) * ATT_QBLK, rows(hp)] = o_pair_t.T.astype(BF16)


def _band_bias(table):
    i = np.arange(ATT_QBLK)[None, :]
    jk = np.arange(ATT_WIN)[:, None]
    out = []
    for v in range(BAND_CHUNKS // 2 + 1):
        off_chunks = 2 * v if v < BAND_CHUNKS // 2 else BAND_CHUNKS
        dchunk = (off_chunks + i // CHUNK) - jk // CHUNK
        valid = (dchunk >= 0) & (dchunk <= BAND_CHUNKS)
        n_f = ATT_WIN + ATT_QBLK - 1
        idx = np.clip(off_chunks * CHUNK + (ATT_QBLK - 1) - np.arange(n_f + 1), -REL_CLIP, REL_CLIP) + REL_CLIP
        g = table[:, idx].astype(F32)
        rows = jnp.tile(g, (1, ATT_QBLK))[:, :ATT_QBLK * n_f].reshape(N_HEADS_A, ATT_QBLK, n_f)
        b = rows[:, :, ATT_QBLK - 1:].transpose(0, 2, 1)
        b = jnp.where(valid[None], b * LOG2_E, jnp.float32(NEG_BIG))
        b = b.reshape(N_HEADS_A // 2, 2, ATT_WIN, ATT_QBLK).transpose(0, 2, 1, 3)
        out.append(b.reshape(N_HEADS_A // 2, ATT_WIN, 2 * ATT_QBLK))
    return jnp.stack(out)


def _attention_prompt(qat, ka, vat, table, b):
    s = ka.shape[1]
    nq = s // ATT_QBLK
    ns = nq // ATT_QPS
    assert nq == ns * ATT_QPS
    bias = _band_bias(table)
    nvar = bias.shape[0]
    n_units = ATT_QPS * N_HEADS_A // 2
    bias_specs = [pl.BlockSpec((1, N_HEADS_A // 2, ATT_WIN, 2 * ATT_QBLK),
                               functools.partial(lambda bi, j, qb: (jnp.minimum(j * ATT_QPS + qb, nvar - 1), 0, 0, 0),
                                                 qb=qb)) for qb in range(ATT_QPS)]
    return pl.pallas_call(
        _attn_prompt_kernel,
        grid=(b, ns),
        in_specs=[
            pl.BlockSpec((ATT_QPS, W_A, LANES), lambda bi, j: (bi * ns + j, 0, 0)),
            pl.BlockSpec((1, s, W_A), lambda bi, j: (bi, 0, 0)),
            pl.BlockSpec((nq, W_A, LANES), lambda bi, j: (bi, 0, 0)),
        ] + bias_specs,
        out_specs=pl.BlockSpec((1, ATT_QPS * ATT_QBLK, W_A), lambda bi, j: (bi, j, 0)),
        out_shape=jax.ShapeDtypeStruct((b, s, W_A), BF16),
        scratch_shapes=[pltpu.VMEM((n_units, ATT_WIN, 2 * ATT_QBLK), F32),
                        pltpu.VMEM((n_units, ATT_WIN, 2 * ATT_QBLK), BF16)],
        compiler_params=_params(("arbitrary", "arbitrary")),
        name="attn_prompt",
    )(qat, ka, vat, *([bias] * ATT_QPS))


def _attn_sample_kernel(q_ref, k_ref, v_ref, bias_ref, o_ref):
    def sl(hp):
        return slice(hp * LANES, (hp + 1) * LANES)

    def store(hp, val):
        o_ref[0, :, sl(hp)] = val

    _attend_pairs(
        lambda hp: q_ref[0, :, sl(hp)],
        lambda hp: k_ref[0, :, sl(hp)],
        lambda hp: v_ref[0, :, sl(hp)],
        lambda hd: bias_ref[hd],
        store)


def _attention_sample(qa, k_all, v_all, table, n_cache):
    b, n, _ = qa.shape
    nk = k_all.shape[1]
    dist = jnp.arange(n)[:, None] + n_cache - jnp.arange(nk)[None, :]
    bias = table[:, jnp.clip(dist, -REL_CLIP, REL_CLIP) + REL_CLIP].astype(F32)
    return pl.pallas_call(
        _attn_sample_kernel,
        grid=(b,),
        in_specs=[
            pl.BlockSpec((1, n, W_A), lambda bi: (bi, 0, 0)),
            pl.BlockSpec((1, nk, W_A), lambda bi: (bi, 0, 0)),
            pl.BlockSpec((1, nk, W_A), lambda bi: (bi, 0, 0)),
            pl.BlockSpec((N_HEADS_A, n, nk), lambda bi: (0, 0, 0)),
        ],
        out_specs=pl.BlockSpec((1, n, W_A), lambda bi: (bi, 0, 0)),
        out_shape=jax.ShapeDtypeStruct((b, n, W_A), BF16),
        compiler_params=_params(("arbitrary",)),
        name="attn_sample",
    )(qa, k_all, v_all, bias)


def _ret_kernel(gt_ref, q_ref, *refs, nb):
    kt_refs = refs[:nb]
    v_ref, gb_ref, s0_ref, dmask_ref, qd_ref, kd_ref, gn_ref, out_ref, state_ref, s_scr, o_scr = refs[nb:]

    @pl.when(pl.program_id(1) == 0)
    def _():
        state_ref[...] = s0_ref[...]

    units = [(bb, hd) for bb in range(nb) for hd in range(N_HEADS_B)]
    qs = lambda hd: slice(hd * DK_B, (hd + 1) * DK_B)
    vs = lambda hd: slice(hd * DV_B, (hd + 1) * DV_B)

    for u, (bb, hd) in enumerate(units):
        q = q_ref[bb, :, qs(hd)]
        kt = kt_refs[bb][0, qs(hd), :]
        s_scr[u] = (jnp.dot(q, kt.astype(BF16), preferred_element_type=F32) * dmask_ref[hd]).astype(BF16)
        o_scr[u] = jnp.dot(q, state_ref[bb, hd].astype(BF16), preferred_element_type=F32) * qd_ref[hd]
    for u, (bb, hd) in enumerate(units):
        v = v_ref[bb, :, vs(hd)]
        o_scr[u] = o_scr[u] + jnp.dot(s_scr[u], v, preferred_element_type=F32)
        kd = (kt_refs[bb][0, qs(hd), :] * kd_ref[hd]).astype(BF16)
        state_ref[bb, hd] = state_ref[bb, hd] * gt_ref[hd] + jnp.dot(kd, v, preferred_element_type=F32)
    for u, (bb, hd) in enumerate(units):
        o = o_scr[u]
        mu = jnp.mean(o, axis=-1, keepdims=True)
        var = jnp.mean(jnp.square(o - mu), axis=-1, keepdims=True)
        rb = (o - mu) * lax.rsqrt(var + EPS) * gn_ref[:, vs(hd)]
        gb = gb_ref[bb, :, vs(hd)]
        out_ref[bb, :, vs(hd)] = (gb * jax.nn.sigmoid(gb) * rb).astype(BF16)


def _retention(qb, kbt, vb, gb, state0, log_g, ret_norm_g, *, t, kt_index, nb=4):
    b, s, _ = qb.shape
    assert b % nb == 0
    nc = s // t
    idx = jnp.arange(t, dtype=F32)
    diff = idx[:, None] - idx[None, :]
    dmask = jnp.where(diff[None] >= 0, jnp.exp(log_g[:, None, None] * jnp.maximum(diff, 0.0)[None]), 0.0)
    q_decay = jnp.exp(log_g[:, None] * (idx[None, :] + 1.0))
    k_decay = jnp.exp(log_g[:, None] * (t - 1.0 - idx[None, :]))
    g_t = jnp.exp(log_g * t)
    qd = jnp.broadcast_to(q_decay[:, :, None], (N_HEADS_B, t, DV_B))
    kd = k_decay[:, None, :]
    const3 = lambda bi, c: (0, 0, 0)
    seq = lambda w: pl.BlockSpec((nb, t, w), lambda bi, c: (bi, c, 0))
    state_spec = pl.BlockSpec((nb, N_HEADS_B, DK_B, DV_B), lambda bi, c: (bi, 0, 0, 0))
    kt_specs = [pl.BlockSpec((1, QK_B, t), functools.partial(lambda bi, c, k: kt_index(nb * bi + k, c), k=k))
                for k in range(nb)]
    return pl.pallas_call(
        functools.partial(_ret_kernel, nb=nb),
        grid=(b // nb, nc),
        in_specs=[pl.BlockSpec(memory_space=pltpu.SMEM), seq(QK_B)] + kt_specs + [
            seq(V_B), seq(V_B), state_spec,
            pl.BlockSpec((N_HEADS_B, t, t), const3),
            pl.BlockSpec((N_HEADS_B, t, DV_B), const3),
            pl.BlockSpec((N_HEADS_B, 1, t), const3),
            pl.BlockSpec((1, V_B), lambda bi, c: (0, 0)),
        ],
        out_specs=(seq(V_B), state_spec),
        out_shape=(jax.ShapeDtypeStruct((b, s, V_B), BF16),
                   jax.ShapeDtypeStruct((b, N_HEADS_B, DK_B, DV_B), F32)),
        scratch_shapes=[pltpu.VMEM((nb * N_HEADS_B, t, t), BF16), pltpu.VMEM((nb * N_HEADS_B, t, DV_B), F32)],
        compiler_params=_params(("arbitrary", "arbitrary")),
        name="retention",
    )(g_t, qb, *([kbt] * nb), vb, gb, state0, dmask, qd, kd, ret_norm_g.reshape(1, V_B))


def _split_bf16(x):
    hi = x.astype(BF16)
    lo = (x - hi.astype(F32)).astype(BF16)
    return hi, lo


def _merge_kernel(x_ref, att_ref, bin_ref, gmix_ref, wgab_ref, wpa_ref, wpb_ref, wout_ref,
                  gffn_ref, wr_hi_ref, wr_lo_ref, br_ref, x1_ref, hx_ref, cls_ref):
    x = x_ref[...]
    h = _rms(x, gmix_ref[...]).astype(BF16)
    gates = jnp.dot(h, wgab_ref[...], preferred_element_type=F32)
    gate_a = gates[:, :D_MODEL]
    gate_b = gates[:, D_MODEL:]
    a = jnp.dot(att_ref[...], wpa_ref[...], preferred_element_type=F32)
    b = jnp.dot(bin_ref[...], wpb_ref[...], preferred_element_type=F32)
    m = jax.nn.sigmoid(gate_a) * a + jax.nn.sigmoid(gate_b) * b
    x1 = x + jnp.dot(m.astype(BF16), wout_ref[...], preferred_element_type=F32)
    x1_ref[...] = x1
    h2 = _rms(x1, gffn_ref[...])
    hx_ref[:, :D_MODEL] = h2

    h_hi, h_lo = _split_bf16(h2)
    nt = (((1,), (1,)), ((), ()))
    lt = (lax.dot_general(wr_hi_ref[...], h_hi, nt, preferred_element_type=F32)
          + lax.dot_general(wr_hi_ref[...], h_lo, nt, preferred_element_type=F32)
          + lax.dot_general(wr_lo_ref[...], h_hi, nt, preferred_element_type=F32)) + br_ref[...]
    tm = lt.shape[1]
    row = lax.broadcasted_iota(jnp.int32, (8, tm), 0)
    lg = lt[0:8, :]
    mg = jnp.max(lg, axis=0, keepdims=True)
    grp = jnp.min(jnp.where(lg == mg, row, 8), axis=0, keepdims=True)
    p_grp = 1.0 / jnp.sum(jnp.exp(lg - mg), axis=0, keepdims=True)
    le = jnp.zeros((8, tm), F32)
    for g in range(N_GROUPS):
        le = jnp.where(grp == g, lt[8 + 8 * g:16 + 8 * g, :], le)
    m0 = jnp.max(le, axis=0, keepdims=True)
    i0 = jnp.min(jnp.where(le == m0, row, 8), axis=0, keepdims=True)
    rest = jnp.where(row == i0, jnp.float32(-jnp.inf), le)
    m1 = jnp.max(rest, axis=0, keepdims=True)
    i1 = jnp.min(jnp.where(rest == m1, row, 8), axis=0, keepdims=True)
    e = jnp.exp(m1 - m0)
    w0 = (1.0 / (1.0 + e)) * p_grp
    w1 = (e / (1.0 + e)) * p_grp
    ea = jnp.minimum(i0, i1)
    eb = jnp.maximum(i0, i1)
    pair = ((ea * (2 * EXPERTS_PER_GROUP - 1 - ea)) >> 1) + (eb - ea - 1)
    cls_ref[...] = jnp.where(row == 0, grp * PAIRS_PER_GROUP + pair, 0)
    wa = jnp.where(i0 < i1, w0, w1)
    wb = jnp.where(i0 < i1, w1, w0)
    wrow = lax.broadcasted_iota(jnp.int32, (LANES, tm), 0)
    wslab = jnp.where(wrow == 0, wa, jnp.where(wrow == 1, wb, 0.0))
    hx_ref[:, D_MODEL:] = wslab.T


def _merge(x2d, att, b_in, lw, *, tm):
    n = x2d.shape[0]
    row = lambda i: (i, 0)
    const = lambda i: (0, 0)
    full = lambda a: pl.BlockSpec(a.shape, const)
    return pl.pallas_call(
        _merge_kernel,
        grid=(n // tm,),
        in_specs=[
            pl.BlockSpec((tm, D_MODEL), row), pl.BlockSpec((tm, W_A), row), pl.BlockSpec((tm, V_B), row),
            full(lw["g_mix"]), full(lw["w_gab"]), full(lw["w_pa"]), full(lw["w_pb"]),
            full(lw["w_out"]), full(lw["g_ffn"]), full(lw["wr_hi"]), full(lw["wr_lo"]), full(lw["b_r"]),
        ],
        out_specs=(pl.BlockSpec((tm, D_MODEL), row), pl.BlockSpec((tm, HX_W), row),
                   pl.BlockSpec((8, tm), lambda i: (0, i))),
        out_shape=(jax.ShapeDtypeStruct((n, D_MODEL), F32), jax.ShapeDtypeStruct((n, HX_W), F32),
                   jax.ShapeDtypeStruct((8, n), jnp.int32)),
        compiler_params=_params(("arbitrary",)),
        name="merge",
    )(x2d, att, b_in, lw["g_mix"], lw["w_gab"], lw["w_pa"], lw["w_pb"], lw["w_out"],
      lw["g_ffn"], lw["wr_hi"], lw["wr_lo"], lw["b_r"])


def _class_onehot(cls_row, base, n_real):
    t = cls_row.shape[1]
    crow = lax.broadcasted_iota(jnp.int32, (CLASS_ROWS, t), 0)
    tok = base + lax.broadcasted_iota(jnp.int32, (CLASS_ROWS, t), 1)
    return (cls_row == crow) & (tok < n_real)


def _rank_kernel(cls_ref, tri_ref, rank_ref, counts_ref, *, n_real):
    i = pl.program_id(0)

    @pl.when(i == 0)
    def _():
        counts_ref[...] = jnp.zeros_like(counts_ref)

    t = tri_ref.shape[0]
    hot = _class_onehot(cls_ref[0:1, :], i * t, n_real)
    incl = jnp.dot(jnp.where(hot, 1.0, 0.0).astype(BF16), tri_ref[...], preferred_element_type=F32)
    carry = counts_ref[:, 0:1]
    rank = jnp.sum(jnp.where(hot, incl + carry, 0.0), axis=0, keepdims=True) - 1.0
    rank_ref[...] = jnp.broadcast_to(rank, rank_ref.shape).astype(jnp.int32)
    counts_ref[...] = counts_ref[...] + incl[:, t - 1:t]


def _slot_kernel(cls_ref, rank_ref, pstart_ref, slot_ref, *, n_real):
    t = cls_ref.shape[1]
    hot = _class_onehot(cls_ref[0:1, :], pl.program_id(0) * t, n_real)
    start = jnp.sum(jnp.where(hot, pstart_ref[...], 0.0), axis=0, keepdims=True)
    slot_ref[...] = jnp.broadcast_to(start.astype(jnp.int32) + rank_ref[0:1, :], slot_ref.shape)


def _routing_plan(cls_all, n_real):
    n_pad = cls_all.shape[1]
    nblk = n_pad // RANK_T
    tri = jnp.asarray(np.triu(np.ones((RANK_T, RANK_T), np.float32)), BF16)
    tok = lambda i: (0, i)
    const = lambda i: (0, 0)
    rank, counts = pl.pallas_call(
        functools.partial(_rank_kernel, n_real=n_real),
        grid=(nblk,),
        in_specs=[pl.BlockSpec((8, RANK_T), tok), pl.BlockSpec((RANK_T, RANK_T), const)],
        out_specs=(pl.BlockSpec((8, RANK_T), tok), pl.BlockSpec((CLASS_ROWS, LANES), const)),
        out_shape=(jax.ShapeDtypeStruct((8, n_pad), jnp.int32), jax.ShapeDtypeStruct((CLASS_ROWS, LANES), F32)),
        compiler_params=_params(("arbitrary",)),
        name="moe_rank",
    )(cls_all, tri)
    counts = counts[:, 0].astype(jnp.int32)
    psizes = ((counts + MOE_TM - 1) // MOE_TM) * MOE_TM
    pends = jnp.cumsum(psizes)
    pstart = (pends - psizes).astype(F32).reshape(CLASS_ROWS, 1)
    slot = pl.pallas_call(
        functools.partial(_slot_kernel, n_real=n_real),
        grid=(nblk,),
        in_specs=[pl.BlockSpec((8, RANK_T), tok), pl.BlockSpec((8, RANK_T), tok),
                  pl.BlockSpec((CLASS_ROWS, 1), const)],
        out_specs=pl.BlockSpec((8, RANK_T), tok),
        out_shape=jax.ShapeDtypeStruct((8, n_pad), jnp.int32),
        compiler_params=_params(("arbitrary",)),
        name="moe_slot",
    )(cls_all, rank, pstart)
    return slot[0], pends


def _dispatch_kernel(slot_ref, slots_ref, hxp_ref, hxs_ref, xs_in_hbm, xs_hbm, buf, sem, *, ns_p, n_s):
    del xs_in_hbm
    i = pl.program_id(0)

    def row_copy(idx_ref, k, r, s):
        return pltpu.make_async_copy(buf.at[s, pl.ds(r, 1)],
                                     xs_hbm.at[pl.ds(idx_ref[0, 0, k * PLAN_T + r], 1)], sem.at[s])

    def wait_block(n, s):
        def body(r, c):
            row_copy(slot_ref, 0, 0, s).wait()
            return c
        lax.fori_loop(0, n, body, 0, unroll=8)

    for k in range(2):
        @pl.when(i >= 1)
        def _():
            wait_block(PLAN_T, k)

        @pl.when(i < ns_p)
        def _():
            buf[k] = hxp_ref[k * PLAN_T:(k + 1) * PLAN_T, :]
            for r in range(PLAN_T):
                row_copy(slot_ref, k, r, k).start()

    @pl.when(i == ns_p)
    def _():
        buf[0, :n_s] = hxs_ref[...]
        for r in range(n_s):
            row_copy(slots_ref, 0, r, 0).start()
        wait_block(n_s, 0)


def _dispatch(slot, hx_p, hx_s, n_slots):
    n_p, n_s = hx_p.shape[0], hx_s.shape[0]
    ns_p = n_p // (2 * PLAN_T)
    assert n_p == ns_p * 2 * PLAN_T and ns_p >= 1 and 0 < n_s <= PLAN_T
    slot_p = slot[:n_p].reshape(ns_p, 1, 2 * PLAN_T)
    slot_s = slot[n_p:n_p + PLAN_T].reshape(1, 1, PLAN_T)
    any_spec = pl.BlockSpec(memory_space=pl.ANY)
    last = ns_p - 1
    return pl.pallas_call(
        functools.partial(_dispatch_kernel, ns_p=ns_p, n_s=n_s),
        grid=(ns_p + 1,),
        in_specs=[pl.BlockSpec((1, 1, 2 * PLAN_T), lambda i: (jnp.minimum(i, last), 0, 0), memory_space=pltpu.SMEM),
                  pl.BlockSpec((1, 1, PLAN_T), lambda i: (0, 0, 0), memory_space=pltpu.SMEM),
                  pl.BlockSpec((2 * PLAN_T, HX_W), lambda i: (jnp.minimum(i, last), 0)),
                  pl.BlockSpec((n_s, HX_W), lambda i: (0, 0)), any_spec],
        out_specs=any_spec,
        out_shape=jax.ShapeDtypeStruct((n_slots, HX_W), F32),
        scratch_shapes=[pltpu.VMEM((2, PLAN_T, HX_W), F32), pltpu.SemaphoreType.DMA((2,))],
        input_output_aliases={4: 0},
        compiler_params=_params(("arbitrary",)),
        name="moe_dispatch",
    )(slot_p, slot_s, hx_p, hx_s, jnp.zeros((n_slots, HX_W), F32))


def _expert_kernel(ea_ref, eb_ref, nv_ref, xs_ref, wga_ref, wua_ref, wda_ref, wgb_ref, wub_ref, wdb_ref, ys_ref,
                   gu_scr, act_scr):
    del ea_ref, eb_ref

    @pl.when(pl.program_id(0) < nv_ref[0])
    def _():
        x = xs_ref[:, :D_MODEL].astype(BF16)
        for k, w_ref in enumerate((wga_ref, wua_ref, wgb_ref, wub_ref)):
            gu_scr[k] = jnp.dot(x, w_ref[0], preferred_element_type=F32)
        for k in range(2):
            g = gu_scr[2 * k]
            act_scr[k] = (g * jax.nn.sigmoid(g) * gu_scr[2 * k + 1]).astype(BF16)
        wa = xs_ref[:, D_MODEL:D_MODEL + 1]
        wb = xs_ref[:, D_MODEL + 1:D_MODEL + 2]
        ys_ref[...] = (jnp.dot(act_scr[0], wda_ref[0], preferred_element_type=F32) * wa
                       + jnp.dot(act_scr[1], wdb_ref[0], preferred_element_type=F32) * wb)

    @pl.when(pl.program_id(0) >= nv_ref[0])
    def _():
        ys_ref[...] = jnp.zeros_like(ys_ref)


def _class_experts():
    ea, eb = [], []
    for g in range(N_GROUPS):
        for a in range(EXPERTS_PER_GROUP):
            for b in range(a + 1, EXPERTS_PER_GROUP):
                ea.append(g * EXPERTS_PER_GROUP + a)
                eb.append(g * EXPERTS_PER_GROUP + b)
    return np.asarray(ea, np.int32), np.asarray(eb, np.int32)


def _experts(xs, pends, w_gate, w_up, w_down):
    n_tiles = xs.shape[0] // MOE_TM
    nv = (pends[N_CLASSES - 1] // MOE_TM).astype(jnp.int32)
    tile_cls = jnp.sum(pends[None, :N_CLASSES] <= (jnp.arange(n_tiles) * MOE_TM)[:, None], axis=1)
    tile_cls = jnp.minimum(tile_cls, N_CLASSES - 1)
    cls_ea, cls_eb = _class_experts()
    tile_ea = jnp.asarray(cls_ea)[tile_cls]
    tile_eb = jnp.asarray(cls_eb)[tile_cls]
    wa_spec = lambda shp: pl.BlockSpec((1,) + shp, lambda i, ea, eb, nv_: (ea[i], 0, 0))
    wb_spec = lambda shp: pl.BlockSpec((1,) + shp, lambda i, ea, eb, nv_: (eb[i], 0, 0))
    up, down = (D_MODEL, D_EXPERT), (D_EXPERT, D_MODEL)
    grid_spec = pltpu.PrefetchScalarGridSpec(
        num_scalar_prefetch=3,
        grid=(n_tiles,),
        in_specs=[pl.BlockSpec((MOE_TM, HX_W), lambda i, ea, eb, nv_: (i, 0)),
                  wa_spec(up), wa_spec(up), wa_spec(down), wb_spec(up), wb_spec(up), wb_spec(down)],
        out_specs=pl.BlockSpec((MOE_TM, D_MODEL), lambda i, ea, eb, nv_: (i, 0)),
        scratch_shapes=[pltpu.VMEM((4, MOE_TM, D_EXPERT), F32), pltpu.VMEM((2, MOE_TM, D_EXPERT), BF16)],
    )
    return pl.pallas_call(
        _expert_kernel,
        grid_spec=grid_spec,
        out_shape=jax.ShapeDtypeStruct((n_tiles * MOE_TM, D_MODEL), F32),
        compiler_params=_params(("arbitrary",)),
        name="moe_experts",
    )(tile_ea, tile_eb, nv.reshape(1), xs, w_gate, w_up, w_down, w_gate, w_up, w_down)


def _ple_kernel(slot_ref, slotn_ref, x1_ref, p_ref, wproj_ref, wgate_ref, gple_ref, gfin_ref, ys_hbm,
                y_ref, ybuf, sem, *, tm):
    i = pl.program_id(0)

    def row_copy(idx_ref, k, r, s):
        return pltpu.make_async_copy(ys_hbm.at[pl.ds(idx_ref[0, 0, k * tm + r], 1)], ybuf.at[s, pl.ds(r, 1)],
                                     sem.at[s])

    def for_rows(fn):
        def body(r, c):
            fn(r)
            return c
        lax.fori_loop(0, tm, body, 0, unroll=8)

    def wait_tile(s):
        for_rows(lambda r: row_copy(slot_ref, 0, 0, s).wait())

    n_groups = 6
    bounds = [g * tm // n_groups for g in range(n_groups + 1)]

    def compute(k, s, idx_ref, k_next, s_next):
        def issue(g):
            for r in range(bounds[g], bounds[g + 1]):
                row_copy(idx_ref, k_next, r, s_next).start(priority=r % 2)

        rows = slice(k * tm, (k + 1) * tm)
        x2 = x1_ref[rows, :] + ybuf[s]
        issue(0)
        h = _rms(x2, gple_ref[...]).astype(BF16)
        issue(1)
        proj = jnp.dot(p_ref[rows, :].astype(BF16), wproj_ref[...], preferred_element_type=F32)
        issue(2)
        gate = jnp.dot(h, wgate_ref[...], preferred_element_type=F32)
        issue(3)
        x3 = x2 + proj * jax.nn.sigmoid(gate)
        issue(4)
        y_ref[rows, :] = _rms(x3, gfin_ref[...])
        issue(5)

    @pl.when(i == 0)
    def _():
        for_rows(lambda r: row_copy(slot_ref, 0, r, 0).start())

    wait_tile(0)
    compute(0, 0, slot_ref, 1, 1)
    wait_tile(1)
    compute(1, 1, slotn_ref, 0, 0)

    @pl.when(i == pl.num_programs(0) - 1)
    def _():
        wait_tile(0)


def _ple(x1, ys, slot, p2d, lw, g_final, *, tm):
    n = x1.shape[0]
    ns = n // (2 * tm)
    assert n == ns * 2 * tm
    slot3 = slot.reshape(ns, 1, 2 * tm)
    row = lambda i: (i, 0)
    const = lambda i: (0, 0)
    idx_spec = lambda f: pl.BlockSpec((1, 1, 2 * tm), f, memory_space=pltpu.SMEM)
    return pl.pallas_call(
        functools.partial(_ple_kernel, tm=tm),
        grid=(ns,),
        in_specs=[
            idx_spec(lambda i: (i, 0, 0)), idx_spec(lambda i: (jnp.minimum(i + 1, ns - 1), 0, 0)),
            pl.BlockSpec((2 * tm, D_MODEL), row), pl.BlockSpec((2 * tm, D_PLE), row),
            pl.BlockSpec((D_PLE, D_MODEL), const), pl.BlockSpec((D_MODEL, D_MODEL), const),
            pl.BlockSpec((1, D_MODEL), const), pl.BlockSpec((1, D_MODEL), const),
            pl.BlockSpec(memory_space=pl.ANY),
        ],
        out_specs=pl.BlockSpec((2 * tm, D_MODEL), row),
        out_shape=jax.ShapeDtypeStruct((n, D_MODEL), F32),
        scratch_shapes=[pltpu.VMEM((2, tm, D_MODEL), F32), pltpu.SemaphoreType.DMA((2,))],
        compiler_params=_params(("arbitrary",)),
        name="ple",
    )(slot3, slot3, x1, p2d, lw["w_ple_proj"], lw["w_ple_gate"], lw["g_ple"], g_final.reshape(1, D_MODEL), ys)


def _layer_weights(i, norm_mix_g, w_in, ret_norm_g, w_proj_a, w_proj_b, w_out, norm_ffn_g,
                   w_router_group, b_router_group, w_router_expert, b_router_expert,
                   w_gate_e, w_up_e, w_down_e, norm_ple_g, w_ple_gate, w_ple_proj):
    w = w_in[i]
    o = _IN_OFFS
    cols = lambda k: w[:, o[k]:o[k + 1]]
    w_main = jnp.concatenate([cols(0), cols(1), cols(2), cols(3), cols(5), cols(6)], axis=1).astype(BF16)
    wr = jnp.zeros((ROUTER_ROWS, D_MODEL), F32)
    wr = wr.at[:N_GROUPS].set(w_router_group[i].T).at[8:].set(w_router_expert[i].T)
    br = jnp.full((ROUTER_ROWS,), NEG_BIG, F32)
    br = br.at[:N_GROUPS].set(b_router_group[i].astype(F32)).at[8:].set(b_router_expert[i].astype(F32))
    wr_hi, wr_lo = _split_bf16(wr)
    return {
        "g_mix": norm_mix_g[i].reshape(1, D_MODEL), "w_main": w_main, "w_kt": cols(4).T.astype(BF16),
        "w_qvt": jnp.concatenate([cols(0).T, cols(2).T], axis=0).astype(BF16),
        "w_gab": w[:, o[7]:o[9]].astype(BF16),
        "ret_norm_g": ret_norm_g[i], "w_pa": w_proj_a[i].astype(BF16), "w_pb": w_proj_b[i].astype(BF16),
        "w_out": w_out[i].astype(BF16), "g_ffn": norm_ffn_g[i].reshape(1, D_MODEL),
        "wr_hi": wr_hi, "wr_lo": wr_lo, "b_r": br.reshape(ROUTER_ROWS, 1),
        "w_gate": w_gate_e[i].astype(BF16), "w_up": w_up_e[i].astype(BF16), "w_down": w_down_e[i].astype(BF16),
        "g_ple": norm_ple_g[i].reshape(1, D_MODEL), "w_ple_gate": w_ple_gate[i].astype(BF16),
        "w_ple_proj": w_ple_proj[i].astype(BF16),
    }


def _moe(hx_p, cls_p, hx_s, cls_s, lw):
    n_p, n_s = hx_p.shape[0], hx_s.shape[0]
    n_real = n_p + n_s
    n_pad = pl.cdiv(n_p + PLAN_T, RANK_T) * RANK_T
    cls_all = jnp.concatenate([cls_p, cls_s, jnp.zeros((8, n_pad - n_real), jnp.int32)], axis=1)
    slot, pends = _routing_plan(cls_all, n_real)
    n_slots = (pl.cdiv(n_real, MOE_TM) + N_CLASSES) * MOE_TM
    xs = _dispatch(slot, hx_p, hx_s, n_slots)
    ys = _experts(xs, pends, lw["w_gate"], lw["w_up"], lw["w_down"])
    return ys, slot[:n_p], slot[n_p:n_real]


def kernel(x_prompt, x_sample, cache_k_a, cache_v_a, state_ret, p_prompt, p_sample, norm_mix_g, w_in, rel_bias, ret_norm_g, w_proj_a, w_proj_b, w_out, norm_ffn_g, w_router_group, b_router_group, w_router_expert, b_router_expert, w_gate_e, w_up_e, w_down_e, norm_ple_g, w_ple_gate, w_ple_proj, final_norm_g):
    depth = w_in.shape[0]
    assert depth == 1, "the final norm is fused into the last layer; deeper stacks are not supported"
    bp, sp, _ = x_prompt.shape
    bs, ss, _ = x_sample.shape
    keep = min(WINDOW_A, sp)
    n_cache = cache_k_a.shape[2]
    log_g = jnp.log(1.0 - 2.0 ** (-5.0 - jnp.arange(N_HEADS_B, dtype=F32)))
    i = 0
    lw = _layer_weights(i, norm_mix_g, w_in, ret_norm_g, w_proj_a, w_proj_b, w_out, norm_ffn_g,
                        w_router_group, b_router_group, w_router_expert, b_router_expert,
                        w_gate_e, w_up_e, w_down_e, norm_ple_g, w_ple_gate, w_ple_proj)

    tm = 512
    assert sp % tm == 0 and keep == tm and sp >= ATT_WIN
    t_ret = 128
    xp2 = x_prompt.reshape(bp * sp, D_MODEL)
    qa, ka, va, qb, kbt, vb, gb, ka32, va32 = _project(
        xp2, jnp.arange(sp), lw["g_mix"], lw["w_main"], lw["w_kt"], lw["w_qvt"], tm=tm,
        tiles_per_keep=sp // tm, feature_major_qv=True)
    r3 = lambda a: a.reshape(bp, sp, a.shape[-1])
    att = _attention_prompt(qa, r3(ka), va, rel_bias[i], bp)
    per_tile = tm // t_ret
    tiles_per_b = sp // tm
    b_in, s_prompt = _retention(
        r3(qb), kbt, r3(vb), r3(gb), jnp.zeros((bp, N_HEADS_B, DK_B, DV_B), F32), log_g, lw["ret_norm_g"],
        t=t_ret, kt_index=lambda bi, c: (bi * tiles_per_b + c // per_tile, 0, c % per_tile))
    x1_p, hx_p, cls_p = _merge(xp2, att.reshape(bp * sp, W_A), b_in.reshape(bp * sp, V_B), lw, tm=tm)
    k_a_prompt = ka32.reshape(bp, keep, N_HEADS_A, HEAD_DIM_A)
    v_a_prompt = va32.reshape(bp, keep, N_HEADS_A, HEAD_DIM_A)

    ns = bs * ss
    xs2 = x_sample.reshape(ns, D_MODEL)
    pos_s = jnp.tile(PAST_LEN + jnp.arange(ss), bs)
    qa, ka, va, qb, kbt, vb, gb, ka32, va32 = _project(
        xs2, pos_s, lw["g_mix"], lw["w_main"], lw["w_kt"], lw["w_qvt"], tm=ns, tiles_per_keep=1,
        feature_major_qv=False)
    r3 = lambda a: a.reshape(bs, ss, a.shape[-1])
    k_all = jnp.concatenate([cache_k_a[i].reshape(bs, n_cache, W_A).astype(BF16), r3(ka)], axis=1)
    v_all = jnp.concatenate([cache_v_a[i].reshape(bs, n_cache, W_A).astype(BF16), r3(va)], axis=1)
    att = _attention_sample(r3(qa), k_all, v_all, rel_bias[i], n_cache)
    kbt_s = kbt.reshape(QK_B, bs, ss).transpose(1, 0, 2)
    b_in, s_sample = _retention(
        r3(qb), kbt_s, r3(vb), r3(gb), state_ret[i].astype(F32), log_g, lw["ret_norm_g"],
        t=ss, kt_index=lambda bi, c: (bi, 0, 0))
    x1_s, hx_s, cls_s = _merge(xs2, att.reshape(ns, W_A), b_in.reshape(ns, V_B), lw, tm=ns)

    ys, slot_p, slot_s = _moe(hx_p, cls_p, hx_s, cls_s, lw)
    y_prompt = _ple(x1_p, ys, slot_p, p_prompt[i].reshape(bp * sp, D_PLE), lw, final_norm_g, tm=tm)
    y_sample = _ple(x1_s, ys, slot_s, p_sample[i].reshape(ns, D_PLE), lw, final_norm_g, tm=ns // 2)
    k_a_sample = ka32.reshape(bs, ss, N_HEADS_A, HEAD_DIM_A)
    v_a_sample = va32.reshape(bs, ss, N_HEADS_A, HEAD_DIM_A)

    return (y_prompt.reshape(bp, sp, D_MODEL), y_sample.reshape(bs, ss, D_MODEL),
            k_a_prompt[None], v_a_prompt[None], s_prompt[None],
            k_a_sample[None], v_a_sample[None], s_sample.astype(state_ret.dtype)[None])
```
